```python
import jax, jax.numpy as jnp
from jax import lax
import numpy as np

D_MODEL = 2048
BATCH = 4
SEQ = 4096
DEPTH = 1

MEM_LEN = 256
GRID_W = 64
HEAD_DIM = 64
NA_HEADS = 16
NA_ROWS = 8
NA_COLS = 16
WIN_Q_HEADS = 16
WIN_KV_HEADS = 4
WINDOW = 128
ROT_DIM = HEAD_DIM // 4
ROPE_THETA = 500000.0
CROSS_HEADS = 4
CROSS_HEAD_DIM = 128
N_EXPERTS = 32
TOP_K = 4
D_FF = D_MODEL
SWIGLU_LIMIT = 7.0
SWIGLU_ALPHA = 1.702
EXPERT_BLOCK = 128
NORM_EPS = 1e-5
NEG_INF = -1e30

NA_WIDTH = NA_HEADS * HEAD_DIM
WIN_Q_WIDTH = WIN_Q_HEADS * HEAD_DIM
WIN_KV_WIDTH = WIN_KV_HEADS * HEAD_DIM
CROSS_WIDTH = CROSS_HEADS * CROSS_HEAD_DIM
D_IN = 3 * NA_WIDTH + WIN_Q_WIDTH + 2 * WIN_KV_WIDTH + 2 * D_MODEL

kernel_name = "hybrid_natten_swa_moe_encoder"


def _split_points():
    widths = [NA_WIDTH, NA_WIDTH, NA_WIDTH, WIN_Q_WIDTH, WIN_KV_WIDTH, WIN_KV_WIDTH, D_MODEL]
    pts, acc = [], 0
    for w in widths:
        acc += w
        pts.append(acc)
    return pts


def rmsnorm(x, g):
    x32 = x.astype(jnp.float32)
    y = x32 * lax.rsqrt(jnp.mean(x32 * x32, axis=-1, keepdims=True) + NORM_EPS)
    return (y * g.astype(jnp.float32)).astype(x.dtype)


def partial_rope(x, pos):
    half = ROT_DIM // 2
    inv = ROPE_THETA ** (-(jnp.arange(half, dtype=jnp.float32) * 2.0 / ROT_DIM))
    ang = pos.astype(jnp.float32)[:, None] * inv[None, :]
    cos = jnp.cos(ang)[None, :, None, :]
    sin = jnp.sin(ang)[None, :, None, :]
    xf = x.astype(jnp.float32)
    x1 = xf[..., :half]
    x2 = xf[..., half:ROT_DIM]
    out = jnp.concatenate([x1 * cos - x2 * sin, x2 * cos + x1 * sin, xf[..., ROT_DIM:]], axis=-1)
    return out.astype(x.dtype)


def neighbourhood_attention(q, k, v, rpb):
    b, s, h, dh = q.shape
    rows = s // GRID_W
    kr = min(NA_ROWS, rows)
    kc = NA_COLS
    scale = dh ** -0.5
    q = q.reshape(b, rows, GRID_W, h, dh)
    k = k.reshape(b, rows, GRID_W, h, dh)
    v = v.reshape(b, rows, GRID_W, h, dh)
    cols = jnp.arange(GRID_W)
    col_start = jnp.clip(cols - kc // 2, 0, GRID_W - kc)
    col_idx = col_start[:, None] + jnp.arange(kc)[None, :]
    dc = col_idx - cols[:, None] + (NA_COLS - 1)

    def row_block(r):
        rs = jnp.clip(r - kr // 2, 0, rows - kr)
        kb = lax.dynamic_slice_in_dim(k, rs, kr, axis=1)[:, :, col_idx]
        vb = lax.dynamic_slice_in_dim(v, rs, kr, axis=1)[:, :, col_idx]
        qr = lax.dynamic_index_in_dim(q, r, axis=1, keepdims=False)
        sc = jnp.einsum('bchd,brckhd->bhcrk', qr, kb).astype(jnp.float32) * scale
        dr = rs + jnp.arange(kr) - r + (NA_ROWS - 1)
        bias = rpb[:, dr[None, :, None], dc[:, None, :]].astype(jnp.float32)
        sc = sc + bias[None]
        p = jax.nn.softmax(sc.reshape(b, h, GRID_W, kr * kc), axis=-1)
        p = p.reshape(b, h, GRID_W, kr, kc).astype(v.dtype)
        return jnp.einsum('bhcrk,brckhd->bchd', p, vb)

    o = lax.map(row_block, jnp.arange(rows))
    return o.transpose(1, 0, 2, 3, 4).reshape(b, s, h * dh)


def windowed_gqa(q, k, v, sinks):
    b, s, hq, dh = q.shape
    hkv = k.shape[2]
    grp = hq // hkv
    nb = s // WINDOW
    scale = dh ** -0.5
    pad = ((0, 0), (WINDOW, WINDOW), (0, 0), (0, 0))
    kp = jnp.pad(k, pad).reshape(b, nb + 2, WINDOW, hkv, dh)
    vp = jnp.pad(v, pad).reshape(b, nb + 2, WINDOW, hkv, dh)
    kw = jnp.concatenate([kp[:, :-2], kp[:, 1:-1], kp[:, 2:]], axis=2)
    vw = jnp.concatenate([vp[:, :-2], vp[:, 1:-1], vp[:, 2:]], axis=2)
    qb = q.reshape(b, nb, WINDOW, hkv, grp, dh)
    sc = jnp.einsum('bnqhgd,bnkhd->bnhgqk', qb, kw).astype(jnp.float32) * scale
    blk = jnp.arange(nb)[:, None] * WINDOW
    qpos = blk + jnp.arange(WINDOW)[None, :]
    kpos = blk - WINDOW + jnp.arange(3 * WINDOW)[None, :]
    valid = (jnp.abs(qpos[:, :, None] - kpos[:, None, :]) <= WINDOW) & (kpos[:, None, :] >= 0) & (kpos[:, None, :] < s)
    sc = jnp.where(valid[None, :, None, None], sc, NEG_INF)
    sink = sinks.astype(jnp.float32).reshape(hkv, grp)[None, None, :, :, None, None]
    m = jnp.maximum(jnp.max(sc, axis=-1, keepdims=True), sink)
    p = jnp.exp(sc - m)
    denom = jnp.sum(p, axis=-1, keepdims=True) + jnp.exp(sink - m)
    p = (p / denom).astype(v.dtype)
    o = jnp.einsum('bnhgqk,bnkhd->bnqhgd', p, vw)
    return o.reshape(b, s, hq * dh)


def cross_attention(h, mem_n, w_cq, w_ckv, w_co):
    b, s, _ = h.shape
    m = mem_n.shape[1]
    q = (h @ w_cq).reshape(b, s, CROSS_HEADS, CROSS_HEAD_DIM)
    kv = mem_n @ w_ckv
    k = kv[..., :CROSS_WIDTH].reshape(b, m, CROSS_HEADS, CROSS_HEAD_DIM)
    v = kv[..., CROSS_WIDTH:].reshape(b, m, CROSS_HEADS, CROSS_HEAD_DIM)
    sc = jnp.einsum('bshd,bmhd->bhsm', q, k).astype(jnp.float32) * (CROSS_HEAD_DIM ** -0.5)
    p = jax.nn.softmax(sc, axis=-1).astype(v.dtype)
    o = jnp.einsum('bhsm,bmhd->bshd', p, v).reshape(b, s, CROSS_WIDTH)
    return o @ w_co


def clamped_swiglu(g, u):
    g = jnp.minimum(g, SWIGLU_LIMIT)
    u = jnp.clip(u, -SWIGLU_LIMIT, SWIGLU_LIMIT)
    return (u + 1.0) * (g * jax.nn.sigmoid(SWIGLU_ALPHA * g))


def moe(h, w_router, b_router, w_gate, b_gate, w_up, b_up, w_down, b_down):
    b, s, d = h.shape
    t = b * s
    ht = h.reshape(t, d)
    logits = (ht @ w_router + b_router).astype(jnp.float32)
    top_val, top_idx = lax.top_k(logits, TOP_K)
    top_w = jax.nn.softmax(top_val, axis=-1).astype(h.dtype)
    n_assign = t * TOP_K
    flat_e = top_idx.reshape(n_assign)
    flat_tok = jnp.repeat(jnp.arange(t, dtype=jnp.int32), TOP_K)
    flat_w = top_w.reshape(n_assign)
    order = jnp.argsort(flat_e)
    sorted_e = flat_e[order]
    counts = jnp.bincount(flat_e, length=N_EXPERTS)
    padded = (counts + EXPERT_BLOCK - 1) // EXPERT_BLOCK * EXPERT_BLOCK
    pad_end = jnp.cumsum(padded)
    pad_start = pad_end - padded
    start = jnp.cumsum(counts) - counts
    dest = pad_start[sorted_e] + (jnp.arange(n_assign) - start[sorted_e])
    n_blocks = n_assign // EXPERT_BLOCK + N_EXPERTS
    n_rows = n_blocks * EXPERT_BLOCK
    row_tok = jnp.full((n_rows,), t, dtype=jnp.int32).at[dest].set(flat_tok[order])
    row_w = jnp.zeros((n_rows,), h.dtype).at[dest].set(flat_w[order])
    block_e = jnp.minimum(jnp.searchsorted(pad_end, jnp.arange(n_blocks) * EXPERT_BLOCK, side='right'), N_EXPERTS - 1)
    x_pad = jnp.concatenate([ht, jnp.zeros((1, d), h.dtype)], axis=0)
    xb = x_pad[row_tok].reshape(n_blocks, EXPERT_BLOCK, d)

    def expert_block(args):
        xe, e = args
        g = xe @ w_gate[e] + b_gate[e]
        u = xe @ w_up[e] + b_up[e]
        return clamped_swiglu(g, u) @ w_down[e] + b_down[e]

    yb = lax.map(expert_block, (xb, block_e)).reshape(n_rows, d)
    out = jnp.zeros((t + 1, d), h.dtype).at[row_tok].add(yb * row_w[:, None])[:t]
    return out.reshape(b, s, d)


def setup_inputs(seed: int = 0) -> dict:
    key = jax.random.key(seed)
    ks = jax.random.split(key, 26)
    f32 = jnp.float32

    def nrm(k, shape, scale):
        return jax.random.normal(k, shape, f32) * scale

    def gain(k, shape):
        return 1.0 + 0.1 * jax.random.normal(k, shape, f32)

    L = DEPTH
    return {
        "x": nrm(ks[0], (BATCH, SEQ, D_MODEL), 1.0),
        "mem": nrm(ks[1], (BATCH, MEM_LEN, D_MODEL), 1.0),
        "g_mix": gain(ks[2], (L, D_MODEL)),
        "w_in": nrm(ks[3], (L, D_MODEL, D_IN), D_MODEL ** -0.5),
        "rpb_na": nrm(ks[4], (L, NA_HEADS, 2 * NA_ROWS - 1, 2 * NA_COLS - 1), 0.5),
        "sinks": nrm(ks[5], (L, WIN_Q_HEADS), 1.0),
        "w_na_o": nrm(ks[6], (L, NA_WIDTH, D_MODEL), NA_WIDTH ** -0.5),
        "w_win_o": nrm(ks[7], (L, WIN_Q_WIDTH, D_MODEL), WIN_Q_WIDTH ** -0.5),
        "w_out": nrm(ks[8], (L, D_MODEL, D_MODEL), D_MODEL ** -0.5),
        "g_cross": gain(ks[9], (L, D_MODEL)),
        "g_mem": gain(ks[10], (L, D_MODEL)),
        "w_cq": nrm(ks[11], (L, D_MODEL, CROSS_WIDTH), D_MODEL ** -0.5),
        "w_ckv": nrm(ks[12], (L, D_MODEL, 2 * CROSS_WIDTH), D_MODEL ** -0.5),
        "w_co": nrm(ks[13], (L, CROSS_WIDTH, D_MODEL), CROSS_WIDTH ** -0.5),
        "g_moe": gain(ks[14], (L, D_MODEL)),
        "w_router": nrm(ks[15], (L, D_MODEL, N_EXPERTS), D_MODEL ** -0.5),
        "b_router": nrm(ks[16], (L, N_EXPERTS), 0.01),
        "w_gate": nrm(ks[17], (L, N_EXPERTS, D_MODEL, D_FF), D_MODEL ** -0.5),
        "b_gate": nrm(ks[18], (L, N_EXPERTS, D_FF), 0.01),
        "w_up": nrm(ks[19], (L, N_EXPERTS, D_MODEL, D_FF), D_MODEL ** -0.5),
        "b_up": nrm(ks[20], (L, N_EXPERTS, D_FF), 0.01),
        "w_down": nrm(ks[21], (L, N_EXPERTS, D_FF, D_MODEL), D_FF ** -0.5),
        "b_down": nrm(ks[22], (L, N_EXPERTS, D_MODEL), 0.01),
        "g_final": gain(ks[23], (D_MODEL,)),
    }


def reference(x, mem, g_mix, w_in, rpb_na, sinks, w_na_o, w_win_o, w_out, g_cross, g_mem, w_cq, w_ckv, w_co, g_moe, w_router, b_router, w_gate, b_gate, w_up, b_up, w_down, b_down, g_final):
    b, s, _ = x.shape
    pos = jnp.arange(s)
    splits = _split_points()
    for l in range(DEPTH):
        h = rmsnorm(x, g_mix[l])
        proj = h @ w_in[l]
        qa, ka, va, qb, kb, vb, ga, gb = jnp.split(proj, splits, axis=-1)
        ya = neighbourhood_attention(
            qa.reshape(b, s, NA_HEADS, HEAD_DIM),
            ka.reshape(b, s, NA_HEADS, HEAD_DIM),
            va.reshape(b, s, NA_HEADS, HEAD_DIM),
            rpb_na[l])
        qb = partial_rope(qb.reshape(b, s, WIN_Q_HEADS, HEAD_DIM), pos)
        kb = partial_rope(kb.reshape(b, s, WIN_KV_HEADS, HEAD_DIM), pos)
        yb = windowed_gqa(qb, kb, vb.reshape(b, s, WIN_KV_HEADS, HEAD_DIM), sinks[l])
        merged = jax.nn.sigmoid(ga) * (ya @ w_na_o[l]) + jax.nn.sigmoid(gb) * (yb @ w_win_o[l])
        x = x + merged @ w_out[l]
        x = x + cross_attention(rmsnorm(x, g_cross[l]), rmsnorm(mem, g_mem[l]), w_cq[l], w_ckv[l], w_co[l])
        x = x + moe(rmsnorm(x, g_moe[l]), w_router[l], b_router[l], w_gate[l], b_gate[l], w_up[l], b_up[l], w_down[l], b_down[l])
    return rmsnorm(x, g_final)
```

```python
import functools
from typing import NamedTuple

import jax
import jax.numpy as jnp
from jax import lax
from jax.experimental import pallas as pl
from jax.experimental.pallas import tpu as pltpu

F32 = jnp.float32
BF16 = jnp.bfloat16

GRID_W = 64
HEAD_DIM = 64
NA_HEADS = 16
NA_ROWS = 8
NA_COLS = 16
WIN_Q_HEADS = 16
WIN_KV_HEADS = 4
WINDOW = 128
ROT_DIM = HEAD_DIM // 4
ROPE_THETA = 500000.0
CROSS_HEADS = 4
CROSS_HEAD_DIM = 128
N_EXPERTS = 32
TOP_K = 4
SWIGLU_LIMIT = 7.0
SWIGLU_ALPHA = 1.702
NORM_EPS = 1e-5
NEG_INF = -1e30

LANES = 128
VMEM_LIMIT = 56 * 1024 * 1024

NA_WIDTH = NA_HEADS * HEAD_DIM
WIN_Q_WIDTH = WIN_Q_HEADS * HEAD_DIM
WIN_KV_WIDTH = WIN_KV_HEADS * HEAD_DIM
CROSS_WIDTH = CROSS_HEADS * CROSS_HEAD_DIM


class _Cols(NamedTuple):
    ga: int
    gb: int
    qa: int
    ka: int
    va: int
    qb: int
    kb: int
    vb: int
    end: int


def _cols(d):
    widths = [d, d, NA_WIDTH, NA_WIDTH, NA_WIDTH, WIN_Q_WIDTH, WIN_KV_WIDTH, WIN_KV_WIDTH]
    offs = [0]
    for w in widths:
        offs.append(offs[-1] + w)
    return _Cols(*offs)


PROJ_TM = 1024
PROJ_TN = 512
MERGE_TM = 256
CROSS_TM = 256
MOE_TM = 1024
MOE_TSB = 256
MOE_TF = 512
COMBINE_TM = 256


def _cparams(sem):
    return pltpu.CompilerParams(dimension_semantics=sem, vmem_limit_bytes=VMEM_LIMIT)


def _rms(x, g):
    ms = jnp.mean(x * x, axis=-1, keepdims=True)
    return x * lax.rsqrt(ms + NORM_EPS) * g


def _dot(a, b):
    return jnp.dot(a, b, preferred_element_type=F32)


def _dot_nt(a, b):
    return lax.dot_general(a, b, (((1,), (1,)), ((), ())), preferred_element_type=F32)


def _rope_chunk(x, c, sa, sb):
    return x * c + pltpu.roll(x, LANES - ROT_DIM // 2, 1) * sa + pltpu.roll(x, ROT_DIM // 2, 1) * sb


def _inproj_kernel(x_ref, g_ref, w_ref, c_ref, sa_ref, sb_ref, o_ref, h_ref, *, cols):
    j = pl.program_id(1)
    tn = o_ref.shape[1]
    nchunk = tn // LANES
    scale = HEAD_DIM ** -0.5

    @pl.when(j == 0)
    def _():
        h_ref[...] = _rms(x_ref[...], g_ref[...]).astype(BF16)

    acc = _dot(h_ref[...], w_ref[...])

    j_qa, j_ka, j_qb, j_kb = cols.qa // tn, cols.ka // tn, cols.qb // tn, cols.kb // tn
    kb_chunks = WIN_KV_WIDTH // LANES

    @pl.when(j < j_qa)
    def _():
        o_ref[...] = jax.nn.sigmoid(acc).astype(BF16)

    @pl.when((j >= j_qa) & (j < j_ka))
    def _():
        o_ref[...] = (acc * scale).astype(BF16)

    @pl.when((j >= j_ka) & (j < j_qb))
    def _():
        o_ref[...] = acc.astype(BF16)

    @pl.when((j >= j_qb) & (j < j_kb))
    def _():
        c, sa, sb = c_ref[...] * scale, sa_ref[...] * scale, sb_ref[...] * scale
        for k in range(nchunk):
            sl = slice(k * LANES, (k + 1) * LANES)
            o_ref[:, sl] = _rope_chunk(acc[:, sl], c, sa, sb).astype(BF16)

    @pl.when(j >= j_kb)
    def _():
        c, sa, sb = c_ref[...], sa_ref[...], sb_ref[...]
        for k in range(nchunk):
            sl = slice(k * LANES, (k + 1) * LANES)
            if k < kb_chunks:
                o_ref[:, sl] = _rope_chunk(acc[:, sl], c, sa, sb).astype(BF16)
            else:
                o_ref[:, sl] = acc[:, sl].astype(BF16)


def _in_proj(x2d, g, w_bf, rope_c, rope_sa, rope_sb, seq):
    t, d = x2d.shape
    n = w_bf.shape[1]
    cols = _cols(d)
    tm, tn = PROJ_TM, PROJ_TN
    assert t % tm == 0 and n % tn == 0 and seq % tm == 0 and n == cols.end
    assert all(c % tn == 0 for c in (cols.qa, cols.ka, cols.qb, cols.kb)) and cols.end == cols.kb + tn
    sblocks = seq // tm
    rope_spec = pl.BlockSpec((tm, LANES), lambda i, j: (i % sblocks, 0))
    return pl.pallas_call(
        functools.partial(_inproj_kernel, cols=cols),
        grid=(t // tm, n // tn),
        in_specs=[
            pl.BlockSpec((tm, d), lambda i, j: (i, 0)),
            pl.BlockSpec((1, d), lambda i, j: (0, 0)),
            pl.BlockSpec((d, tn), lambda i, j: (0, j)),
            rope_spec, rope_spec, rope_spec,
        ],
        out_specs=pl.BlockSpec((tm, tn), lambda i, j: (i, j)),
        out_shape=jax.ShapeDtypeStruct((t, n), BF16),
        scratch_shapes=[pltpu.VMEM((tm, d), BF16)],
        compiler_params=_cparams(("parallel", "arbitrary")),
        name="in_proj",
    )(x2d, g, w_bf, rope_c, rope_sa, rope_sb)


def _natten_kernel(q_ref, k_ref, v_ref, b_ref, o_ref, *, rows):
    kr = NA_ROWS
    lane = lax.broadcasted_iota(jnp.int32, (GRID_W, LANES), 1)
    first = lane < HEAD_DIM

    def body(r, carry):
        rs = jnp.clip(r - kr // 2, 0, rows - kr)
        dr0 = rs - r + (NA_ROWS - 1)
        q0 = pl.multiple_of(r * GRID_W, GRID_W)
        k0 = pl.multiple_of(rs * GRID_W, GRID_W)
        q2 = q_ref[pl.ds(q0, GRID_W), :]
        k2 = k_ref[pl.ds(k0, kr * GRID_W), :]
        v2 = v_ref[pl.ds(k0, kr * GRID_W), :]
        zero = jnp.zeros_like(q2)
        qs = jnp.concatenate([jnp.where(first, q2, zero), jnp.where(first, zero, q2)], axis=0)
        s = _dot_nt(qs, k2) + b_ref[dr0]
        m = jnp.max(s, axis=-1, keepdims=True)
        p = jnp.exp(s - m)
        l = jnp.sum(p, axis=-1, keepdims=True)
        o = _dot(p.astype(BF16), v2) * (1.0 / l)
        o_ref[pl.ds(q0, GRID_W), :] = jnp.where(first, o[:GRID_W], o[GRID_W:]).astype(BF16)
        return carry

    lax.fori_loop(0, rows, body, 0)


def _natten(proj, bias_tab, cols, batch, seq):
    npairs = NA_HEADS // 2
    rows = seq // GRID_W
    assert rows >= NA_ROWS
    kq, kk, kv = cols.qa // LANES, cols.ka // LANES, cols.va // LANES
    blk = (seq, LANES)
    return pl.pallas_call(
        functools.partial(_natten_kernel, rows=rows),
        grid=(batch, npairs),
        in_specs=[
            pl.BlockSpec(blk, lambda b, p: (b, kq + p)),
            pl.BlockSpec(blk, lambda b, p: (b, kk + p)),
            pl.BlockSpec(blk, lambda b, p: (b, kv + p)),
            pl.BlockSpec((None, NA_ROWS, 2 * GRID_W, NA_ROWS * GRID_W), lambda b, p: (p, 0, 0, 0)),
        ],
        out_specs=pl.BlockSpec(blk, lambda b, p: (b, p)),
        out_shape=jax.ShapeDtypeStruct((batch * seq, NA_WIDTH), BF16),
        compiler_params=_cparams(("parallel", "parallel")),
        name="natten",
    )(proj, proj, proj, bias_tab)


def _natten_bias_table(rpb):
    cols = jnp.arange(GRID_W)
    col_start = jnp.clip(cols - NA_COLS // 2, 0, GRID_W - NA_COLS)
    kc = jnp.arange(GRID_W)
    inwin = (kc[None, :] >= col_start[:, None]) & (kc[None, :] < col_start[:, None] + NA_COLS)
    dc = jnp.clip(kc[None, :] - cols[:, None] + (NA_COLS - 1), 0, 2 * NA_COLS - 2)
    dr = jnp.arange(NA_ROWS)[:, None] + jnp.arange(NA_ROWS)[None, :]
    tab = rpb.astype(F32)[:, dr[:, :, None, None], dc[None, None, :, :]]
    tab = jnp.where(inwin[None, None, None], tab, NEG_INF)
    tab = tab.transpose(0, 1, 3, 2, 4).reshape(NA_HEADS, NA_ROWS, GRID_W, NA_ROWS * GRID_W)
    tab = tab.reshape(NA_HEADS // 2, 2, NA_ROWS, GRID_W, NA_ROWS * GRID_W).transpose(0, 2, 1, 3, 4)
    return tab.reshape(NA_HEADS // 2, NA_ROWS, 2 * GRID_W, NA_ROWS * GRID_W)


def _swa_kernel(sink_ref, q_ref, kp_ref, kc_ref, kn_ref, vp_ref, vc_ref, vn_ref, o_ref, *, seq):
    kp = pl.program_id(1)
    n = pl.program_id(2)
    w = WINDOW
    grp = WIN_Q_HEADS // WIN_KV_HEADS
    k3 = jnp.concatenate([kp_ref[...], kc_ref[...], kn_ref[...]], axis=0)
    v3 = jnp.concatenate([vp_ref[...], vc_ref[...], vn_ref[...]], axis=0)
    qi = lax.broadcasted_iota(jnp.int32, (2 * w, 3 * w), 0)
    qi = jnp.where(qi >= w, qi - w, qi)
    kj = lax.broadcasted_iota(jnp.int32, (2 * w, 3 * w), 1)
    kpos = n * w - w + kj
    valid = (jnp.abs(kj - w - qi) <= w) & (kpos >= 0) & (kpos < seq)
    lane = lax.broadcasted_iota(jnp.int32, (w, LANES), 1)
    first = lane < HEAD_DIM
    rowhalf = lax.broadcasted_iota(jnp.int32, (2 * w, 1), 0) < w
    for c in range(grp):
        sl = slice(c * LANES, (c + 1) * LANES)
        q2 = q_ref[:, sl]
        zero = jnp.zeros_like(q2)
        qs = jnp.concatenate([jnp.where(first, q2, zero), jnp.where(first, zero, q2)], axis=0)
        s = jnp.where(valid, _dot_nt(qs, k3), NEG_INF)
        sink0 = sink_ref[(2 * kp) * grp + c]
        sink1 = sink_ref[(2 * kp + 1) * grp + c]
        sink = jnp.where(rowhalf, sink0, sink1)
        m = jnp.maximum(jnp.max(s, axis=-1, keepdims=True), sink)
        p = jnp.exp(s - m)
        denom = jnp.sum(p, axis=-1, keepdims=True) + jnp.exp(sink - m)
        o = _dot(p.astype(BF16), v3) * (1.0 / denom)
        o_ref[:, sl] = jnp.where(first, o[:w], o[w:]).astype(BF16)


def _swa(proj, sinks, cols, batch, seq):
    w = WINDOW
    nb = seq // w
    grp = WIN_Q_HEADS // WIN_KV_HEADS
    qw = grp * LANES
    assert cols.qb % qw == 0
    cq, ck, cv = cols.qb // qw, cols.kb // LANES, cols.vb // LANES

    def kv_spec(col0, shift):
        def imap(b, kp, n):
            return (b * nb + jnp.clip(n + shift, 0, nb - 1), col0 + kp)
        return pl.BlockSpec((w, LANES), imap)

    return pl.pallas_call(
        functools.partial(_swa_kernel, seq=seq),
        grid=(batch, WIN_KV_HEADS // 2, nb),
        in_specs=[
            pl.BlockSpec(memory_space=pltpu.SMEM),
            pl.BlockSpec((w, qw), lambda b, kp, n: (b * nb + n, cq + kp)),
            kv_spec(ck, -1), kv_spec(ck, 0), kv_spec(ck, 1),
            kv_spec(cv, -1), kv_spec(cv, 0), kv_spec(cv, 1),
        ],
        out_specs=pl.BlockSpec((w, qw), lambda b, kp, n: (b * nb + n, kp)),
        out_shape=jax.ShapeDtypeStruct((batch * seq, WIN_Q_WIDTH), BF16),
        compiler_params=_cparams(("parallel", "parallel", "parallel")),
        name="swa",
    )(sinks, proj, proj, proj, proj, proj, proj, proj)


def _swa_head_perm():
    grp = WIN_Q_HEADS // WIN_KV_HEADS
    perm = []
    for kp in range(WIN_KV_HEADS // 2):
        for c in range(grp):
            for half in range(2):
                perm.append(grp * (2 * kp + half) + c)
    return perm


def _merge_kernel(ya_ref, yb_ref, ga_ref, gb_ref, x_ref, wna_ref, wwin_ref, wout_ref, o_ref):
    a = _dot(ya_ref[...], wna_ref[...])
    b = _dot(yb_ref[...], wwin_ref[...])
    merged = ga_ref[...].astype(F32) * a + gb_ref[...].astype(F32) * b
    o_ref[...] = x_ref[...] + _dot(merged.astype(BF16), wout_ref[...])


def _resident(shape):
    return pl.BlockSpec(shape, lambda i: (0,) * len(shape), pipeline_mode=pl.Buffered(1))


def _merge(ya, yb, proj, x2d, wna, wwin, wout):
    t, d = x2d.shape
    tm = MERGE_TM
    return pl.pallas_call(
        _merge_kernel,
        grid=(t // tm,),
        in_specs=[
            pl.BlockSpec((tm, NA_WIDTH), lambda i: (i, 0)),
            pl.BlockSpec((tm, WIN_Q_WIDTH), lambda i: (i, 0)),
            pl.BlockSpec((tm, d), lambda i: (i, 0)),
            pl.BlockSpec((tm, d), lambda i: (i, 1)),
            pl.BlockSpec((tm, d), lambda i: (i, 0)),
            _resident(wna.shape), _resident(wwin.shape), _resident(wout.shape),
        ],
        out_specs=pl.BlockSpec((tm, d), lambda i: (i, 0)),
        out_shape=jax.ShapeDtypeStruct((t, d), F32),
        compiler_params=_cparams(("parallel",)),
        name="merge",
    )(ya, yb, proj, proj, x2d, wna, wwin, wout)


def _memkv_kernel(m_ref, g_ref, w_ref, o_ref):
    o_ref[...] = _dot(_rms(m_ref[...], g_ref[...]).astype(BF16), w_ref[...]).astype(BF16)


def _memkv(mem2d, g, w_bf, mem_len):
    t, d = mem2d.shape
    n = w_bf.shape[1]
    return pl.pallas_call(
        _memkv_kernel,
        grid=(t // mem_len,),
        in_specs=[
            pl.BlockSpec((mem_len, d), lambda i: (i, 0)),
            pl.BlockSpec((1, d), lambda i: (0, 0)),
            pl.BlockSpec((d, n), lambda i: (0, 0)),
        ],
        out_specs=pl.BlockSpec((mem_len, n), lambda i: (i, 0)),
        out_shape=jax.ShapeDtypeStruct((t, n), BF16),
        compiler_params=_cparams(("parallel",)),
        name="memkv",
    )(mem2d, g, w_bf)


def _cross_kernel(x_ref, gc_ref, wcq_ref, k_ref, v_ref, wco_ref, gm_ref, wrh_ref, wrl_ref, br_ref,
                  x2_ref, h_ref, lg_ref):
    x = x_ref[...]
    q = _dot(_rms(x, gc_ref[...]).astype(BF16), wcq_ref[...]).astype(BF16)
    scale = CROSS_HEAD_DIM ** -0.5
    outs = []
    for h in range(CROSS_HEADS):
        sl = slice(h * CROSS_HEAD_DIM, (h + 1) * CROSS_HEAD_DIM)
        s = _dot_nt(q[:, sl], k_ref[:, sl]) * scale
        m = jnp.max(s, axis=-1, keepdims=True)
        p = jnp.exp(s - m)
        l = jnp.sum(p, axis=-1, keepdims=True)
        outs.append((_dot(p.astype(BF16), v_ref[:, sl]) * (1.0 / l)).astype(BF16))
    o = jnp.concatenate(outs, axis=-1)
    x2 = x + _dot(o, wco_ref[...])
    x2_ref[...] = x2
    h = _rms(x2, gm_ref[...])
    h_hi = h.astype(BF16)
    h_lo = (h - h_hi.astype(F32)).astype(BF16)
    h_ref[...] = h_hi
    lg_ref[...] = (_dot(h_hi, wrh_ref[...]) + _dot(h_hi, wrl_ref[...]) + _dot(h_lo, wrh_ref[...])
                   + br_ref[...])


def _cross(x1, gc, wcq, memkv, wco, gm, wr_hi, wr_lo, br, seq, mem_len):
    t, d = x1.shape
    tm = CROSS_TM
    ne = wr_hi.shape[1]
    per_b = seq // tm
    row = lambda i: (i, 0)
    return pl.pallas_call(
        _cross_kernel,
        grid=(t // tm,),
        in_specs=[
            pl.BlockSpec((tm, d), row),
            _resident((1, d)),
            _resident(wcq.shape),
            pl.BlockSpec((mem_len, CROSS_WIDTH), lambda i: (i // per_b, 0)),
            pl.BlockSpec((mem_len, CROSS_WIDTH), lambda i: (i // per_b, 1)),
            _resident(wco.shape),
            _resident((1, d)),
            _resident(wr_hi.shape), _resident(wr_lo.shape),
            _resident((1, ne)),
        ],
        out_specs=[
            pl.BlockSpec((tm, d), row),
            pl.BlockSpec((tm, d), row),
            pl.BlockSpec((tm, ne), row),
        ],
        out_shape=[
            jax.ShapeDtypeStruct((t, d), F32),
            jax.ShapeDtypeStruct((t, d), BF16),
            jax.ShapeDtypeStruct((t, ne), F32),
        ],
        compiler_params=_cparams(("parallel",)),
        name="cross",
    )(x1, gc, wcq, memkv, memkv, wco, gm, wr_hi, wr_lo, br)


def _moe_kernel(te_ref, tv_ref, x_ref, wg_ref, wu_ref, wd_ref, bg_ref, bu_ref, bd_ref, o_ref,
                a_ref, w1_ref, w2_ref, *, nf):
    i = pl.program_id(0)
    s = pl.program_id(1)
    valid = tv_ref[i]
    tm = x_ref.shape[0]
    tf = wg_ref.shape[1]
    nsb = tm // MOE_TSB

    @pl.when((s < nf) & (valid > 0))
    def _():
        w1_ref[...] = wg_ref[...].astype(BF16)
        w2_ref[...] = wu_ref[...].astype(BF16)
        for sb in range(nsb):
            rows = slice(sb * MOE_TSB, (sb + 1) * MOE_TSB)

            @pl.when(sb * MOE_TSB < valid)
            def _():
                xs = x_ref[rows, :]
                g = _dot(xs, w1_ref[...]) + bg_ref[...]
                u = _dot(xs, w2_ref[...]) + bu_ref[...]
                g = jnp.minimum(g, SWIGLU_LIMIT)
                u = jnp.clip(u, -SWIGLU_LIMIT, SWIGLU_LIMIT)
                a_ref[s, rows, :] = ((u + 1.0) * (g * jax.nn.sigmoid(SWIGLU_ALPHA * g))).astype(BF16)

    @pl.when(s >= nf)
    def _():
        @pl.when(valid > 0)
        def _():
            w1_ref[...] = wd_ref[...].astype(BF16)

        for sb in range(nsb):
            rows = slice(sb * MOE_TSB, (sb + 1) * MOE_TSB)

            @pl.when(sb * MOE_TSB < valid)
            def _():
                acc = bd_ref[...] + _dot(a_ref[0, rows, :], w1_ref[0:tf, :])
                for f in range(1, nf):
                    acc = acc + _dot(a_ref[f, rows, :], w1_ref[f * tf:(f + 1) * tf, :])
                o_ref[rows, :] = acc.astype(BF16)

            @pl.when(sb * MOE_TSB >= valid)
            def _():
                o_ref[rows, :] = jnp.zeros((MOE_TSB, tf), BF16)


def _moe_ffn(tile_e, tile_valid, xs, w_gate, b_gate, w_up, b_up, w_down, b_down):
    n_rows, d = xs.shape
    e, _, dff = w_gate.shape
    tm, tf = MOE_TM, MOE_TF
    assert d == dff, "the weight staging buffers are shared between the up and down projections"
    nf = dff // tf
    n_tiles = n_rows // tm
    up = lambda i, s, te, tv: (te[i], 0, jnp.minimum(s, nf - 1))
    dn = lambda i, s, te, tv: (te[i], 0, jnp.maximum(s - nf, 0))
    grid_spec = pltpu.PrefetchScalarGridSpec(
        num_scalar_prefetch=2,
        grid=(n_tiles, 2 * nf),
        in_specs=[
            pl.BlockSpec((tm, d), lambda i, s, te, tv: (i, 0)),
            pl.BlockSpec((None, d, tf), up),
            pl.BlockSpec((None, d, tf), up),
            pl.BlockSpec((None, dff, tf), dn),
            pl.BlockSpec((None, 1, tf), up),
            pl.BlockSpec((None, 1, tf), up),
            pl.BlockSpec((None, 1, tf), dn),
        ],
        out_specs=pl.BlockSpec((tm, tf), lambda i, s, te, tv: (i, jnp.maximum(s - nf, 0))),
        scratch_shapes=[
            pltpu.VMEM((nf, tm, tf), BF16),
            pltpu.VMEM((d, tf), BF16),
            pltpu.VMEM((d, tf), BF16),
        ],
    )
    return pl.pallas_call(
        functools.partial(_moe_kernel, nf=nf),
        grid_spec=grid_spec,
        out_shape=jax.ShapeDtypeStruct((n_rows, d), BF16),
        compiler_params=_cparams(("arbitrary", "arbitrary")),
        name="moe_ffn",
    )(tile_e, tile_valid, xs, w_gate, w_up, w_down,
      b_gate.reshape(e, 1, dff), b_up.reshape(e, 1, dff), b_down.reshape(e, 1, d))


def _combine_kernel(x_ref, y_ref, w_ref, g_ref, o_ref, *, final):
    d = x_ref.shape[1]
    acc = x_ref[...]
    w = w_ref[...]
    for k in range(TOP_K):
        acc = acc + w[:, k:k + 1] * y_ref[:, k * d:(k + 1) * d].astype(F32)
    o_ref[...] = _rms(acc, g_ref[...]) if final else acc


def _combine(x2, yg, top_w, g, final):
    t, d = x2.shape
    tm = COMBINE_TM
    return pl.pallas_call(
        functools.partial(_combine_kernel, final=final),
        grid=(t // tm,),
        in_specs=[
            pl.BlockSpec((tm, d), lambda i: (i, 0)),
            pl.BlockSpec((tm, TOP_K * d), lambda i: (i, 0)),
            pl.BlockSpec((tm, TOP_K), lambda i: (i, 0)),
            pl.BlockSpec((1, d), lambda i: (0, 0)),
        ],
        out_specs=pl.BlockSpec((tm, d), lambda i: (i, 0)),
        out_shape=jax.ShapeDtypeStruct((t, d), F32),
        compiler_params=_cparams(("parallel",)),
        name="combine",
    )(x2, yg, top_w, g)


def _rope_tables(seq):
    half = ROT_DIM // 2
    inv = ROPE_THETA ** (-(jnp.arange(half, dtype=F32) * 2.0 / ROT_DIM))
    ang = jnp.arange(seq, dtype=F32)[:, None] * inv[None, :]
    cos, sin = jnp.cos(ang), jnp.sin(ang)
    ones = jnp.ones((seq, HEAD_DIM - ROT_DIM), F32)
    zeros = jnp.zeros((seq, HEAD_DIM - ROT_DIM), F32)
    zh = jnp.zeros((seq, half), F32)
    c = jnp.concatenate([cos, cos, ones], axis=1)
    sa = jnp.concatenate([-sin, zh, zeros], axis=1)
    sb = jnp.concatenate([zh, sin, zeros], axis=1)
    rep = LANES // HEAD_DIM
    return jnp.tile(c, (1, rep)), jnp.tile(sa, (1, rep)), jnp.tile(sb, (1, rep))


def _route(logits, tm):
    t = logits.shape[0]
    top_val, top_idx = lax.top_k(logits, TOP_K)
    top_w = jax.nn.softmax(top_val, axis=-1)
    n_assign = t * TOP_K
    flat_e = top_idx.reshape(n_assign).astype(jnp.int32)
    order = jnp.argsort(flat_e)
    sorted_e = flat_e[order]
    counts = jnp.bincount(flat_e, length=N_EXPERTS).astype(jnp.int32)
    padded = (counts + tm - 1) // tm * tm
    pad_end = jnp.cumsum(padded)
    pad_start = pad_end - padded
    start = jnp.cumsum(counts) - counts
    dest = pad_start[sorted_e] + (jnp.arange(n_assign, dtype=jnp.int32) - start[sorted_e])
    n_tiles = n_assign // tm + N_EXPERTS
    n_rows = n_tiles * tm
    row_tok = jnp.full((n_rows,), t, jnp.int32).at[dest].set((order // TOP_K).astype(jnp.int32))
    pos = jnp.zeros((n_assign,), jnp.int32).at[order].set(dest)
    tile_row0 = jnp.arange(n_tiles, dtype=jnp.int32) * tm
    tile_e = jnp.minimum(jnp.searchsorted(pad_end, tile_row0, side="right"), N_EXPERTS - 1).astype(jnp.int32)
    tile_valid = jnp.clip(counts[tile_e] - (tile_row0 - pad_start[tile_e]), 0, tm).astype(jnp.int32)
    last_e = jnp.max(jnp.where(tile_valid > 0, tile_e, 0))
    tile_e = jnp.where(tile_valid > 0, tile_e, last_e).astype(jnp.int32)
    return top_w, row_tok, pos, tile_e, tile_valid


def kernel(x, mem, g_mix, w_in, rpb_na, sinks, w_na_o, w_win_o, w_out, g_cross, g_mem, w_cq, w_ckv, w_co,
           g_moe, w_router, b_router, w_gate, b_gate, w_up, b_up, w_down, b_down, g_final):
    b, s, d = x.shape
    mem_len = mem.shape[1]
    t = b * s
    depth = w_in.shape[0]
    xc = x.reshape(t, d)
    mem2d = mem.reshape(b * mem_len, d)
    rope_c, rope_sa, rope_sb = _rope_tables(s)

    cols = _cols(d)
    yb_rows = jnp.concatenate([h * HEAD_DIM + jnp.arange(HEAD_DIM) for h in _swa_head_perm()])
    n_attn = cols.end - cols.qa
    src_cols = jnp.concatenate([n_attn + jnp.arange(2 * d), jnp.arange(3 * NA_WIDTH),
                                3 * NA_WIDTH + yb_rows, 3 * NA_WIDTH + WIN_Q_WIDTH + jnp.arange(2 * WIN_KV_WIDTH)])

    for l in range(depth):
        w_in_l = w_in[l][:, src_cols].astype(BF16)
        proj = _in_proj(xc, g_mix[l].reshape(1, d), w_in_l, rope_c, rope_sa, rope_sb, s)
        ya = _natten(proj, _natten_bias_table(rpb_na[l]), cols, b, s)
        yb = _swa(proj, sinks[l].astype(F32), cols, b, s)
        x1 = _merge(ya, yb, proj, xc, w_na_o[l].astype(BF16), w_win_o[l][yb_rows].astype(BF16),
                    w_out[l].astype(BF16))
        memkv = _memkv(mem2d, g_mem[l].reshape(1, d), w_ckv[l].astype(BF16), mem_len)
        wr_hi = w_router[l].astype(BF16)
        wr_lo = (w_router[l] - wr_hi.astype(F32)).astype(BF16)
        x2, h3, logits = _cross(x1, g_cross[l].reshape(1, d), w_cq[l].astype(BF16), memkv, w_co[l].astype(BF16),
                                g_moe[l].reshape(1, d), wr_hi, wr_lo, b_router[l].reshape(1, -1), s, mem_len)
        top_w, row_tok, pos, tile_e, tile_valid = _route(logits, MOE_TM)
        h_pad = jnp.concatenate([h3, jnp.zeros((1, d), BF16)], axis=0)
        xs = h_pad[row_tok]
        ys = _moe_ffn(tile_e, tile_valid, xs, w_gate[l], b_gate[l], w_up[l], b_up[l], w_down[l], b_down[l])
        yg = ys[pos].reshape(t, TOP_K * d)
        xc = _combine(x2, yg, top_w, g_final.reshape(1, d), final=(l + 1 == depth))
    return xc.reshape(b, s, d)
```

```python
import functools
from typing import NamedTuple

import jax
import jax.numpy as jnp
from jax import lax
from jax.experimental import pallas as pl
from jax.experimental.pallas import tpu as pltpu

F32 = jnp.float32
BF16 = jnp.bfloat16
U32 = jnp.uint32

GRID_W = 64
HEAD_DIM = 64
NA_HEADS = 16
NA_ROWS = 8
NA_COLS = 16
WIN_Q_HEADS = 16
WIN_KV_HEADS = 4
WINDOW = 128
ROT_DIM = HEAD_DIM // 4
ROPE_THETA = 500000.0
CROSS_HEADS = 4
CROSS_HEAD_DIM = 128
N_EXPERTS = 32
TOP_K = 4
SWIGLU_LIMIT = 7.0
SWIGLU_ALPHA = 1.702
NORM_EPS = 1e-5
NEG_INF = -1e30

LANES = 128
SUBLANES = 8
VMEM_LIMIT = 56 * 1024 * 1024

NA_WIDTH = NA_HEADS * HEAD_DIM
WIN_Q_WIDTH = WIN_Q_HEADS * HEAD_DIM
WIN_KV_WIDTH = WIN_KV_HEADS * HEAD_DIM
CROSS_WIDTH = CROSS_HEADS * CROSS_HEAD_DIM


class _Cols(NamedTuple):
    ga: int
    gb: int
    qa: int
    ka: int
    va: int
    qb: int
    kb: int
    vb: int
    end: int


def _cols(d):
    widths = [d, d, NA_WIDTH, NA_WIDTH, NA_WIDTH, WIN_Q_WIDTH, WIN_KV_WIDTH, WIN_KV_WIDTH]
    offs = [0]
    for w in widths:
        offs.append(offs[-1] + w)
    return _Cols(*offs)


PROJ_TM = 1024
PROJ_TN = 512
NA_UNROLL = 4
MERGE_TM = 256
CROSS_TM = 256
MOE_TM = 2048
MOE_TSB = 256
MOE_TF = 256
COMBINE_TM = 256


def _cparams(sem):
    return pltpu.CompilerParams(dimension_semantics=sem, vmem_limit_bytes=VMEM_LIMIT)


def _rms(x, g):
    ms = jnp.mean(x * x, axis=-1, keepdims=True)
    return x * lax.rsqrt(ms + NORM_EPS) * g


def _dot(a, b):
    return jnp.dot(a, b, preferred_element_type=F32)


def _dot_nt(a, b):
    return lax.dot_general(a, b, (((1,), (1,)), ((), ())), preferred_element_type=F32)


def _pack_halves(v):
    half = v.shape[1] // 2
    lo = lax.shift_right_logical(pltpu.bitcast(v[:, :half], U32), jnp.uint32(16))
    hi = pltpu.bitcast(v[:, half:], U32) & jnp.uint32(0xFFFF0000)
    return hi | lo


def _unpack_lo(p):
    return pltpu.bitcast(lax.shift_left(p, jnp.uint32(16)), F32)


def _unpack_hi(p):
    return pltpu.bitcast(p & jnp.uint32(0xFFFF0000), F32)


def _store_token_tiles(ref, packed):
    m, w = packed.shape
    n_sl = w // LANES
    for s in range(n_sl):
        ref[pl.ds(s, m, stride=n_sl), :] = packed[:, s * LANES:(s + 1) * LANES]


def _rope_chunk(x, c, sa, sb):
    return x * c + pltpu.roll(x, LANES - ROT_DIM // 2, 1) * sa + pltpu.roll(x, ROT_DIM // 2, 1) * sb


def _inproj_kernel(x_ref, g_ref, w_ref, c_ref, sa_ref, sb_ref, o_ref, h_ref, *, cols):
    j = pl.program_id(1)
    tn = o_ref.shape[1]
    nchunk = tn // LANES
    scale = HEAD_DIM ** -0.5

    @pl.when(j == 0)
    def _():
        h_ref[...] = _rms(x_ref[...], g_ref[...]).astype(BF16)

    acc = _dot(h_ref[...], w_ref[...])

    j_qa, j_ka, j_qb, j_kb = cols.qa // tn, cols.ka // tn, cols.qb // tn, cols.kb // tn
    kb_chunks = WIN_KV_WIDTH // LANES

    @pl.when(j < j_qa)
    def _():
        o_ref[...] = jax.nn.sigmoid(acc).astype(BF16)

    @pl.when((j >= j_qa) & (j < j_ka))
    def _():
        o_ref[...] = (acc * scale).astype(BF16)

    @pl.when((j >= j_ka) & (j < j_qb))
    def _():
        o_ref[...] = acc.astype(BF16)

    @pl.when((j >= j_qb) & (j < j_kb))
    def _():
        c, sa, sb = c_ref[...] * scale, sa_ref[...] * scale, sb_ref[...] * scale
        for k in range(nchunk):
            sl = slice(k * LANES, (k + 1) * LANES)
            o_ref[:, sl] = _rope_chunk(acc[:, sl], c, sa, sb).astype(BF16)

    @pl.when(j >= j_kb)
    def _():
        c, sa, sb = c_ref[...], sa_ref[...], sb_ref[...]
        for k in range(nchunk):
            sl = slice(k * LANES, (k + 1) * LANES)
            if k < kb_chunks:
                o_ref[:, sl] = _rope_chunk(acc[:, sl], c, sa, sb).astype(BF16)
            else:
                o_ref[:, sl] = acc[:, sl].astype(BF16)


def _in_proj(x2d, g, w_bf, rope_c, rope_sa, rope_sb, seq):
    t, d = x2d.shape
    n = w_bf.shape[1]
    cols = _cols(d)
    tm, tn = PROJ_TM, PROJ_TN
    assert t % tm == 0 and n % tn == 0 and seq % tm == 0 and n == cols.end
    assert all(c % tn == 0 for c in (cols.qa, cols.ka, cols.qb, cols.kb)) and cols.end == cols.kb + tn
    sblocks = seq // tm
    rope_spec = pl.BlockSpec((tm, LANES), lambda i, j: (i % sblocks, 0))
    return pl.pallas_call(
        functools.partial(_inproj_kernel, cols=cols),
        grid=(t // tm, n // tn),
        in_specs=[
            pl.BlockSpec((tm, d), lambda i, j: (i, 0)),
            pl.BlockSpec((1, d), lambda i, j: (0, 0)),
            pl.BlockSpec((d, tn), lambda i, j: (0, j)),
            rope_spec, rope_spec, rope_spec,
        ],
        out_specs=pl.BlockSpec((tm, tn), lambda i, j: (i, j)),
        out_shape=jax.ShapeDtypeStruct((t, n), BF16),
        scratch_shapes=[pltpu.VMEM((tm, d), BF16)],
        compiler_params=_cparams(("parallel", "arbitrary")),
        name="in_proj",
    )(x2d, g, w_bf, rope_c, rope_sa, rope_sb)


def _natten_kernel(q_ref, k_ref, v_ref, b_ref, o_ref, *, rows):
    kr = NA_ROWS
    lane = lax.broadcasted_iota(jnp.int32, (GRID_W, LANES), 1)
    first = lane < HEAD_DIM

    def body(r, carry):
        rs = jnp.clip(r - kr // 2, 0, rows - kr)
        dr0 = rs - r + (NA_ROWS - 1)
        q0 = pl.multiple_of(r * GRID_W, GRID_W)
        k0 = pl.multiple_of(rs * GRID_W, GRID_W)
        q2 = q_ref[pl.ds(q0, GRID_W), :]
        k2 = k_ref[pl.ds(k0, kr * GRID_W), :]
        v2 = v_ref[pl.ds(k0, kr * GRID_W), :]
        zero = jnp.zeros_like(q2)
        qs = jnp.concatenate([jnp.where(first, q2, zero), jnp.where(first, zero, q2)], axis=0)
        s = _dot_nt(qs, k2) + b_ref[dr0]
        m = jnp.max(s, axis=-1, keepdims=True)
        p = jnp.exp(s - m)
        l = jnp.sum(p, axis=-1, keepdims=True)
        o = _dot(p.astype(BF16), v2) * (1.0 / l)
        o_ref[pl.ds(q0, GRID_W), :] = jnp.where(first, o[:GRID_W], o[GRID_W:]).astype(BF16)
        return carry

    lax.fori_loop(0, rows, body, 0, unroll=NA_UNROLL)


def _natten(proj, bias_tab, cols, batch, seq):
    npairs = NA_HEADS // 2
    rows = seq // GRID_W
    assert rows >= NA_ROWS
    kq, kk, kv = cols.qa // LANES, cols.ka // LANES, cols.va // LANES
    blk = (seq, LANES)
    return pl.pallas_call(
        functools.partial(_natten_kernel, rows=rows),
        grid=(batch, npairs),
        in_specs=[
            pl.BlockSpec(blk, lambda b, p: (b, kq + p)),
            pl.BlockSpec(blk, lambda b, p: (b, kk + p)),
            pl.BlockSpec(blk, lambda b, p: (b, kv + p)),
            pl.BlockSpec((None, NA_ROWS, 2 * GRID_W, NA_ROWS * GRID_W), lambda b, p: (p, 0, 0, 0)),
        ],
        out_specs=pl.BlockSpec(blk, lambda b, p: (b, p)),
        out_shape=jax.ShapeDtypeStruct((batch * seq, NA_WIDTH), BF16),
        compiler_params=_cparams(("parallel", "parallel")),
        name="natten",
    )(proj, proj, proj, bias_tab)


def _natten_bias_table(rpb):
    cols = jnp.arange(GRID_W)
    col_start = jnp.clip(cols - NA_COLS // 2, 0, GRID_W - NA_COLS)
    kc = jnp.arange(GRID_W)
    inwin = (kc[None, :] >= col_start[:, None]) & (kc[None, :] < col_start[:, None] + NA_COLS)
    dc = jnp.clip(kc[None, :] - cols[:, None] + (NA_COLS - 1), 0, 2 * NA_COLS - 2)
    dr = jnp.arange(NA_ROWS)[:, None] + jnp.arange(NA_ROWS)[None, :]
    tab = rpb.astype(F32)[:, dr[:, :, None, None], dc[None, None, :, :]]
    tab = jnp.where(inwin[None, None, None], tab, NEG_INF)
    tab = tab.transpose(0, 1, 3, 2, 4).reshape(NA_HEADS, NA_ROWS, GRID_W, NA_ROWS * GRID_W)
    tab = tab.reshape(NA_HEADS // 2, 2, NA_ROWS, GRID_W, NA_ROWS * GRID_W).transpose(0, 2, 1, 3, 4)
    return tab.reshape(NA_HEADS // 2, NA_ROWS, 2 * GRID_W, NA_ROWS * GRID_W)


def _swa_kernel(sink_ref, q_ref, kp_ref, kc_ref, kn_ref, vp_ref, vc_ref, vn_ref, o_ref, *, seq):
    kp = pl.program_id(1)
    n = pl.program_id(2)
    w = WINDOW
    grp = WIN_Q_HEADS // WIN_KV_HEADS
    k3 = jnp.concatenate([kp_ref[...], kc_ref[...], kn_ref[...]], axis=0)
    v3 = jnp.concatenate([vp_ref[...], vc_ref[...], vn_ref[...]], axis=0)
    qi = lax.broadcasted_iota(jnp.int32, (2 * w, 3 * w), 0)
    qi = jnp.where(qi >= w, qi - w, qi)
    kj = lax.broadcasted_iota(jnp.int32, (2 * w, 3 * w), 1)
    kpos = n * w - w + kj
    valid = (jnp.abs(kj - w - qi) <= w) & (kpos >= 0) & (kpos < seq)
    lane = lax.broadcasted_iota(jnp.int32, (w, LANES), 1)
    first = lane < HEAD_DIM
    rowhalf = lax.broadcasted_iota(jnp.int32, (2 * w, 1), 0) < w
    for c in range(grp):
        sl = slice(c * LANES, (c + 1) * LANES)
        q2 = q_ref[:, sl]
        zero = jnp.zeros_like(q2)
        qs = jnp.concatenate([jnp.where(first, q2, zero), jnp.where(first, zero, q2)], axis=0)
        s = jnp.where(valid, _dot_nt(qs, k3), NEG_INF)
        sink0 = sink_ref[(2 * kp) * grp + c]
        sink1 = sink_ref[(2 * kp + 1) * grp + c]
        sink = jnp.where(rowhalf, sink0, sink1)
        m = jnp.maximum(jnp.max(s, axis=-1, keepdims=True), sink)
        p = jnp.exp(s - m)
        denom = jnp.sum(p, axis=-1, keepdims=True) + jnp.exp(sink - m)
        o = _dot(p.astype(BF16), v3) * (1.0 / denom)
        o_ref[:, sl] = jnp.where(first, o[:w], o[w:]).astype(BF16)


def _swa(proj, sinks, cols, batch, seq):
    w = WINDOW
    nb = seq // w
    grp = WIN_Q_HEADS // WIN_KV_HEADS
    qw = grp * LANES
    assert cols.qb % qw == 0
    cq, ck, cv = cols.qb // qw, cols.kb // LANES, cols.vb // LANES

    def kv_spec(col0, shift):
        def imap(b, kp, n):
            return (b * nb + jnp.clip(n + shift, 0, nb - 1), col0 + kp)
        return pl.BlockSpec((w, LANES), imap)

    return pl.pallas_call(
        functools.partial(_swa_kernel, seq=seq),
        grid=(batch, WIN_KV_HEADS // 2, nb),
        in_specs=[
            pl.BlockSpec(memory_space=pltpu.SMEM),
            pl.BlockSpec((w, qw), lambda b, kp, n: (b * nb + n, cq + kp)),
            kv_spec(ck, -1), kv_spec(ck, 0), kv_spec(ck, 1),
            kv_spec(cv, -1), kv_spec(cv, 0), kv_spec(cv, 1),
        ],
        out_specs=pl.BlockSpec((w, qw), lambda b, kp, n: (b * nb + n, kp)),
        out_shape=jax.ShapeDtypeStruct((batch * seq, WIN_Q_WIDTH), BF16),
        compiler_params=_cparams(("parallel", "parallel", "parallel")),
        name="swa",
    )(sinks, proj, proj, proj, proj, proj, proj, proj)


def _swa_head_perm():
    grp = WIN_Q_HEADS // WIN_KV_HEADS
    perm = []
    for kp in range(WIN_KV_HEADS // 2):
        for c in range(grp):
            for half in range(2):
                perm.append(grp * (2 * kp + half) + c)
    return perm


def _merge_kernel(ya_ref, yb_ref, ga_ref, gb_ref, x_ref, wna_ref, wwin_ref, wout_ref, o_ref):
    a = _dot(ya_ref[...], wna_ref[...])
    b = _dot(yb_ref[...], wwin_ref[...])
    merged = ga_ref[...].astype(F32) * a + gb_ref[...].astype(F32) * b
    o_ref[...] = x_ref[...] + _dot(merged.astype(BF16), wout_ref[...])


def _resident(shape):
    return pl.BlockSpec(shape, lambda i: (0,) * len(shape), pipeline_mode=pl.Buffered(1))


def _merge(ya, yb, proj, x2d, wna, wwin, wout):
    t, d = x2d.shape
    tm = MERGE_TM
    return pl.pallas_call(
        _merge_kernel,
        grid=(t // tm,),
        in_specs=[
            pl.BlockSpec((tm, NA_WIDTH), lambda i: (i, 0)),
            pl.BlockSpec((tm, WIN_Q_WIDTH), lambda i: (i, 0)),
            pl.BlockSpec((tm, d), lambda i: (i, 0)),
            pl.BlockSpec((tm, d), lambda i: (i, 1)),
            pl.BlockSpec((tm, d), lambda i: (i, 0)),
            _resident(wna.shape), _resident(wwin.shape), _resident(wout.shape),
        ],
        out_specs=pl.BlockSpec((tm, d), lambda i: (i, 0)),
        out_shape=jax.ShapeDtypeStruct((t, d), F32),
        compiler_params=_cparams(("parallel",)),
        name="merge",
    )(ya, yb, proj, proj, x2d, wna, wwin, wout)


def _memkv_kernel(m_ref, g_ref, w_ref, o_ref):
    o_ref[...] = _dot(_rms(m_ref[...], g_ref[...]).astype(BF16), w_ref[...]).astype(BF16)


def _memkv(mem2d, g, w_bf, mem_len):
    t, d = mem2d.shape
    n = w_bf.shape[1]
    return pl.pallas_call(
        _memkv_kernel,
        grid=(t // mem_len,),
        in_specs=[
            pl.BlockSpec((mem_len, d), lambda i: (i, 0)),
            pl.BlockSpec((1, d), lambda i: (0, 0)),
            pl.BlockSpec((d, n), lambda i: (0, 0)),
        ],
        out_specs=pl.BlockSpec((mem_len, n), lambda i: (i, 0)),
        out_shape=jax.ShapeDtypeStruct((t, n), BF16),
        compiler_params=_cparams(("parallel",)),
        name="memkv",
    )(mem2d, g, w_bf)


def _cross_kernel(x_ref, gc_ref, wcq_ref, k_ref, v_ref, wco_ref, gm_ref, wrh_ref, wrl_ref, br_ref,
                  x2_ref, h_ref, lg_ref):
    x = x_ref[...]
    q = _dot(_rms(x, gc_ref[...]).astype(BF16), wcq_ref[...]).astype(BF16)
    scale = CROSS_HEAD_DIM ** -0.5
    outs = []
    for h in range(CROSS_HEADS):
        sl = slice(h * CROSS_HEAD_DIM, (h + 1) * CROSS_HEAD_DIM)
        s = _dot_nt(q[:, sl], k_ref[:, sl]) * scale
        m = jnp.max(s, axis=-1, keepdims=True)
        p = jnp.exp(s - m)
        l = jnp.sum(p, axis=-1, keepdims=True)
        outs.append((_dot(p.astype(BF16), v_ref[:, sl]) * (1.0 / l)).astype(BF16))
    o = jnp.concatenate(outs, axis=-1)
    x2 = x + _dot(o, wco_ref[...])
    x2_ref[...] = x2
    h = _rms(x2, gm_ref[...])
    h_hi = h.astype(BF16)
    h_lo = (h - h_hi.astype(F32)).astype(BF16)
    _store_token_tiles(h_ref, _pack_halves(h_hi.astype(F32)))
    lg_ref[...] = (_dot(h_hi, wrh_ref[...]) + _dot(h_hi, wrl_ref[...]) + _dot(h_lo, wrh_ref[...])
                   + br_ref[...])


def _cross(x1, gc, wcq, memkv, wco, gm, wr_hi, wr_lo, br, seq, mem_len):
    t, d = x1.shape
    tm = CROSS_TM
    ne = wr_hi.shape[1]
    per_b = seq // tm
    row = lambda i: (i, 0)
    return pl.pallas_call(
        _cross_kernel,
        grid=(t // tm,),
        in_specs=[
            pl.BlockSpec((tm, d), row),
            _resident((1, d)),
            _resident(wcq.shape),
            pl.BlockSpec((mem_len, CROSS_WIDTH), lambda i: (i // per_b, 0)),
            pl.BlockSpec((mem_len, CROSS_WIDTH), lambda i: (i // per_b, 1)),
            _resident(wco.shape),
            _resident((1, d)),
            _resident(wr_hi.shape), _resident(wr_lo.shape),
            _resident((1, ne)),
        ],
        out_specs=[
            pl.BlockSpec((tm, d), row),
            pl.BlockSpec((tm * SUBLANES, LANES), row),
            pl.BlockSpec((tm, ne), row),
        ],
        out_shape=[
            jax.ShapeDtypeStruct((t, d), F32),
            jax.ShapeDtypeStruct((t * SUBLANES, LANES), U32),
            jax.ShapeDtypeStruct((t, ne), F32),
        ],
        compiler_params=_cparams(("parallel",)),
        name="cross",
    )(x1, gc, wcq, memkv, memkv, wco, gm, wr_hi, wr_lo, br)


def _moe_kernel(te_ref, tv_ref, tc_ref, tok_ref, h_hbm, wg_ref, wu_ref, wd_ref, bg_ref, bu_ref, bd_ref, y_hbm,
                raw_ref, xb_ref, a_ref, w1_ref, w2_ref, ys_ref, gsem, osem, *, nf, n_tiles):
    i = pl.program_id(0)
    s = pl.program_id(1)
    tsb = MOE_TSB
    rpt = SUBLANES
    tm = xb_ref.shape[0]
    tf = wg_ref.shape[1]
    half = xb_ref.shape[1] // 2
    valid = tv_ref[i]
    n_sb = (valid + tsb - 1) // tsb

    def row_copy(tok, k):
        src = h_hbm.at[pl.ds(pl.multiple_of(tok * rpt, rpt), rpt)]
        dst = raw_ref.at[pl.ds(pl.multiple_of(k * rpt, rpt), rpt)]
        return pltpu.make_async_copy(src, dst, gsem)

    def issue_gather(t):
        v = tv_ref[t]
        c0 = tc_ref[t]

        def real(k, c):
            row_copy(tok_ref[c0 + k], k).start()
            return c

        def pad(k, c):
            row_copy(0, k).start()
            return c

        lax.fori_loop(0, v, real, 0)
        lax.fori_loop(v, (v + tsb - 1) // tsb * tsb, pad, 0)

    def block_rows(ref, sb):
        return ref.at[pl.ds(pl.multiple_of(sb * (tsb * rpt), tsb * rpt), tsb * rpt)]

    def out_copy(t):
        dst = y_hbm.at[pl.ds(pl.multiple_of(t * (tm * rpt), tm * rpt), tm * rpt)]
        return pltpu.make_async_copy(ys_ref, dst, osem)

    @pl.when((i == 0) & (s == 0))
    def _():
        ys_ref[...] = jnp.zeros(ys_ref.shape, U32)
        issue_gather(0)

    @pl.when(s == 0)
    def _():
        def wait_rows(sb, c):
            pltpu.make_async_copy(block_rows(h_hbm, 0), block_rows(raw_ref, sb), gsem).wait()
            return c

        lax.fori_loop(0, n_sb, wait_rows, 0)

        def unpack(sb, c):
            r0 = pl.multiple_of(sb * tsb, tsb)
            for sl in range(rpt):
                p = raw_ref[pl.ds(r0 * rpt + sl, tsb, stride=rpt), :]
                xb_ref[pl.ds(r0, tsb), sl * LANES:(sl + 1) * LANES] = _unpack_lo(p).astype(BF16)
                xb_ref[pl.ds(r0, tsb), half + sl * LANES:half + (sl + 1) * LANES] = _unpack_hi(p).astype(BF16)
            return c

        lax.fori_loop(0, n_sb, unpack, 0)

    @pl.when((s == 1) & (i + 1 < n_tiles))
    def _():
        issue_gather(i + 1)

    def for_row_blocks(fn):
        def pair(b, c):
            fn(pl.multiple_of(b * (2 * tsb), 2 * tsb), 2 * tsb)
            return c

        lax.fori_loop(0, n_sb // 2, pair, 0)

        @pl.when(n_sb % 2 == 1)
        def _():
            fn(pl.multiple_of((n_sb - 1) * tsb, tsb), tsb)

    @pl.when((s < nf) & (valid > 0))
    def _():
        w1_ref[...] = wg_ref[...].astype(BF16)
        w2_ref[...] = wu_ref[...].astype(BF16)

        def gate_up(r0, rows):
            xs = xb_ref[pl.ds(r0, rows), :]
            g = _dot(xs, w1_ref[...]) + bg_ref[...]
            u = _dot(xs, w2_ref[...]) + bu_ref[...]
            g = jnp.minimum(g, SWIGLU_LIMIT)
            u = jnp.clip(u, -SWIGLU_LIMIT, SWIGLU_LIMIT)
            a_ref[s, pl.ds(r0, rows), :] = ((u + 1.0) * (g * jax.nn.sigmoid(SWIGLU_ALPHA * g))).astype(BF16)

        for_row_blocks(gate_up)

    @pl.when((s == nf) & (i > 0))
    def _():
        out_copy(i - 1).wait()

    @pl.when((s >= nf) & (valid > 0))
    def _():
        n = s - nf
        w1_ref[...] = wd_ref[...].astype(BF16)

        def down(r0, rows):
            acc = bd_ref[...] + _dot(a_ref[0, pl.ds(r0, rows), :], w1_ref[0:tf, :])
            for f in range(1, nf):
                acc = acc + _dot(a_ref[f, pl.ds(r0, rows), :], w1_ref[f * tf:(f + 1) * tf, :])
            packed = _pack_halves(acc.astype(BF16).astype(F32))
            for j in range(tf // 2 // LANES):
                ys_ref[pl.ds(r0 * rpt + n * (tf // 2 // LANES) + j, rows, stride=rpt), :] = (
                    packed[:, j * LANES:(j + 1) * LANES])

        for_row_blocks(down)

    @pl.when(s == 2 * nf - 1)
    def _():
        out_copy(i).start()

        @pl.when(i == n_tiles - 1)
        def _():
            out_copy(i).wait()


def _moe_ffn(tile_e, tile_valid, tile_c0, toks, h_tiles, w_gate, b_gate, w_up, b_up, w_down, b_down):
    e, d, dff = w_gate.shape
    tm, tf = MOE_TM, MOE_TF
    rpt = SUBLANES
    assert d == dff, "the weight staging buffers are shared between the up and down projections"
    assert d == 2 * rpt * LANES, "a packed token row must be exactly one (8, 128) tile"
    assert tm % MOE_TSB == 0 and dff % tf == 0 and (tf // 2) % LANES == 0
    nf = dff // tf
    assert nf >= 2
    n_tiles = tile_e.shape[0]
    up = lambda i, s, te, tv, tc, tk: (te[i], 0, jnp.minimum(s, nf - 1))
    dn = lambda i, s, te, tv, tc, tk: (te[i], 0, jnp.maximum(s - nf, 0))
    grid_spec = pltpu.PrefetchScalarGridSpec(
        num_scalar_prefetch=4,
        grid=(n_tiles, 2 * nf),
        in_specs=[
            pl.BlockSpec(memory_space=pl.ANY),
            pl.BlockSpec((None, d, tf), up),
            pl.BlockSpec((None, d, tf), up),
            pl.BlockSpec((None, dff, tf), dn),
            pl.BlockSpec((None, 1, tf), up),
            pl.BlockSpec((None, 1, tf), up),
            pl.BlockSpec((None, 1, tf), dn),
        ],
        out_specs=pl.BlockSpec(memory_space=pl.ANY),
        scratch_shapes=[
            pltpu.VMEM((tm * rpt, LANES), U32),
            pltpu.VMEM((tm, d), BF16),
            pltpu.VMEM((nf, tm, tf), BF16),
            pltpu.VMEM((d, tf), BF16),
            pltpu.VMEM((d, tf), BF16),
            pltpu.VMEM((tm * rpt, LANES), U32),
            pltpu.SemaphoreType.DMA(()),
            pltpu.SemaphoreType.DMA(()),
        ],
    )
    return pl.pallas_call(
        functools.partial(_moe_kernel, nf=nf, n_tiles=n_tiles),
        grid_spec=grid_spec,
        out_shape=jax.ShapeDtypeStruct((n_tiles * tm * rpt, LANES), U32),
        compiler_params=_cparams(("arbitrary", "arbitrary")),
        name="moe_ffn",
    )(tile_e, tile_valid, tile_c0, toks, h_tiles, w_gate, w_up, w_down,
      b_gate.reshape(e, 1, dff), b_up.reshape(e, 1, dff), b_down.reshape(e, 1, d))


def _combine_kernel(pos_ref, x_ref, y_hbm, w_ref, g_ref, o_ref, buf_ref, sem, *, final, n_steps, tf):
    i = pl.program_id(0)
    tm, d = x_ref.shape
    rpt = SUBLANES
    slot = i % 2

    def issue(step, sl_):
        def body(tt, c):
            for k in range(TOP_K):
                p = pos_ref[(step * tm + tt) * TOP_K + k]
                src = y_hbm.at[pl.ds(pl.multiple_of(p * rpt, rpt), rpt)]
                dst = buf_ref.at[sl_, pl.ds(pl.multiple_of((k * tm + tt) * rpt, rpt), rpt)]
                pltpu.make_async_copy(src, dst, sem.at[sl_]).start()
            return c

        lax.fori_loop(0, tm, body, 0)

    @pl.when(i == 0)
    def _():
        issue(0, 0)

    @pl.when(i + 1 < n_steps)
    def _():
        issue(i + 1, 1 - slot)

    pltpu.make_async_copy(y_hbm.at[pl.ds(0, TOP_K * tm * rpt)], buf_ref.at[slot], sem.at[slot]).wait()

    w = w_ref[...]
    wk = [jnp.broadcast_to(w[:, k:k + 1], (tm, LANES)) for k in range(TOP_K)]
    per_chunk = tf // 2 // LANES
    slabs = [None] * (d // LANES)
    for sl in range(rpt):
        n, j = sl // per_chunk, sl % per_chunk
        c_lo = (n * tf) // LANES + j
        c_hi = c_lo + per_chunk
        lo = x_ref[:, c_lo * LANES:(c_lo + 1) * LANES]
        hi = x_ref[:, c_hi * LANES:(c_hi + 1) * LANES]
        for k in range(TOP_K):
            p = buf_ref[slot, pl.ds(k * tm * rpt + sl, tm, stride=rpt), :]
            lo = lo + wk[k] * _unpack_lo(p)
            hi = hi + wk[k] * _unpack_hi(p)
        slabs[c_lo], slabs[c_hi] = lo, hi
    acc = jnp.concatenate(slabs, axis=1)
    o_ref[...] = _rms(acc, g_ref[...]) if final else acc


def _combine(pos, x2, y_tiles, top_w, g, final):
    t, d = x2.shape
    tm = COMBINE_TM
    rpt = SUBLANES
    n_steps = t // tm
    grid_spec = pltpu.PrefetchScalarGridSpec(
        num_scalar_prefetch=1,
        grid=(n_steps,),
        in_specs=[
            pl.BlockSpec((tm, d), lambda i, pos: (i, 0)),
            pl.BlockSpec(memory_space=pl.ANY),
            pl.BlockSpec((tm, TOP_K), lambda i, pos: (i, 0)),
            pl.BlockSpec((1, d), lambda i, pos: (0, 0)),
        ],
        out_specs=pl.BlockSpec((tm, d), lambda i, pos: (i, 0)),
        scratch_shapes=[
            pltpu.VMEM((2, TOP_K * tm * rpt, LANES), U32),
            pltpu.SemaphoreType.DMA((2,)),
        ],
    )
    return pl.pallas_call(
        functools.partial(_combine_kernel, final=final, n_steps=n_steps, tf=MOE_TF),
        grid_spec=grid_spec,
        out_shape=jax.ShapeDtypeStruct((t, d), F32),
        compiler_params=_cparams(("arbitrary",)),
        name="combine",
    )(pos, x2, y_tiles, top_w, g)


def _rope_tables(seq):
    half = ROT_DIM // 2
    inv = ROPE_THETA ** (-(jnp.arange(half, dtype=F32) * 2.0 / ROT_DIM))
    ang = jnp.arange(seq, dtype=F32)[:, None] * inv[None, :]
    cos, sin = jnp.cos(ang), jnp.sin(ang)
    ones = jnp.ones((seq, HEAD_DIM - ROT_DIM), F32)
    zeros = jnp.zeros((seq, HEAD_DIM - ROT_DIM), F32)
    zh = jnp.zeros((seq, half), F32)
    c = jnp.concatenate([cos, cos, ones], axis=1)
    sa = jnp.concatenate([-sin, zh, zeros], axis=1)
    sb = jnp.concatenate([zh, sin, zeros], axis=1)
    rep = LANES // HEAD_DIM
    return jnp.tile(c, (1, rep)), jnp.tile(sa, (1, rep)), jnp.tile(sb, (1, rep))


def _route(logits, tm):
    t = logits.shape[0]
    i32 = jnp.int32
    top_val, top_idx = lax.top_k(logits, TOP_K)
    top_w = jax.nn.softmax(top_val, axis=-1)
    n_assign = t * TOP_K
    flat_e = top_idx.reshape(n_assign).astype(i32)
    order = jnp.argsort(flat_e).astype(i32)
    rank = jnp.argsort(order).astype(i32)
    onehot = flat_e[:, None] == jnp.arange(N_EXPERTS, dtype=i32)[None, :]
    counts = jnp.sum(onehot, axis=0, dtype=i32)
    start = jnp.cumsum(counts) - counts
    tiles_e = (counts + tm - 1) // tm
    tile_end = jnp.cumsum(tiles_e)
    tile_first = tile_end - tiles_e
    n_tiles = n_assign // tm + N_EXPERTS
    j = jnp.arange(n_tiles, dtype=i32)
    tile_e = jnp.minimum(jnp.searchsorted(tile_end, j, side="right"), N_EXPERTS - 1).astype(i32)
    local = j - tile_first[tile_e]
    tile_valid = jnp.clip(counts[tile_e] - local * tm, 0, tm).astype(i32)
    tile_c0 = jnp.where(tile_valid > 0, start[tile_e] + local * tm, 0).astype(i32)
    last_e = jnp.max(jnp.where(tile_valid > 0, tile_e, 0))
    tile_e = jnp.where(tile_valid > 0, tile_e, last_e).astype(i32)
    row_base = jnp.sum(jnp.where(onehot, (tile_first * tm - start)[None, :], 0), axis=1, dtype=i32)
    pos = rank + row_base
    return top_w, order // TOP_K, pos, tile_e, tile_valid, tile_c0


def kernel(x, mem, g_mix, w_in, rpb_na, sinks, w_na_o, w_win_o, w_out, g_cross, g_mem, w_cq, w_ckv, w_co,
           g_moe, w_router, b_router, w_gate, b_gate, w_up, b_up, w_down, b_down, g_final):
    b, s, d = x.shape
    mem_len = mem.shape[1]
    t = b * s
    depth = w_in.shape[0]
    xc = x.reshape(t, d)
    mem2d = mem.reshape(b * mem_len, d)
    rope_c, rope_sa, rope_sb = _rope_tables(s)

    cols = _cols(d)
    yb_rows = jnp.concatenate([h * HEAD_DIM + jnp.arange(HEAD_DIM) for h in _swa_head_perm()])
    n_attn = cols.end - cols.qa
    src_cols = jnp.concatenate([n_attn + jnp.arange(2 * d), jnp.arange(3 * NA_WIDTH),
                                3 * NA_WIDTH + yb_rows, 3 * NA_WIDTH + WIN_Q_WIDTH + jnp.arange(2 * WIN_KV_WIDTH)])

    for l in range(depth):
        w_in_l = w_in[l][:, src_cols].astype(BF16)
        proj = _in_proj(xc, g_mix[l].reshape(1, d), w_in_l, rope_c, rope_sa, rope_sb, s)
        ya = _natten(proj, _natten_bias_table(rpb_na[l]), cols, b, s)
        yb = _swa(proj, sinks[l].astype(F32), cols, b, s)
        x1 = _merge(ya, yb, proj, xc, w_na_o[l].astype(BF16), w_win_o[l][yb_rows].astype(BF16),
                    w_out[l].astype(BF16))
        memkv = _memkv(mem2d, g_mem[l].reshape(1, d), w_ckv[l].astype(BF16), mem_len)
        wr_hi = w_router[l].astype(BF16)
        wr_lo = (w_router[l] - wr_hi.astype(F32)).astype(BF16)
        x2, h3, logits = _cross(x1, g_cross[l].reshape(1, d), w_cq[l].astype(BF16), memkv, w_co[l].astype(BF16),
                                g_moe[l].reshape(1, d), wr_hi, wr_lo, b_router[l].reshape(1, -1), s, mem_len)
        top_w, toks, pos, tile_e, tile_valid, tile_c0 = _route(logits, MOE_TM)
        ys = _moe_ffn(tile_e, tile_valid, tile_c0, toks, h3, w_gate[l], b_gate[l], w_up[l], b_up[l],
                      w_down[l], b_down[l])
        xc = _combine(pos, x2, ys, top_w, g_final.reshape(1, d), final=(l + 1 == depth))
    return xc.reshape(b, s, d)
```

```python
import functools
from typing import NamedTuple

import jax
import jax.numpy as jnp
from jax import lax
from jax.experimental import pallas as pl
from jax.experimental.pallas import tpu as pltpu

F32 = jnp.float32
BF16 = jnp.bfloat16
U32 = jnp.uint32

GRID_W = 64
HEAD_DIM = 64
NA_HEADS = 16
NA_ROWS = 8
NA_COLS = 16
WIN_Q_HEADS = 16
WIN_KV_HEADS = 4
WINDOW = 128
ROT_DIM = HEAD_DIM // 4
ROPE_THETA = 500000.0
CROSS_HEADS = 4
CROSS_HEAD_DIM = 128
N_EXPERTS = 32
TOP_K = 4
SWIGLU_LIMIT = 7.0
SWIGLU_ALPHA = 1.702
NORM_EPS = 1e-5
NEG_INF = -1e30

LANES = 128
SUBLANES = 8
VMEM_LIMIT = 56 * 1024 * 1024

NA_WIDTH = NA_HEADS * HEAD_DIM
WIN_Q_WIDTH = WIN_Q_HEADS * HEAD_DIM
WIN_KV_WIDTH = WIN_KV_HEADS * HEAD_DIM
CROSS_WIDTH = CROSS_HEADS * CROSS_HEAD_DIM


class _Cols(NamedTuple):
    ga: int
    gb: int
    qa: int
    ka: int
    va: int
    qb: int
    kb: int
    vb: int
    end: int


def _cols(d):
    widths = [d, d, NA_WIDTH, NA_WIDTH, NA_WIDTH, WIN_Q_WIDTH, WIN_KV_WIDTH, WIN_KV_WIDTH]
    offs = [0]
    for w in widths:
        offs.append(offs[-1] + w)
    return _Cols(*offs)


PROJ_TM = 1024
PROJ_TN = 512
NA_UNROLL = 2
MERGE_TM = 256
CROSS_TM = 256
MOE_TM = 2304
MOE_TSB = 256
MOE_TF = 256
GATHER_UNROLL = 8
COMBINE_TM = 256


def _cparams(sem):
    return pltpu.CompilerParams(dimension_semantics=sem, vmem_limit_bytes=VMEM_LIMIT)


def _rms(x, g):
    ms = jnp.mean(x * x, axis=-1, keepdims=True)
    return x * lax.rsqrt(ms + NORM_EPS) * g


def _dot(a, b):
    return jnp.dot(a, b, preferred_element_type=F32)


def _dot_nt(a, b):
    return lax.dot_general(a, b, (((1,), (1,)), ((), ())), preferred_element_type=F32)


def _pack_halves(v):
    half = v.shape[1] // 2
    lo = lax.shift_right_logical(pltpu.bitcast(v[:, :half], U32), jnp.uint32(16))
    hi = pltpu.bitcast(v[:, half:], U32) & jnp.uint32(0xFFFF0000)
    return hi | lo


def _unpack_lo(p):
    return pltpu.bitcast(lax.shift_left(p, jnp.uint32(16)), F32)


def _unpack_hi(p):
    return pltpu.bitcast(p & jnp.uint32(0xFFFF0000), F32)


def _store_token_tiles(ref, packed):
    m, w = packed.shape
    n_sl = w // LANES
    for s in range(n_sl):
        ref[pl.ds(s, m, stride=n_sl), :] = packed[:, s * LANES:(s + 1) * LANES]


def _rope_chunk(x, c, sa, sb):
    return x * c + pltpu.roll(x, LANES - ROT_DIM // 2, 1) * sa + pltpu.roll(x, ROT_DIM // 2, 1) * sb


def _inproj_kernel(x_ref, g_ref, w_ref, c_ref, sa_ref, sb_ref, o_ref, h_ref, *, cols):
    j = pl.program_id(1)
    tn = o_ref.shape[1]
    nchunk = tn // LANES

    @pl.when(j == 0)
    def _():
        h_ref[...] = _rms(x_ref[...], g_ref[...]).astype(BF16)

    acc = _dot(h_ref[...], w_ref[...])
    o_ref[...] = acc.astype(BF16)

    j_qb, j_kb = cols.qb // tn, cols.kb // tn
    kb_chunks = WIN_KV_WIDTH // LANES

    @pl.when(j >= j_qb)
    def _():
        c, sa, sb = c_ref[...], sa_ref[...], sb_ref[...]

        def rotate(k):
            sl = slice(k * LANES, (k + 1) * LANES)
            o_ref[:, sl] = _rope_chunk(acc[:, sl], c, sa, sb).astype(BF16)

        for k in range(nchunk):
            if k < kb_chunks:
                rotate(k)
            else:
                pl.when(j < j_kb)(functools.partial(rotate, k))


def _in_proj(x2d, g, w_bf, rope_c, rope_sa, rope_sb, seq):
    t, d = x2d.shape
    n = w_bf.shape[1]
    cols = _cols(d)
    tm, tn = PROJ_TM, PROJ_TN
    assert t % tm == 0 and n % tn == 0 and seq % tm == 0 and n == cols.end
    assert all(c % tn == 0 for c in (cols.qa, cols.ka, cols.qb, cols.kb)) and cols.end == cols.kb + tn
    sblocks = seq // tm
    rope_spec = pl.BlockSpec((tm, LANES), lambda i, j: (i % sblocks, 0))
    return pl.pallas_call(
        functools.partial(_inproj_kernel, cols=cols),
        grid=(t // tm, n // tn),
        in_specs=[
            pl.BlockSpec((tm, d), lambda i, j: (i, 0)),
            pl.BlockSpec((1, d), lambda i, j: (0, 0)),
            pl.BlockSpec((d, tn), lambda i, j: (0, j)),
            rope_spec, rope_spec, rope_spec,
        ],
        out_specs=pl.BlockSpec((tm, tn), lambda i, j: (i, j)),
        out_shape=jax.ShapeDtypeStruct((t, n), BF16),
        scratch_shapes=[pltpu.VMEM((tm, d), BF16)],
        compiler_params=_cparams(("parallel", "arbitrary")),
        name="in_proj",
    )(x2d, g, w_bf, rope_c, rope_sa, rope_sb)


def _natten_kernel(q_ref, k_ref, v_ref, b_ref, o_ref, s_ref, p_ref, l_ref, *, rows):
    kr = NA_ROWS
    nu = NA_UNROLL
    lane = lax.broadcasted_iota(jnp.int32, (GRID_W, LANES), 1)
    first = lane < HEAD_DIM

    def window(r):
        rs = jnp.clip(r - kr // 2, 0, rows - kr)
        return pl.multiple_of(rs * GRID_W, GRID_W), rs - r + (NA_ROWS - 1)

    def stage_a(r, slot):
        k0, dr0 = window(r)
        q2 = q_ref[pl.ds(pl.multiple_of(r * GRID_W, GRID_W), GRID_W), :]
        k2 = k_ref[pl.ds(k0, kr * GRID_W), :]
        zero = jnp.zeros_like(q2)
        qs = jnp.concatenate([jnp.where(first, q2, zero), jnp.where(first, zero, q2)], axis=0)
        s_ref[slot] = _dot_nt(qs, k2) + b_ref[dr0]

    def stage_b(slot):
        s = s_ref[slot]
        m = jnp.max(s, axis=-1, keepdims=True)
        p = jnp.exp(s - m)
        l_ref[slot] = jnp.broadcast_to(1.0 / jnp.sum(p, axis=-1, keepdims=True), l_ref.shape[1:])
        p_ref[slot] = p.astype(BF16)

    def stage_c(r, slot):
        k0, _ = window(r)
        v2 = v_ref[pl.ds(k0, kr * GRID_W), :]
        o = _dot(p_ref[slot], v2) * l_ref[slot]
        o_ref[pl.ds(pl.multiple_of(r * GRID_W, GRID_W), GRID_W), :] = (
            jnp.where(first, o[:GRID_W], o[GRID_W:]).astype(BF16))

    def iteration(t, do_a, do_b, do_c):
        for j in range(nu):
            if do_c:
                stage_c((t - 2) * nu + j, j)
        for j in range(nu):
            if do_b:
                stage_b(j)
        for j in range(nu):
            if do_a:
                stage_a(t * nu + j, j)

    nt = rows // nu
    iteration(0, True, False, False)
    iteration(1, True, True, False)

    def body(t, carry):
        iteration(t, True, True, True)
        return carry

    lax.fori_loop(2, nt, body, 0)
    iteration(nt, False, True, True)
    iteration(nt + 1, False, False, True)


def _natten(proj, bias_tab, cols, batch, seq):
    npairs = NA_HEADS // 2
    rows = seq // GRID_W
    nu = NA_UNROLL
    assert rows >= NA_ROWS and rows % nu == 0 and rows // nu >= 2
    kq, kk, kv = cols.qa // LANES, cols.ka // LANES, cols.va // LANES
    blk = (seq, LANES)
    return pl.pallas_call(
        functools.partial(_natten_kernel, rows=rows),
        grid=(batch, npairs),
        in_specs=[
            pl.BlockSpec(blk, lambda b, p: (b, kq + p)),
            pl.BlockSpec(blk, lambda b, p: (b, kk + p)),
            pl.BlockSpec(blk, lambda b, p: (b, kv + p)),
            pl.BlockSpec((None, NA_ROWS, 2 * GRID_W, NA_ROWS * GRID_W), lambda b, p: (p, 0, 0, 0)),
        ],
        out_specs=pl.BlockSpec(blk, lambda b, p: (b, p)),
        out_shape=jax.ShapeDtypeStruct((batch * seq, NA_WIDTH), BF16),
        scratch_shapes=[
            pltpu.VMEM((nu, 2 * GRID_W, NA_ROWS * GRID_W), F32),
            pltpu.VMEM((nu, 2 * GRID_W, NA_ROWS * GRID_W), BF16),
            pltpu.VMEM((nu, 2 * GRID_W, LANES), F32),
        ],
        compiler_params=_cparams(("parallel", "parallel")),
        name="natten",
    )(proj, proj, proj, bias_tab)


def _natten_bias_table(rpb):
    cols = jnp.arange(GRID_W)
    col_start = jnp.clip(cols - NA_COLS // 2, 0, GRID_W - NA_COLS)
    kc = jnp.arange(GRID_W)
    inwin = (kc[None, :] >= col_start[:, None]) & (kc[None, :] < col_start[:, None] + NA_COLS)
    dc = kc[None, :] - cols[:, None] + (NA_COLS - 1)
    onehot = (dc[:, :, None] == jnp.arange(2 * NA_COLS - 1)[None, None, :]).astype(F32)
    exp_c = jnp.einsum("ckd,hrd->hrck", onehot, rpb.astype(F32), precision=lax.Precision.HIGHEST)
    tab = jnp.stack([exp_c[:, v:v + NA_ROWS] for v in range(NA_ROWS)], axis=1)
    tab = jnp.where(inwin[None, None, None], tab, NEG_INF)
    tab = tab.transpose(0, 1, 3, 2, 4).reshape(NA_HEADS, NA_ROWS, GRID_W, NA_ROWS * GRID_W)
    tab = tab.reshape(NA_HEADS // 2, 2, NA_ROWS, GRID_W, NA_ROWS * GRID_W).transpose(0, 2, 1, 3, 4)
    return tab.reshape(NA_HEADS // 2, NA_ROWS, 2 * GRID_W, NA_ROWS * GRID_W)


def _swa_kernel(sink_ref, q_ref, kp_ref, kc_ref, kn_ref, vp_ref, vc_ref, vn_ref, o_ref, *, seq):
    kp = pl.program_id(1)
    n = pl.program_id(2)
    w = WINDOW
    grp = WIN_Q_HEADS // WIN_KV_HEADS
    k3 = jnp.concatenate([kp_ref[...], kc_ref[...], kn_ref[...]], axis=0)
    v3 = jnp.concatenate([vp_ref[...], vc_ref[...], vn_ref[...]], axis=0)
    qi = lax.broadcasted_iota(jnp.int32, (2 * w, 3 * w), 0)
    qi = jnp.where(qi >= w, qi - w, qi)
    kj = lax.broadcasted_iota(jnp.int32, (2 * w, 3 * w), 1)
    kpos = n * w - w + kj
    valid = (jnp.abs(kj - w - qi) <= w) & (kpos >= 0) & (kpos < seq)
    lane = lax.broadcasted_iota(jnp.int32, (w, LANES), 1)
    first = lane < HEAD_DIM
    rowhalf = lax.broadcasted_iota(jnp.int32, (2 * w, 1), 0) < w
    for c in range(grp):
        sl = slice(c * LANES, (c + 1) * LANES)
        q2 = q_ref[:, sl]
        zero = jnp.zeros_like(q2)
        qs = jnp.concatenate([jnp.where(first, q2, zero), jnp.where(first, zero, q2)], axis=0)
        s = jnp.where(valid, _dot_nt(qs, k3), NEG_INF)
        sink0 = sink_ref[(2 * kp) * grp + c]
        sink1 = sink_ref[(2 * kp + 1) * grp + c]
        sink = jnp.where(rowhalf, sink0, sink1)
        m = jnp.maximum(jnp.max(s, axis=-1, keepdims=True), sink)
        p = jnp.exp(s - m)
        denom = jnp.sum(p, axis=-1, keepdims=True) + jnp.exp(sink - m)
        o = _dot(p.astype(BF16), v3) * (1.0 / denom)
        o_ref[:, sl] = jnp.where(first, o[:w], o[w:]).astype(BF16)


def _swa(proj, sinks, cols, batch, seq):
    w = WINDOW
    nb = seq // w
    grp = WIN_Q_HEADS // WIN_KV_HEADS
    qw = grp * LANES
    assert cols.qb % qw == 0
    cq, ck, cv = cols.qb // qw, cols.kb // LANES, cols.vb // LANES

    def kv_spec(col0, shift):
        def imap(b, kp, n):
            return (b * nb + jnp.clip(n + shift, 0, nb - 1), col0 + kp)
        return pl.BlockSpec((w, LANES), imap)

    return pl.pallas_call(
        functools.partial(_swa_kernel, seq=seq),
        grid=(batch, WIN_KV_HEADS // 2, nb),
        in_specs=[
            pl.BlockSpec(memory_space=pltpu.SMEM),
            pl.BlockSpec((w, qw), lambda b, kp, n: (b * nb + n, cq + kp)),
            kv_spec(ck, -1), kv_spec(ck, 0), kv_spec(ck, 1),
            kv_spec(cv, -1), kv_spec(cv, 0), kv_spec(cv, 1),
        ],
        out_specs=pl.BlockSpec((w, qw), lambda b, kp, n: (b * nb + n, kp)),
        out_shape=jax.ShapeDtypeStruct((batch * seq, WIN_Q_WIDTH), BF16),
        compiler_params=_cparams(("parallel", "parallel", "parallel")),
        name="swa",
    )(sinks, proj, proj, proj, proj, proj, proj, proj)


def _swa_head_order(w, axis):
    grp = WIN_Q_HEADS // WIN_KV_HEADS
    shape = w.shape
    split = shape[:axis] + (WIN_KV_HEADS // 2, 2, grp, HEAD_DIM) + shape[axis + 1:]
    return jnp.swapaxes(w.reshape(split), axis + 1, axis + 2).reshape(shape)


def _merge_kernel(ya_ref, yb_ref, ga_ref, gb_ref, x_ref, wna_ref, wwin_ref, wout_ref, o_ref):
    a = _dot(ya_ref[...], wna_ref[...])
    b = _dot(yb_ref[...], wwin_ref[...])
    merged = jax.nn.sigmoid(ga_ref[...].astype(F32)) * a + jax.nn.sigmoid(gb_ref[...].astype(F32)) * b
    o_ref[...] = x_ref[...] + _dot(merged.astype(BF16), wout_ref[...])


def _resident(shape):
    return pl.BlockSpec(shape, lambda i: (0,) * len(shape), pipeline_mode=pl.Buffered(1))


def _merge(ya, yb, proj, x2d, wna, wwin, wout):
    t, d = x2d.shape
    tm = MERGE_TM
    return pl.pallas_call(
        _merge_kernel,
        grid=(t // tm,),
        in_specs=[
            pl.BlockSpec((tm, NA_WIDTH), lambda i: (i, 0)),
            pl.BlockSpec((tm, WIN_Q_WIDTH), lambda i: (i, 0)),
            pl.BlockSpec((tm, d), lambda i: (i, 0)),
            pl.BlockSpec((tm, d), lambda i: (i, 1)),
            pl.BlockSpec((tm, d), lambda i: (i, 0)),
            _resident(wna.shape), _resident(wwin.shape), _resident(wout.shape),
        ],
        out_specs=pl.BlockSpec((tm, d), lambda i: (i, 0)),
        out_shape=jax.ShapeDtypeStruct((t, d), F32),
        compiler_params=_cparams(("parallel",)),
        name="merge",
    )(ya, yb, proj, proj, x2d, wna, wwin, wout)


def _memkv_kernel(m_ref, g_ref, w_ref, o_ref):
    o_ref[...] = _dot(_rms(m_ref[...], g_ref[...]).astype(BF16), w_ref[...]).astype(BF16)


def _memkv(mem2d, g, w_bf, mem_len):
    t, d = mem2d.shape
    n = w_bf.shape[1]
    return pl.pallas_call(
        _memkv_kernel,
        grid=(t // mem_len,),
        in_specs=[
            pl.BlockSpec((mem_len, d), lambda i: (i, 0)),
            pl.BlockSpec((1, d), lambda i: (0, 0)),
            pl.BlockSpec((d, n), lambda i: (0, 0)),
        ],
        out_specs=pl.BlockSpec((mem_len, n), lambda i: (i, 0)),
        out_shape=jax.ShapeDtypeStruct((t, n), BF16),
        compiler_params=_cparams(("parallel",)),
        name="memkv",
    )(mem2d, g, w_bf)


def _cross_kernel(x_ref, gc_ref, wcq_ref, k_ref, v_ref, wco_ref, gm_ref, wrh_ref, wrl_ref, br_ref,
                  x2_ref, h_ref, lg_ref):
    x = x_ref[...]
    q = _dot(_rms(x, gc_ref[...]).astype(BF16), wcq_ref[...]).astype(BF16)
    scale = CROSS_HEAD_DIM ** -0.5
    outs = []
    for h in range(CROSS_HEADS):
        sl = slice(h * CROSS_HEAD_DIM, (h + 1) * CROSS_HEAD_DIM)
        s = _dot_nt(q[:, sl], k_ref[:, sl]) * scale
        m = jnp.max(s, axis=-1, keepdims=True)
        p = jnp.exp(s - m)
        l = jnp.sum(p, axis=-1, keepdims=True)
        outs.append((_dot(p.astype(BF16), v_ref[:, sl]) * (1.0 / l)).astype(BF16))
    o = jnp.concatenate(outs, axis=-1)
    x2 = x + _dot(o, wco_ref[...])
    x2_ref[...] = x2
    h = _rms(x2, gm_ref[...])
    h_hi = h.astype(BF16)
    h_lo = (h - h_hi.astype(F32)).astype(BF16)
    _store_token_tiles(h_ref, _pack_halves(h_hi.astype(F32)))
    lg_ref[...] = (_dot(h_hi, wrh_ref[...]) + _dot(h_hi, wrl_ref[...]) + _dot(h_lo, wrh_ref[...])
                   + br_ref[...])


def _cross(x1, gc, wcq, memkv, wco, gm, wr_hi, wr_lo, br, seq, mem_len):
    t, d = x1.shape
    tm = CROSS_TM
    ne = wr_hi.shape[1]
    per_b = seq // tm
    row = lambda i: (i, 0)
    return pl.pallas_call(
        _cross_kernel,
        grid=(t // tm,),
        in_specs=[
            pl.BlockSpec((tm, d), row),
            _resident((1, d)),
            _resident(wcq.shape),
            pl.BlockSpec((mem_len, CROSS_WIDTH), lambda i: (i // per_b, 0)),
            pl.BlockSpec((mem_len, CROSS_WIDTH), lambda i: (i // per_b, 1)),
            _resident(wco.shape),
            _resident((1, d)),
            _resident(wr_hi.shape), _resident(wr_lo.shape),
            _resident((1, ne)),
        ],
        out_specs=[
            pl.BlockSpec((tm, d), row),
            pl.BlockSpec((tm * SUBLANES, LANES), row),
            pl.BlockSpec((tm, ne), row),
        ],
        out_shape=[
            jax.ShapeDtypeStruct((t, d), F32),
            jax.ShapeDtypeStruct((t * SUBLANES, LANES), U32),
            jax.ShapeDtypeStruct((t, ne), F32),
        ],
        compiler_params=_cparams(("parallel",)),
        name="cross",
    )(x1, gc, wcq, memkv, memkv, wco, gm, wr_hi, wr_lo, br)


def _moe_kernel(te_ref, tv_ref, tc_ref, tok_ref, h_hbm, wg_ref, wu_ref, wd_ref, bg_ref, bu_ref, bd_ref, y_hbm,
                raw_ref, xb_ref, a_ref, w1_ref, w2_ref, ys_ref, gsem, osem, *, nf, n_tiles):
    i = pl.program_id(0)
    s = pl.program_id(1)
    tsb = MOE_TSB
    rpt = SUBLANES
    tm = xb_ref.shape[0]
    tf = wg_ref.shape[1]
    half = xb_ref.shape[1] // 2
    valid = tv_ref[i]
    n_sb = (valid + tsb - 1) // tsb

    def row_copy(tok, k):
        src = h_hbm.at[pl.ds(pl.multiple_of(tok * rpt, rpt), rpt)]
        dst = raw_ref.at[pl.ds(pl.multiple_of(k * rpt, rpt), rpt)]
        return pltpu.make_async_copy(src, dst, gsem)

    def issue_gather(t):
        v = tv_ref[t]
        c0 = tc_ref[t]

        def group(gi, c):
            for u in range(GATHER_UNROLL):
                k = gi * GATHER_UNROLL + u
                row_copy(tok_ref[c0 + k], k).start(priority=u % 2)
            return c

        def single(k, c):
            row_copy(tok_ref[c0 + k], k).start()
            return c

        def pad(k, c):
            row_copy(0, k).start()
            return c

        n_grp = lax.shift_right_logical(v, GATHER_UNROLL.bit_length() - 1)
        lax.fori_loop(0, n_grp, group, 0)
        lax.fori_loop(n_grp * GATHER_UNROLL, v, single, 0)
        lax.fori_loop(v, (v + tsb - 1) // tsb * tsb, pad, 0)

    def block_rows(ref, sb):
        return ref.at[pl.ds(pl.multiple_of(sb * (tsb * rpt), tsb * rpt), tsb * rpt)]

    def out_copy(t):
        dst = y_hbm.at[pl.ds(pl.multiple_of(t * (tm * rpt), tm * rpt), tm * rpt)]
        return pltpu.make_async_copy(ys_ref, dst, osem)

    @pl.when((i == 0) & (s == 0))
    def _():
        ys_ref[...] = jnp.zeros(ys_ref.shape, U32)
        issue_gather(0)

    @pl.when(s == 0)
    def _():
        def wait_rows(sb, c):
            pltpu.make_async_copy(block_rows(h_hbm, 0), block_rows(raw_ref, sb), gsem).wait()
            return c

        lax.fori_loop(0, n_sb, wait_rows, 0)

        def unpack(sb, c):
            r0 = pl.multiple_of(sb * tsb, tsb)
            for sl in range(rpt):
                p = raw_ref[pl.ds(r0 * rpt + sl, tsb, stride=rpt), :]
                xb_ref[pl.ds(r0, tsb), sl * LANES:(sl + 1) * LANES] = _unpack_lo(p).astype(BF16)
                xb_ref[pl.ds(r0, tsb), half + sl * LANES:half + (sl + 1) * LANES] = _unpack_hi(p).astype(BF16)
            return c

        lax.fori_loop(0, n_sb, unpack, 0)

    @pl.when((s == 1) & (i + 1 < n_tiles))
    def _():
        issue_gather(i + 1)

    def for_row_blocks(fn):
        def pair(b, c):
            fn(pl.multiple_of(b * (2 * tsb), 2 * tsb), 2 * tsb)
            return c

        lax.fori_loop(0, n_sb // 2, pair, 0)

        @pl.when(n_sb % 2 == 1)
        def _():
            fn(pl.multiple_of((n_sb - 1) * tsb, tsb), tsb)

    @pl.when((s < nf) & (valid > 0))
    def _():
        w1_ref[...] = wg_ref[...].astype(BF16)
        w2_ref[...] = wu_ref[...].astype(BF16)

        def gate_up(r0, rows):
            xs = xb_ref[pl.ds(r0, rows), :]
            g = _dot(xs, w1_ref[...]) + bg_ref[...]
            u = _dot(xs, w2_ref[...]) + bu_ref[...]
            g = jnp.minimum(g, SWIGLU_LIMIT)
            u = jnp.clip(u, -SWIGLU_LIMIT, SWIGLU_LIMIT)
            a_ref[s, pl.ds(r0, rows), :] = ((u + 1.0) * (g * jax.nn.sigmoid(SWIGLU_ALPHA * g))).astype(BF16)

        for_row_blocks(gate_up)

    @pl.when((s == nf) & (i > 0))
    def _():
        out_copy(i - 1).wait()

    @pl.when((s >= nf) & (valid > 0))
    def _():
        n = s - nf
        w1_ref[...] = wd_ref[...].astype(BF16)

        def down(r0, rows):
            acc = bd_ref[...] + _dot(a_ref[0, pl.ds(r0, rows), :], w1_ref[0:tf, :])
            for f in range(1, nf):
                acc = acc + _dot(a_ref[f, pl.ds(r0, rows), :], w1_ref[f * tf:(f + 1) * tf, :])
            packed = _pack_halves(acc.astype(BF16).astype(F32))
            for j in range(tf // 2 // LANES):
                ys_ref[pl.ds(r0 * rpt + n * (tf // 2 // LANES) + j, rows, stride=rpt), :] = (
                    packed[:, j * LANES:(j + 1) * LANES])

        for_row_blocks(down)

    @pl.when(s == 2 * nf - 1)
    def _():
        out_copy(i).start()

        @pl.when(i == n_tiles - 1)
        def _():
            out_copy(i).wait()


def _moe_ffn(tile_e, tile_valid, tile_c0, toks, h_tiles, w_gate, b_gate, w_up, b_up, w_down, b_down):
    e, d, dff = w_gate.shape
    tm, tf = MOE_TM, MOE_TF
    rpt = SUBLANES
    assert d == dff, "the weight staging buffers are shared between the up and down projections"
    assert d == 2 * rpt * LANES, "a packed token row must be exactly one (8, 128) tile"
    assert tm % MOE_TSB == 0 and dff % tf == 0 and (tf // 2) % LANES == 0
    nf = dff // tf
    assert nf >= 2
    n_tiles = tile_e.shape[0]
    up = lambda i, s, te, tv, tc, tk: (te[i], 0, jnp.minimum(s, nf - 1))
    dn = lambda i, s, te, tv, tc, tk: (te[i], 0, jnp.maximum(s - nf, 0))
    grid_spec = pltpu.PrefetchScalarGridSpec(
        num_scalar_prefetch=4,
        grid=(n_tiles, 2 * nf),
        in_specs=[
            pl.BlockSpec(memory_space=pl.ANY),
            pl.BlockSpec((None, d, tf), up),
            pl.BlockSpec((None, d, tf), up),
            pl.BlockSpec((None, dff, tf), dn),
            pl.BlockSpec((None, 1, tf), up),
            pl.BlockSpec((None, 1, tf), up),
            pl.BlockSpec((None, 1, tf), dn),
        ],
        out_specs=pl.BlockSpec(memory_space=pl.ANY),
        scratch_shapes=[
            pltpu.VMEM((tm * rpt, LANES), U32),
            pltpu.VMEM((tm, d), BF16),
            pltpu.VMEM((nf, tm, tf), BF16),
            pltpu.VMEM((d, tf), BF16),
            pltpu.VMEM((d, tf), BF16),
            pltpu.VMEM((tm * rpt, LANES), U32),
            pltpu.SemaphoreType.DMA(()),
            pltpu.SemaphoreType.DMA(()),
        ],
    )
    return pl.pallas_call(
        functools.partial(_moe_kernel, nf=nf, n_tiles=n_tiles),
        grid_spec=grid_spec,
        out_shape=jax.ShapeDtypeStruct((n_tiles * tm * rpt, LANES), U32),
        compiler_params=_cparams(("arbitrary", "arbitrary")),
        name="moe_ffn",
    )(tile_e, tile_valid, tile_c0, toks, h_tiles, w_gate, w_up, w_down,
      b_gate.reshape(e, 1, dff), b_up.reshape(e, 1, dff), b_down.reshape(e, 1, d))


def _combine_kernel(pos_ref, x_ref, y_hbm, w_ref, g_ref, o_ref, buf_ref, sem, *, final, n_steps, tf):
    i = pl.program_id(0)
    tm, d = x_ref.shape
    rpt = SUBLANES
    slot = i % 2

    def issue(step, sl_):
        def body(pair, c):
            for u in range(2):
                tt = pair * 2 + u
                for k in range(TOP_K):
                    p = pos_ref[(step * tm + tt) * TOP_K + k]
                    src = y_hbm.at[pl.ds(pl.multiple_of(p * rpt, rpt), rpt)]
                    dst = buf_ref.at[sl_, pl.ds(pl.multiple_of((k * tm + tt) * rpt, rpt), rpt)]
                    pltpu.make_async_copy(src, dst, sem.at[sl_]).start(priority=k % 2)
            return c

        lax.fori_loop(0, tm // 2, body, 0)

    @pl.when(i == 0)
    def _():
        issue(0, 0)

    @pl.when(i + 1 < n_steps)
    def _():
        issue(i + 1, 1 - slot)

    pltpu.make_async_copy(y_hbm.at[pl.ds(0, TOP_K * tm * rpt)], buf_ref.at[slot], sem.at[slot]).wait()

    w = w_ref[...]
    wk = [jnp.broadcast_to(w[:, k:k + 1], (tm, LANES)) for k in range(TOP_K)]
    per_chunk = tf // 2 // LANES
    slabs = [None] * (d // LANES)
    for sl in range(rpt):
        n, j = sl // per_chunk, sl % per_chunk
        c_lo = (n * tf) // LANES + j
        c_hi = c_lo + per_chunk
        lo = x_ref[:, c_lo * LANES:(c_lo + 1) * LANES]
        hi = x_ref[:, c_hi * LANES:(c_hi + 1) * LANES]
        for k in range(TOP_K):
            p = buf_ref[slot, pl.ds(k * tm * rpt + sl, tm, stride=rpt), :]
            lo = lo + wk[k] * _unpack_lo(p)
            hi = hi + wk[k] * _unpack_hi(p)
        slabs[c_lo], slabs[c_hi] = lo, hi
    acc = jnp.concatenate(slabs, axis=1)
    o_ref[...] = _rms(acc, g_ref[...]) if final else acc


def _combine(pos, x2, y_tiles, top_w, g, final):
    t, d = x2.shape
    tm = COMBINE_TM
    rpt = SUBLANES
    n_steps = t // tm
    grid_spec = pltpu.PrefetchScalarGridSpec(
        num_scalar_prefetch=1,
        grid=(n_steps,),
        in_specs=[
            pl.BlockSpec((tm, d), lambda i, pos: (i, 0)),
            pl.BlockSpec(memory_space=pl.ANY),
            pl.BlockSpec((tm, TOP_K), lambda i, pos: (i, 0)),
            pl.BlockSpec((1, d), lambda i, pos: (0, 0)),
        ],
        out_specs=pl.BlockSpec((tm, d), lambda i, pos: (i, 0)),
        scratch_shapes=[
            pltpu.VMEM((2, TOP_K * tm * rpt, LANES), U32),
            pltpu.SemaphoreType.DMA((2,)),
        ],
    )
    return pl.pallas_call(
        functools.partial(_combine_kernel, final=final, n_steps=n_steps, tf=MOE_TF),
        grid_spec=grid_spec,
        out_shape=jax.ShapeDtypeStruct((t, d), F32),
        compiler_params=_cparams(("arbitrary",)),
        name="combine",
    )(pos, x2, y_tiles, top_w, g)


def _rope_tables(seq):
    half = ROT_DIM // 2
    inv = ROPE_THETA ** (-(jnp.arange(half, dtype=F32) * 2.0 / ROT_DIM))
    ang = jnp.arange(seq, dtype=F32)[:, None] * inv[None, :]
    cos, sin = jnp.cos(ang), jnp.sin(ang)
    ones = jnp.ones((seq, HEAD_DIM - ROT_DIM), F32)
    zeros = jnp.zeros((seq, HEAD_DIM - ROT_DIM), F32)
    zh = jnp.zeros((seq, half), F32)
    c = jnp.concatenate([cos, cos, ones], axis=1)
    sa = jnp.concatenate([-sin, zh, zeros], axis=1)
    sb = jnp.concatenate([zh, sin, zeros], axis=1)
    rep = LANES // HEAD_DIM
    return jnp.tile(c, (1, rep)), jnp.tile(sa, (1, rep)), jnp.tile(sb, (1, rep))


def _route(logits, tm):
    t = logits.shape[0]
    i32 = jnp.int32
    top_val, top_idx = lax.top_k(logits, TOP_K)
    top_w = jax.nn.softmax(top_val, axis=-1)
    n_assign = t * TOP_K
    flat_e = top_idx.reshape(n_assign).astype(i32)
    order = jnp.argsort(flat_e).astype(i32)
    rank = jnp.argsort(order).astype(i32)
    onehot = flat_e[:, None] == jnp.arange(N_EXPERTS, dtype=i32)[None, :]
    counts = jnp.sum(onehot, axis=0, dtype=i32)
    start = jnp.cumsum(counts) - counts
    tiles_e = (counts + tm - 1) // tm
    tile_end = jnp.cumsum(tiles_e)
    tile_first = tile_end - tiles_e
    n_tiles = n_assign // tm + N_EXPERTS
    j = jnp.arange(n_tiles, dtype=i32)
    tile_e = jnp.minimum(jnp.searchsorted(tile_end, j, side="right"), N_EXPERTS - 1).astype(i32)
    local = j - tile_first[tile_e]
    tile_valid = jnp.clip(counts[tile_e] - local * tm, 0, tm).astype(i32)
    tile_c0 = jnp.where(tile_valid > 0, start[tile_e] + local * tm, 0).astype(i32)
    last_e = jnp.max(jnp.where(tile_valid > 0, tile_e, 0))
    tile_e = jnp.where(tile_valid > 0, tile_e, last_e).astype(i32)
    row_base = jnp.sum(jnp.where(onehot, (tile_first * tm - start)[None, :], 0), axis=1, dtype=i32)
    pos = rank + row_base
    return top_w, order // TOP_K, pos, tile_e, tile_valid, tile_c0


def kernel(x, mem, g_mix, w_in, rpb_na, sinks, w_na_o, w_win_o, w_out, g_cross, g_mem, w_cq, w_ckv, w_co,
           g_moe, w_router, b_router, w_gate, b_gate, w_up, b_up, w_down, b_down, g_final):
    b, s, d = x.shape
    mem_len = mem.shape[1]
    t = b * s
    depth = w_in.shape[0]
    xc = x.reshape(t, d)
    mem2d = mem.reshape(b * mem_len, d)
    rope_c, rope_sa, rope_sb = _rope_tables(s)

    cols = _cols(d)
    n_attn = cols.end - cols.qa
    scale = HEAD_DIM ** -0.5
    o_qb, o_kb = 3 * NA_WIDTH, 3 * NA_WIDTH + WIN_Q_WIDTH

    for l in range(depth):
        w = w_in[l]
        w_in_l = jnp.concatenate([w[:, n_attn:], w[:, :NA_WIDTH] * scale, w[:, NA_WIDTH:o_qb],
                                  _swa_head_order(w[:, o_qb:o_kb] * scale, axis=1), w[:, o_kb:n_attn]],
                                 axis=1).astype(BF16)
        proj = _in_proj(xc, g_mix[l].reshape(1, d), w_in_l, rope_c, rope_sa, rope_sb, s)
        ya = _natten(proj, _natten_bias_table(rpb_na[l]), cols, b, s)
        yb = _swa(proj, sinks[l].astype(F32), cols, b, s)
        x1 = _merge(ya, yb, proj, xc, w_na_o[l].astype(BF16), _swa_head_order(w_win_o[l], axis=0).astype(BF16),
                    w_out[l].astype(BF16))
        memkv = _memkv(mem2d, g_mem[l].reshape(1, d), w_ckv[l].astype(BF16), mem_len)
        wr_hi = w_router[l].astype(BF16)
        wr_lo = (w_router[l] - wr_hi.astype(F32)).astype(BF16)
        x2, h3, logits = _cross(x1, g_cross[l].reshape(1, d), w_cq[l].astype(BF16), memkv, w_co[l].astype(BF16),
                                g_moe[l].reshape(1, d), wr_hi, wr_lo, b_router[l].reshape(1, -1), s, mem_len)
        top_w, toks, pos, tile_e, tile_valid, tile_c0 = _route(logits, MOE_TM)
        ys = _moe_ffn(tile_e, tile_valid, tile_c0, toks, h3, w_gate[l], b_gate[l], w_up[l], b_up[l],
                      w_down[l], b_down[l])
        xc = _combine(pos, x2, ys, top_w, g_final.reshape(1, d), final=(l + 1 == depth))
    return xc.reshape(b, s, d)
```

```python
import functools
from typing import NamedTuple

import jax
import jax.numpy as jnp
from jax import lax
from jax.experimental import pallas as pl
from jax.experimental.pallas import tpu as pltpu

F32 = jnp.float32
BF16 = jnp.bfloat16
U32 = jnp.uint32

GRID_W = 64
HEAD_DIM = 64
NA_HEADS = 16
NA_ROWS = 8
NA_COLS = 16
WIN_Q_HEADS = 16
WIN_KV_HEADS = 4
WINDOW = 128
ROT_DIM = HEAD_DIM // 4
ROPE_THETA = 500000.0
CROSS_HEADS = 4
CROSS_HEAD_DIM = 128
N_EXPERTS = 32
TOP_K = 4
SWIGLU_LIMIT = 7.0
SWIGLU_ALPHA = 1.702
NORM_EPS = 1e-5
NEG_INF = -1e30

LANES = 128
SUBLANES = 8
VMEM_LIMIT = 56 * 1024 * 1024

NA_WIDTH = NA_HEADS * HEAD_DIM
WIN_Q_WIDTH = WIN_Q_HEADS * HEAD_DIM
WIN_KV_WIDTH = WIN_KV_HEADS * HEAD_DIM
CROSS_WIDTH = CROSS_HEADS * CROSS_HEAD_DIM


class _Cols(NamedTuple):
    ga: int
    gb: int
    qa: int
    ka: int
    va: int
    qb: int
    kb: int
    vb: int
    end: int


def _cols(d):
    widths = [d, d, NA_WIDTH, NA_WIDTH, NA_WIDTH, WIN_Q_WIDTH, WIN_KV_WIDTH, WIN_KV_WIDTH]
    offs = [0]
    for w in widths:
        offs.append(offs[-1] + w)
    return _Cols(*offs)


PROJ_TM = 1024
PROJ_TN = 512
NA_UNROLL = 2
MERGE_TM = 256
CROSS_TM = 256
MOE_TM = 2304
MOE_TSB = 256
MOE_TF = 256
GATHER_UNROLL = 8
COMBINE_TM = 256


def _cparams(sem):
    return pltpu.CompilerParams(dimension_semantics=sem, vmem_limit_bytes=VMEM_LIMIT)


def _rms(x, g):
    ms = jnp.mean(x * x, axis=-1, keepdims=True)
    return x * lax.rsqrt(ms + NORM_EPS) * g


def _dot(a, b):
    return jnp.dot(a, b, preferred_element_type=F32)


def _dot_nt(a, b):
    return lax.dot_general(a, b, (((1,), (1,)), ((), ())), preferred_element_type=F32)


def _pack_halves(v):
    half = v.shape[1] // 2
    lo = lax.shift_right_logical(pltpu.bitcast(v[:, :half], U32), jnp.uint32(16))
    hi = pltpu.bitcast(v[:, half:], U32) & jnp.uint32(0xFFFF0000)
    return hi | lo


def _unpack_lo(p):
    return pltpu.bitcast(lax.shift_left(p, jnp.uint32(16)), F32)


def _unpack_hi(p):
    return pltpu.bitcast(p & jnp.uint32(0xFFFF0000), F32)


def _store_token_tiles(ref, packed):
    m, w = packed.shape
    n_sl = w // LANES
    for s in range(n_sl):
        ref[pl.ds(s, m, stride=n_sl), :] = packed[:, s * LANES:(s + 1) * LANES]


def _rope_chunk(x, c, sa, sb):
    return x * c + pltpu.roll(x, LANES - ROT_DIM // 2, 1) * sa + pltpu.roll(x, ROT_DIM // 2, 1) * sb


def _inproj_kernel(x_ref, g_ref, w_ref, c_ref, sa_ref, sb_ref, o_ref, h_ref, *, cols):
    j = pl.program_id(1)
    tn = o_ref.shape[1]
    nchunk = tn // LANES

    @pl.when(j == 0)
    def _():
        h_ref[...] = _rms(x_ref[...], g_ref[...]).astype(BF16)

    acc = _dot(h_ref[...], w_ref[...])
    o_ref[...] = acc.astype(BF16)

    j_qb, j_kb = cols.qb // tn, cols.kb // tn
    kb_chunks = WIN_KV_WIDTH // LANES

    @pl.when(j >= j_qb)
    def _():
        c, sa, sb = c_ref[...], sa_ref[...], sb_ref[...]

        def rotate(k):
            sl = slice(k * LANES, (k + 1) * LANES)
            o_ref[:, sl] = _rope_chunk(acc[:, sl], c, sa, sb).astype(BF16)

        for k in range(nchunk):
            if k < kb_chunks:
                rotate(k)
            else:
                pl.when(j < j_kb)(functools.partial(rotate, k))


def _in_proj(x2d, g, w_bf, rope_c, rope_sa, rope_sb, seq):
    t, d = x2d.shape
    n = w_bf.shape[1]
    cols = _cols(d)
    tm, tn = PROJ_TM, PROJ_TN
    assert t % tm == 0 and n % tn == 0 and seq % tm == 0 and n == cols.end
    assert all(c % tn == 0 for c in (cols.qa, cols.ka, cols.qb, cols.kb)) and cols.end == cols.kb + tn
    sblocks = seq // tm
    rope_spec = pl.BlockSpec((tm, LANES), lambda i, j: (i % sblocks, 0))
    return pl.pallas_call(
        functools.partial(_inproj_kernel, cols=cols),
        grid=(t // tm, n // tn),
        in_specs=[
            pl.BlockSpec((tm, d), lambda i, j: (i, 0)),
            pl.BlockSpec((1, d), lambda i, j: (0, 0)),
            pl.BlockSpec((d, tn), lambda i, j: (0, j)),
            rope_spec, rope_spec, rope_spec,
        ],
        out_specs=pl.BlockSpec((tm, tn), lambda i, j: (i, j)),
        out_shape=jax.ShapeDtypeStruct((t, n), BF16),
        scratch_shapes=[pltpu.VMEM((tm, d), BF16)],
        compiler_params=_cparams(("parallel", "arbitrary")),
        name="in_proj",
    )(x2d, g, w_bf, rope_c, rope_sa, rope_sb)


def _natten_kernel(q_ref, k_ref, v_ref, b_ref, o_ref, s_ref, p_ref, l_ref, *, rows):
    kr = NA_ROWS
    nu = NA_UNROLL
    lane = lax.broadcasted_iota(jnp.int32, (GRID_W, LANES), 1)
    first = lane < HEAD_DIM

    def window(r):
        rs = jnp.clip(r - kr // 2, 0, rows - kr)
        return pl.multiple_of(rs * GRID_W, GRID_W), rs - r + (NA_ROWS - 1)

    def stage_a(r, slot):
        k0, dr0 = window(r)
        q2 = q_ref[pl.ds(pl.multiple_of(r * GRID_W, GRID_W), GRID_W), :]
        k2 = k_ref[pl.ds(k0, kr * GRID_W), :]
        zero = jnp.zeros_like(q2)
        qs = jnp.concatenate([jnp.where(first, q2, zero), jnp.where(first, zero, q2)], axis=0)
        s_ref[slot] = _dot_nt(qs, k2) + b_ref[dr0]

    def stage_b(slot):
        s = s_ref[slot]
        m = jnp.max(s, axis=-1, keepdims=True)
        p = jnp.exp(s - m)
        l_ref[slot] = jnp.broadcast_to(1.0 / jnp.sum(p, axis=-1, keepdims=True), l_ref.shape[1:])
        p_ref[slot] = p.astype(BF16)

    def stage_c(r, slot):
        k0, _ = window(r)
        v2 = v_ref[pl.ds(k0, kr * GRID_W), :]
        o = _dot(p_ref[slot], v2) * l_ref[slot]
        o_ref[pl.ds(pl.multiple_of(r * GRID_W, GRID_W), GRID_W), :] = (
            jnp.where(first, o[:GRID_W], o[GRID_W:]).astype(BF16))

    def iteration(t, do_a, do_b, do_c):
        for j in range(nu):
            if do_c:
                stage_c((t - 2) * nu + j, j)
        for j in range(nu):
            if do_b:
                stage_b(j)
        for j in range(nu):
            if do_a:
                stage_a(t * nu + j, j)

    nt = rows // nu
    iteration(0, True, False, False)
    iteration(1, True, True, False)

    def body(t, carry):
        iteration(t, True, True, True)
        return carry

    lax.fori_loop(2, nt, body, 0)
    iteration(nt, False, True, True)
    iteration(nt + 1, False, False, True)


def _natten(proj, bias_tab, cols, batch, seq):
    npairs = NA_HEADS // 2
    rows = seq // GRID_W
    nu = NA_UNROLL
    assert rows >= NA_ROWS and rows % nu == 0 and rows // nu >= 2
    kq, kk, kv = cols.qa // LANES, cols.ka // LANES, cols.va // LANES
    blk = (seq, LANES)
    return pl.pallas_call(
        functools.partial(_natten_kernel, rows=rows),
        grid=(batch, npairs),
        in_specs=[
            pl.BlockSpec(blk, lambda b, p: (b, kq + p)),
            pl.BlockSpec(blk, lambda b, p: (b, kk + p)),
            pl.BlockSpec(blk, lambda b, p: (b, kv + p)),
            pl.BlockSpec((None, NA_ROWS, 2 * GRID_W, NA_ROWS * GRID_W), lambda b, p: (p, 0, 0, 0)),
        ],
        out_specs=pl.BlockSpec(blk, lambda b, p: (b, p)),
        out_shape=jax.ShapeDtypeStruct((batch * seq, NA_WIDTH), BF16),
        scratch_shapes=[
            pltpu.VMEM((nu, 2 * GRID_W, NA_ROWS * GRID_W), F32),
            pltpu.VMEM((nu, 2 * GRID_W, NA_ROWS * GRID_W), BF16),
            pltpu.VMEM((nu, 2 * GRID_W, LANES), F32),
        ],
        compiler_params=_cparams(("parallel", "parallel")),
        name="natten",
    )(proj, proj, proj, bias_tab)


def _natten_bias_table(rpb):
    cols = jnp.arange(GRID_W)
    col_start = jnp.clip(cols - NA_COLS // 2, 0, GRID_W - NA_COLS)
    kc = jnp.arange(GRID_W)
    inwin = (kc[None, :] >= col_start[:, None]) & (kc[None, :] < col_start[:, None] + NA_COLS)
    dc = kc[None, :] - cols[:, None] + (NA_COLS - 1)
    onehot = (dc[:, :, None] == jnp.arange(2 * NA_COLS - 1)[None, None, :]).astype(F32)
    exp_c = jnp.einsum("ckd,hrd->hrck", onehot, rpb.astype(F32), precision=lax.Precision.HIGHEST)
    tab = jnp.stack([exp_c[:, v:v + NA_ROWS] for v in range(NA_ROWS)], axis=1)
    tab = jnp.where(inwin[None, None, None], tab, NEG_INF)
    tab = tab.transpose(0, 1, 3, 2, 4).reshape(NA_HEADS, NA_ROWS, GRID_W, NA_ROWS * GRID_W)
    tab = tab.reshape(NA_HEADS // 2, 2, NA_ROWS, GRID_W, NA_ROWS * GRID_W).transpose(0, 2, 1, 3, 4)
    return tab.reshape(NA_HEADS // 2, NA_ROWS, 2 * GRID_W, NA_ROWS * GRID_W)


def _swa_kernel(sink_ref, q_ref, k_ref, v_ref, mask_ref, o_ref, s_ref, p_ref, l_ref, *, nb):
    kp = pl.program_id(1)
    w = WINDOW
    grp = WIN_Q_HEADS // WIN_KV_HEADS
    lane = lax.broadcasted_iota(jnp.int32, (w, LANES), 1)
    first = lane < HEAD_DIM
    rowhalf = lax.broadcasted_iota(jnp.int32, (2 * w, 1), 0) < w

    def window(n):
        start = jnp.clip(n - 1, 0, nb - 3)
        return pl.multiple_of(start * w, w), n - start

    def stage_a(n, c):
        k0, variant = window(n)
        q2 = q_ref[pl.ds(pl.multiple_of(n * w, w), w), c * LANES:(c + 1) * LANES]
        zero = jnp.zeros_like(q2)
        qs = jnp.concatenate([jnp.where(first, q2, zero), jnp.where(first, zero, q2)], axis=0)
        s_ref[c] = _dot_nt(qs, k_ref[pl.ds(k0, 3 * w), :]) + mask_ref[variant]

    def stage_b(c):
        s = s_ref[c]
        sink = jnp.where(rowhalf, sink_ref[(2 * kp) * grp + c], sink_ref[(2 * kp + 1) * grp + c])
        m = jnp.maximum(jnp.max(s, axis=-1, keepdims=True), sink)
        p = jnp.exp(s - m)
        denom = jnp.sum(p, axis=-1, keepdims=True) + jnp.exp(sink - m)
        l_ref[c] = jnp.broadcast_to(1.0 / denom, l_ref.shape[1:])
        p_ref[c] = p.astype(BF16)

    def stage_c(n, c):
        k0, _ = window(n)
        o = _dot(p_ref[c], v_ref[pl.ds(k0, 3 * w), :]) * l_ref[c]
        o_ref[pl.ds(pl.multiple_of(n * w, w), w), c * LANES:(c + 1) * LANES] = (
            jnp.where(first, o[:w], o[w:]).astype(BF16))

    def iteration(n, do_a, do_b, do_c):
        for c in range(grp):
            if do_c:
                stage_c(n - 2, c)
        for c in range(grp):
            if do_b:
                stage_b(c)
        for c in range(grp):
            if do_a:
                stage_a(n, c)

    iteration(0, True, False, False)
    iteration(1, True, True, False)

    def body(n, carry):
        iteration(n, True, True, True)
        return carry

    lax.fori_loop(2, nb, body, 0)
    iteration(nb, False, True, True)
    iteration(nb + 1, False, False, True)


def _swa_mask_table():
    w = WINDOW
    qi = jnp.arange(2 * w) % w
    kj = jnp.arange(3 * w)
    off = kj[None, None, :] - w * jnp.arange(3)[:, None, None] - qi[None, :, None]
    return jnp.where(jnp.abs(off) <= w, 0.0, NEG_INF).astype(F32)


def _swa(proj, sinks, cols, batch, seq):
    w = WINDOW
    nb = seq // w
    grp = WIN_Q_HEADS // WIN_KV_HEADS
    qw = grp * LANES
    assert cols.qb % qw == 0 and nb >= 3
    cq, ck, cv = cols.qb // qw, cols.kb // LANES, cols.vb // LANES
    return pl.pallas_call(
        functools.partial(_swa_kernel, nb=nb),
        grid=(batch, WIN_KV_HEADS // 2),
        in_specs=[
            pl.BlockSpec(memory_space=pltpu.SMEM),
            pl.BlockSpec((seq, qw), lambda b, kp: (b, cq + kp)),
            pl.BlockSpec((seq, LANES), lambda b, kp: (b, ck + kp)),
            pl.BlockSpec((seq, LANES), lambda b, kp: (b, cv + kp)),
            pl.BlockSpec((3, 2 * w, 3 * w), lambda b, kp: (0, 0, 0)),
        ],
        out_specs=pl.BlockSpec((seq, qw), lambda b, kp: (b, kp)),
        out_shape=jax.ShapeDtypeStruct((batch * seq, WIN_Q_WIDTH), BF16),
        scratch_shapes=[
            pltpu.VMEM((grp, 2 * w, 3 * w), F32),
            pltpu.VMEM((grp, 2 * w, 3 * w), BF16),
            pltpu.VMEM((grp, 2 * w, LANES), F32),
        ],
        compiler_params=_cparams(("parallel", "parallel")),
        name="swa",
    )(sinks, proj, proj, proj, _swa_mask_table())


def _swa_head_order(w, axis):
    grp = WIN_Q_HEADS // WIN_KV_HEADS
    shape = w.shape
    split = shape[:axis] + (WIN_KV_HEADS // 2, 2, grp, HEAD_DIM) + shape[axis + 1:]
    return jnp.swapaxes(w.reshape(split), axis + 1, axis + 2).reshape(shape)


def _merge_kernel(ya_ref, yb_ref, ga_ref, gb_ref, x_ref, wna_ref, wwin_ref, wout_ref, o_ref):
    a = _dot(ya_ref[...], wna_ref[...])
    b = _dot(yb_ref[...], wwin_ref[...])
    merged = jax.nn.sigmoid(ga_ref[...].astype(F32)) * a + jax.nn.sigmoid(gb_ref[...].astype(F32)) * b
    o_ref[...] = x_ref[...] + _dot(merged.astype(BF16), wout_ref[...])


def _resident(shape):
    return pl.BlockSpec(shape, lambda i: (0,) * len(shape), pipeline_mode=pl.Buffered(1))


def _merge(ya, yb, proj, x2d, wna, wwin, wout):
    t, d = x2d.shape
    tm = MERGE_TM
    return pl.pallas_call(
        _merge_kernel,
        grid=(t // tm,),
        in_specs=[
            pl.BlockSpec((tm, NA_WIDTH), lambda i: (i, 0)),
            pl.BlockSpec((tm, WIN_Q_WIDTH), lambda i: (i, 0)),
            pl.BlockSpec((tm, d), lambda i: (i, 0)),
            pl.BlockSpec((tm, d), lambda i: (i, 1)),
            pl.BlockSpec((tm, d), lambda i: (i, 0)),
            _resident(wna.shape), _resident(wwin.shape), _resident(wout.shape),
        ],
        out_specs=pl.BlockSpec((tm, d), lambda i: (i, 0)),
        out_shape=jax.ShapeDtypeStruct((t, d), F32),
        compiler_params=_cparams(("parallel",)),
        name="merge",
    )(ya, yb, proj, proj, x2d, wna, wwin, wout)


def _memkv_kernel(m_ref, g_ref, w_ref, o_ref):
    o_ref[...] = _dot(_rms(m_ref[...], g_ref[...]).astype(BF16), w_ref[...]).astype(BF16)


def _memkv(mem2d, g, w_bf, mem_len):
    t, d = mem2d.shape
    n = w_bf.shape[1]
    return pl.pallas_call(
        _memkv_kernel,
        grid=(t // mem_len,),
        in_specs=[
            pl.BlockSpec((mem_len, d), lambda i: (i, 0)),
            pl.BlockSpec((1, d), lambda i: (0, 0)),
            pl.BlockSpec((d, n), lambda i: (0, 0)),
        ],
        out_specs=pl.BlockSpec((mem_len, n), lambda i: (i, 0)),
        out_shape=jax.ShapeDtypeStruct((t, n), BF16),
        compiler_params=_cparams(("parallel",)),
        name="memkv",
    )(mem2d, g, w_bf)


def _cross_kernel(x_ref, gc_ref, wcq_ref, k_ref, v_ref, wco_ref, gm_ref, wrh_ref, wrl_ref, br_ref,
                  x2_ref, h_ref, lg_ref):
    x = x_ref[...]
    q = _dot(_rms(x, gc_ref[...]).astype(BF16), wcq_ref[...]).astype(BF16)
    scale = CROSS_HEAD_DIM ** -0.5
    outs = []
    for h in range(CROSS_HEADS):
        sl = slice(h * CROSS_HEAD_DIM, (h + 1) * CROSS_HEAD_DIM)
        s = _dot_nt(q[:, sl], k_ref[:, sl]) * scale
        m = jnp.max(s, axis=-1, keepdims=True)
        p = jnp.exp(s - m)
        l = jnp.sum(p, axis=-1, keepdims=True)
        outs.append((_dot(p.astype(BF16), v_ref[:, sl]) * (1.0 / l)).astype(BF16))
    o = jnp.concatenate(outs, axis=-1)
    x2 = x + _dot(o, wco_ref[...])
    x2_ref[...] = x2
    h = _rms(x2, gm_ref[...])
    h_hi = h.astype(BF16)
    h_lo = (h - h_hi.astype(F32)).astype(BF16)
    _store_token_tiles(h_ref, _pack_halves(h_hi.astype(F32)))
    lg_ref[...] = (_dot(h_hi, wrh_ref[...]) + _dot(h_hi, wrl_ref[...]) + _dot(h_lo, wrh_ref[...])
                   + br_ref[...])


def _cross(x1, gc, wcq, memkv, wco, gm, wr_hi, wr_lo, br, seq, mem_len):
    t, d = x1.shape
    tm = CROSS_TM
    ne = wr_hi.shape[1]
    per_b = seq // tm
    row = lambda i: (i, 0)
    return pl.pallas_call(
        _cross_kernel,
        grid=(t // tm,),
        in_specs=[
            pl.BlockSpec((tm, d), row),
            _resident((1, d)),
            _resident(wcq.shape),
            pl.BlockSpec((mem_len, CROSS_WIDTH), lambda i: (i // per_b, 0)),
            pl.BlockSpec((mem_len, CROSS_WIDTH), lambda i: (i // per_b, 1)),
            _resident(wco.shape),
            _resident((1, d)),
            _resident(wr_hi.shape), _resident(wr_lo.shape),
            _resident((1, ne)),
        ],
        out_specs=[
            pl.BlockSpec((tm, d), row),
            pl.BlockSpec((tm * SUBLANES, LANES), row),
            pl.BlockSpec((tm, ne), row),
        ],
        out_shape=[
            jax.ShapeDtypeStruct((t, d), F32),
            jax.ShapeDtypeStruct((t * SUBLANES, LANES), U32),
            jax.ShapeDtypeStruct((t, ne), F32),
        ],
        compiler_params=_cparams(("parallel",)),
        name="cross",
    )(x1, gc, wcq, memkv, memkv, wco, gm, wr_hi, wr_lo, br)


def _moe_kernel(te_ref, tv_ref, tc_ref, tok_ref, h_hbm, wg_ref, wu_ref, wd_ref, bg_ref, bu_ref, bd_ref, y_hbm,
                raw_ref, xb_ref, a_ref, w1_ref, w2_ref, ys_ref, gsem, osem, *, nf, n_tiles):
    i = pl.program_id(0)
    s = pl.program_id(1)
    tsb = MOE_TSB
    rpt = SUBLANES
    tm = xb_ref.shape[0]
    tf = wg_ref.shape[1]
    half = xb_ref.shape[1] // 2
    valid = tv_ref[i]
    n_sb = (valid + tsb - 1) // tsb

    def row_copy(tok, k):
        src = h_hbm.at[pl.ds(pl.multiple_of(tok * rpt, rpt), rpt)]
        dst = raw_ref.at[pl.ds(pl.multiple_of(k * rpt, rpt), rpt)]
        return pltpu.make_async_copy(src, dst, gsem)

    def issue_gather(t):
        v = tv_ref[t]
        c0 = tc_ref[t]

        def group(gi, c):
            for u in range(GATHER_UNROLL):
                k = gi * GATHER_UNROLL + u
                row_copy(tok_ref[c0 + k], k).start(priority=u % 2)
            return c

        def single(k, c):
            row_copy(tok_ref[c0 + k], k).start()
            return c

        def pad(k, c):
            row_copy(0, k).start()
            return c

        n_grp = lax.shift_right_logical(v, GATHER_UNROLL.bit_length() - 1)
        lax.fori_loop(0, n_grp, group, 0)
        lax.fori_loop(n_grp * GATHER_UNROLL, v, single, 0)
        lax.fori_loop(v, (v + tsb - 1) // tsb * tsb, pad, 0)

    def block_rows(ref, sb):
        return ref.at[pl.ds(pl.multiple_of(sb * (tsb * rpt), tsb * rpt), tsb * rpt)]

    def out_copy(t):
        dst = y_hbm.at[pl.ds(pl.multiple_of(t * (tm * rpt), tm * rpt), tm * rpt)]
        return pltpu.make_async_copy(ys_ref, dst, osem)

    @pl.when((i == 0) & (s == 0))
    def _():
        ys_ref[...] = jnp.zeros(ys_ref.shape, U32)
        issue_gather(0)

    @pl.when(s == 0)
    def _():
        def wait_rows(sb, c):
            pltpu.make_async_copy(block_rows(h_hbm, 0), block_rows(raw_ref, sb), gsem).wait()
            return c

        lax.fori_loop(0, n_sb, wait_rows, 0)

        def unpack(sb, c):
            r0 = pl.multiple_of(sb * tsb, tsb)
            for sl in range(rpt):
                p = raw_ref[pl.ds(r0 * rpt + sl, tsb, stride=rpt), :]
                xb_ref[pl.ds(r0, tsb), sl * LANES:(sl + 1) * LANES] = _unpack_lo(p).astype(BF16)
                xb_ref[pl.ds(r0, tsb), half + sl * LANES:half + (sl + 1) * LANES] = _unpack_hi(p).astype(BF16)
            return c

        lax.fori_loop(0, n_sb, unpack, 0)

    @pl.when((s == 1) & (i + 1 < n_tiles))
    def _():
        issue_gather(i + 1)

    def for_row_blocks(fn):
        def quad(b, c):
            fn(pl.multiple_of(b * (4 * tsb), 4 * tsb), 4 * tsb)
            return c

        n_quad = lax.shift_right_logical(n_sb, 2)
        lax.fori_loop(0, n_quad, quad, 0)

        @pl.when((n_sb & 2) != 0)
        def _():
            fn(pl.multiple_of(n_quad * (4 * tsb), 2 * tsb), 2 * tsb)

        @pl.when((n_sb & 1) != 0)
        def _():
            fn(pl.multiple_of((n_sb - 1) * tsb, tsb), tsb)

    @pl.when((s < nf) & (valid > 0))
    def _():
        w1_ref[...] = wg_ref[...].astype(BF16)
        w2_ref[...] = wu_ref[...].astype(BF16)

        def gate_up(r0, rows):
            xs = xb_ref[pl.ds(r0, rows), :]
            g = _dot(xs, w1_ref[...]) + bg_ref[...]
            u = _dot(xs, w2_ref[...]) + bu_ref[...]
            g = jnp.minimum(g, SWIGLU_LIMIT)
            u = jnp.clip(u, -SWIGLU_LIMIT, SWIGLU_LIMIT)
            a_ref[s, pl.ds(r0, rows), :] = ((u + 1.0) * (g * jax.nn.sigmoid(SWIGLU_ALPHA * g))).astype(BF16)

        for_row_blocks(gate_up)

    @pl.when((s == nf) & (i > 0))
    def _():
        out_copy(i - 1).wait()

    @pl.when((s >= nf) & (valid > 0))
    def _():
        n = s - nf
        w1_ref[...] = wd_ref[...].astype(BF16)

        def down(r0, rows):
            acc = bd_ref[...] + _dot(a_ref[0, pl.ds(r0, rows), :], w1_ref[0:tf, :])
            for f in range(1, nf):
                acc = acc + _dot(a_ref[f, pl.ds(r0, rows), :], w1_ref[f * tf:(f + 1) * tf, :])
            packed = _pack_halves(acc.astype(BF16).astype(F32))
            for j in range(tf // 2 // LANES):
                ys_ref[pl.ds(r0 * rpt + n * (tf // 2 // LANES) + j, rows, stride=rpt), :] = (
                    packed[:, j * LANES:(j + 1) * LANES])

        for_row_blocks(down)

    @pl.when(s == 2 * nf - 1)
    def _():
        out_copy(i).start()

        @pl.when(i == n_tiles - 1)
        def _():
            out_copy(i).wait()


def _moe_ffn(tile_e, tile_valid, tile_c0, toks, h_tiles, w_gate, b_gate, w_up, b_up, w_down, b_down):
    e, d, dff = w_gate.shape
    tm, tf = MOE_TM, MOE_TF
    rpt = SUBLANES
    assert d == dff, "the weight staging buffers are shared between the up and down projections"
    assert d == 2 * rpt * LANES, "a packed token row must be exactly one (8, 128) tile"
    assert tm % MOE_TSB == 0 and dff % tf == 0 and (tf // 2) % LANES == 0
    nf = dff // tf
    assert nf >= 2
    n_tiles = tile_e.shape[0]
    up = lambda i, s, te, tv, tc, tk: (te[i], 0, jnp.minimum(s, nf - 1))
    dn = lambda i, s, te, tv, tc, tk: (te[i], 0, jnp.maximum(s - nf, 0))
    grid_spec = pltpu.PrefetchScalarGridSpec(
        num_scalar_prefetch=4,
        grid=(n_tiles, 2 * nf),
        in_specs=[
            pl.BlockSpec(memory_space=pl.ANY),
            pl.BlockSpec((None, d, tf), up),
            pl.BlockSpec((None, d, tf), up),
            pl.BlockSpec((None, dff, tf), dn),
            pl.BlockSpec((None, 1, tf), up),
            pl.BlockSpec((None, 1, tf), up),
            pl.BlockSpec((None, 1, tf), dn),
        ],
        out_specs=pl.BlockSpec(memory_space=pl.ANY),
        scratch_shapes=[
            pltpu.VMEM((tm * rpt, LANES), U32),
            pltpu.VMEM((tm, d), BF16),
            pltpu.VMEM((nf, tm, tf), BF16),
            pltpu.VMEM((d, tf), BF16),
            pltpu.VMEM((d, tf), BF16),
            pltpu.VMEM((tm * rpt, LANES), U32),
            pltpu.SemaphoreType.DMA(()),
            pltpu.SemaphoreType.DMA(()),
        ],
    )
    return pl.pallas_call(
        functools.partial(_moe_kernel, nf=nf, n_tiles=n_tiles),
        grid_spec=grid_spec,
        out_shape=jax.ShapeDtypeStruct((n_tiles * tm * rpt, LANES), U32),
        compiler_params=_cparams(("arbitrary", "arbitrary")),
        name="moe_ffn",
    )(tile_e, tile_valid, tile_c0, toks, h_tiles, w_gate, w_up, w_down,
      b_gate.reshape(e, 1, dff), b_up.reshape(e, 1, dff), b_down.reshape(e, 1, d))


def _combine_kernel(pos_ref, x_ref, y_hbm, w_ref, g_ref, o_ref, buf_ref, sem, *, final, n_steps, tf):
    i = pl.program_id(0)
    tm, d = x_ref.shape
    rpt = SUBLANES
    slot = i % 2

    def issue(step, sl_):
        def body(pair, c):
            for u in range(2):
                tt = pair * 2 + u
                for k in range(TOP_K):
                    p = pos_ref[(step * tm + tt) * TOP_K + k]
                    src = y_hbm.at[pl.ds(pl.multiple_of(p * rpt, rpt), rpt)]
                    dst = buf_ref.at[sl_, pl.ds(pl.multiple_of((k * tm + tt) * rpt, rpt), rpt)]
                    pltpu.make_async_copy(src, dst, sem.at[sl_]).start(priority=k % 2)
            return c

        lax.fori_loop(0, tm // 2, body, 0)

    @pl.when(i == 0)
    def _():
        issue(0, 0)

    @pl.when(i + 1 < n_steps)
    def _():
        issue(i + 1, 1 - slot)

    pltpu.make_async_copy(y_hbm.at[pl.ds(0, TOP_K * tm * rpt)], buf_ref.at[slot], sem.at[slot]).wait()

    w = w_ref[...]
    wk = [jnp.broadcast_to(w[:, k:k + 1], (tm, LANES)) for k in range(TOP_K)]
    per_chunk = tf // 2 // LANES
    slabs = [None] * (d // LANES)
    for sl in range(rpt):
        n, j = sl // per_chunk, sl % per_chunk
        c_lo = (n * tf) // LANES + j
        c_hi = c_lo + per_chunk
        lo = x_ref[:, c_lo * LANES:(c_lo + 1) * LANES]
        hi = x_ref[:, c_hi * LANES:(c_hi + 1) * LANES]
        for k in range(TOP_K):
            p = buf_ref[slot, pl.ds(k * tm * rpt + sl, tm, stride=rpt), :]
            lo = lo + wk[k] * _unpack_lo(p)
            hi = hi + wk[k] * _unpack_hi(p)
        slabs[c_lo], slabs[c_hi] = lo, hi
    acc = jnp.concatenate(slabs, axis=1)
    o_ref[...] = _rms(acc, g_ref[...]) if final else acc


def _combine(pos, x2, y_tiles, top_w, g, final):
    t, d = x2.shape
    tm = COMBINE_TM
    rpt = SUBLANES
    n_steps = t // tm
    grid_spec = pltpu.PrefetchScalarGridSpec(
        num_scalar_prefetch=1,
        grid=(n_steps,),
        in_specs=[
            pl.BlockSpec((tm, d), lambda i, pos: (i, 0)),
            pl.BlockSpec(memory_space=pl.ANY),
            pl.BlockSpec((tm, TOP_K), lambda i, pos: (i, 0)),
            pl.BlockSpec((1, d), lambda i, pos: (0, 0)),
        ],
        out_specs=pl.BlockSpec((tm, d), lambda i, pos: (i, 0)),
        scratch_shapes=[
            pltpu.VMEM((2, TOP_K * tm * rpt, LANES), U32),
            pltpu.SemaphoreType.DMA((2,)),
        ],
    )
    return pl.pallas_call(
        functools.partial(_combine_kernel, final=final, n_steps=n_steps, tf=MOE_TF),
        grid_spec=grid_spec,
        out_shape=jax.ShapeDtypeStruct((t, d), F32),
        compiler_params=_cparams(("arbitrary",)),
        name="combine",
    )(pos, x2, y_tiles, top_w, g)


def _rope_tables(seq):
    half = ROT_DIM // 2
    inv = ROPE_THETA ** (-(jnp.arange(half, dtype=F32) * 2.0 / ROT_DIM))
    ang = jnp.arange(seq, dtype=F32)[:, None] * inv[None, :]
    cos, sin = jnp.cos(ang), jnp.sin(ang)
    ones = jnp.ones((seq, HEAD_DIM - ROT_DIM), F32)
    zeros = jnp.zeros((seq, HEAD_DIM - ROT_DIM), F32)
    zh = jnp.zeros((seq, half), F32)
    c = jnp.concatenate([cos, cos, ones], axis=1)
    sa = jnp.concatenate([-sin, zh, zeros], axis=1)
    sb = jnp.concatenate([zh, sin, zeros], axis=1)
    rep = LANES // HEAD_DIM
    return jnp.tile(c, (1, rep)), jnp.tile(sa, (1, rep)), jnp.tile(sb, (1, rep))


def _route(logits, tm):
    t = logits.shape[0]
    i32 = jnp.int32
    top_val, top_idx = lax.top_k(logits, TOP_K)
    top_w = jax.nn.softmax(top_val, axis=-1)
    n_assign = t * TOP_K
    flat_e = top_idx.reshape(n_assign).astype(i32)
    order = jnp.argsort(flat_e).astype(i32)
    rank = jnp.argsort(order).astype(i32)
    onehot = flat_e[:, None] == jnp.arange(N_EXPERTS, dtype=i32)[None, :]
    counts = jnp.sum(onehot, axis=0, dtype=i32)
    start = jnp.cumsum(counts) - counts
    tiles_e = (counts + tm - 1) // tm
    tile_end = jnp.cumsum(tiles_e)
    tile_first = tile_end - tiles_e
    n_tiles = n_assign // tm + N_EXPERTS
    j = jnp.arange(n_tiles, dtype=i32)
    tile_e = jnp.minimum(jnp.searchsorted(tile_end, j, side="right"), N_EXPERTS - 1).astype(i32)
    local = j - tile_first[tile_e]
    tile_valid = jnp.clip(counts[tile_e] - local * tm, 0, tm).astype(i32)
    tile_c0 = jnp.where(tile_valid > 0, start[tile_e] + local * tm, 0).astype(i32)
    last_e = jnp.max(jnp.where(tile_valid > 0, tile_e, 0))
    tile_e = jnp.where(tile_valid > 0, tile_e, last_e).astype(i32)
    row_base = jnp.sum(jnp.where(onehot, (tile_first * tm - start)[None, :], 0), axis=1, dtype=i32)
    pos = rank + row_base
    return top_w, order // TOP_K, pos, tile_e, tile_valid, tile_c0


def kernel(x, mem, g_mix, w_in, rpb_na, sinks, w_na_o, w_win_o, w_out, g_cross, g_mem, w_cq, w_ckv, w_co,
           g_moe, w_router, b_router, w_gate, b_gate, w_up, b_up, w_down, b_down, g_final):
    b, s, d = x.shape
    mem_len = mem.shape[1]
    t = b * s
    depth = w_in.shape[0]
    xc = x.reshape(t, d)
    mem2d = mem.reshape(b * mem_len, d)
    rope_c, rope_sa, rope_sb = _rope_tables(s)

    cols = _cols(d)
    n_attn = cols.end - cols.qa
    scale = HEAD_DIM ** -0.5
    o_qb, o_kb = 3 * NA_WIDTH, 3 * NA_WIDTH + WIN_Q_WIDTH

    for l in range(depth):
        w = w_in[l]
        w_in_l = jnp.concatenate([w[:, n_attn:], w[:, :NA_WIDTH] * scale, w[:, NA_WIDTH:o_qb],
                                  _swa_head_order(w[:, o_qb:o_kb] * scale, axis=1), w[:, o_kb:n_attn]],
                                 axis=1).astype(BF16)
        proj = _in_proj(xc, g_mix[l].reshape(1, d), w_in_l, rope_c, rope_sa, rope_sb, s)
        ya = _natten(proj, _natten_bias_table(rpb_na[l]), cols, b, s)
        yb = _swa(proj, sinks[l].astype(F32), cols, b, s)
        x1 = _merge(ya, yb, proj, xc, w_na_o[l].astype(BF16), _swa_head_order(w_win_o[l], axis=0).astype(BF16),
                    w_out[l].astype(BF16))
        memkv = _memkv(mem2d, g_mem[l].reshape(1, d), w_ckv[l].astype(BF16), mem_len)
        wr_hi = w_router[l].astype(BF16)
        wr_lo = (w_router[l] - wr_hi.astype(F32)).astype(BF16)
        x2, h3, logits = _cross(x1, g_cross[l].reshape(1, d), w_cq[l].astype(BF16), memkv, w_co[l].astype(BF16),
                                g_moe[l].reshape(1, d), wr_hi, wr_lo, b_router[l].reshape(1, -1), s, mem_len)
        top_w, toks, pos, tile_e, tile_valid, tile_c0 = _route(logits, MOE_TM)
        ys = _moe_ffn(tile_e, tile_valid, tile_c0, toks, h3, w_gate[l], b_gate[l], w_up[l], b_up[l],
                      w_down[l], b_down[l])
        xc = _combine(pos, x2, ys, top_w, g_final.reshape(1, d), final=(l + 1 == depth))
    return xc.reshape(b, s, d)
```

```python
import functools
from typing import NamedTuple

import jax
import jax.numpy as jnp
from jax import lax
from jax.experimental import pallas as pl
from jax.experimental.pallas import tpu as pltpu

F32 = jnp.float32
BF16 = jnp.bfloat16
U32 = jnp.uint32

GRID_W = 64
HEAD_DIM = 64
NA_HEADS = 16
NA_ROWS = 8
NA_COLS = 16
WIN_Q_HEADS = 16
WIN_KV_HEADS = 4
WINDOW = 128
ROT_DIM = HEAD_DIM // 4
ROPE_THETA = 500000.0
CROSS_HEADS = 4
CROSS_HEAD_DIM = 128
N_EXPERTS = 32
TOP_K = 4
SWIGLU_LIMIT = 7.0
SWIGLU_ALPHA = 1.702
NORM_EPS = 1e-5
NEG_INF = -1e30

LANES = 128
SUBLANES = 8
VMEM_LIMIT = 56 * 1024 * 1024

NA_WIDTH = NA_HEADS * HEAD_DIM
WIN_Q_WIDTH = WIN_Q_HEADS * HEAD_DIM
WIN_KV_WIDTH = WIN_KV_HEADS * HEAD_DIM
CROSS_WIDTH = CROSS_HEADS * CROSS_HEAD_DIM


class _Cols(NamedTuple):
    ga: int
    gb: int
    qa: int
    ka: int
    va: int
    qb: int
    kb: int
    vb: int
    end: int


def _cols(d):
    widths = [d, d, NA_WIDTH, NA_WIDTH, NA_WIDTH, WIN_Q_WIDTH, WIN_KV_WIDTH, WIN_KV_WIDTH]
    offs = [0]
    for w in widths:
        offs.append(offs[-1] + w)
    return _Cols(*offs)


PROJ_TM = 1024
PROJ_TN = 512
NA_UNROLL = 2
MERGE_TM = 256
CROSS_TM = 256
MOE_TM = 2304
MOE_TSB = 256
MOE_TF = 256
GATHER_UNROLL = 8
MOE_BIG_BLOCK = 8
MOE_BLOCK_COPIES = 144
MOE_GATE_SPLIT = 4
COMBINE_TM = 256


def _cparams(sem):
    return pltpu.CompilerParams(dimension_semantics=sem, vmem_limit_bytes=VMEM_LIMIT)


def _rms(x, g):
    ms = jnp.mean(x * x, axis=-1, keepdims=True)
    return x * lax.rsqrt(ms + NORM_EPS) * g


def _dot(a, b):
    return jnp.dot(a, b, preferred_element_type=F32)


def _dot_nt(a, b):
    return lax.dot_general(a, b, (((1,), (1,)), ((), ())), preferred_element_type=F32)


def _pack_halves(v):
    half = v.shape[1] // 2
    lo = lax.shift_right_logical(pltpu.bitcast(v[:, :half], U32), jnp.uint32(16))
    hi = pltpu.bitcast(v[:, half:], U32) & jnp.uint32(0xFFFF0000)
    return hi | lo


def _unpack_lo(p):
    return pltpu.bitcast(lax.shift_left(p, jnp.uint32(16)), F32)


def _unpack_hi(p):
    return pltpu.bitcast(p & jnp.uint32(0xFFFF0000), F32)


def _store_token_tiles(ref, packed):
    m, w = packed.shape
    n_sl = w // LANES
    for s in range(n_sl):
        ref[pl.ds(s, m, stride=n_sl), :] = packed[:, s * LANES:(s + 1) * LANES]


def _rope_chunk(x, c, sa, sb):
    return x * c + pltpu.roll(x, LANES - ROT_DIM // 2, 1) * sa + pltpu.roll(x, ROT_DIM // 2, 1) * sb


def _inproj_kernel(x_ref, g_ref, w_ref, c_ref, sa_ref, sb_ref, o_ref, h_ref, *, cols):
    j = pl.program_id(1)
    tn = o_ref.shape[1]
    nchunk = tn // LANES

    @pl.when(j == 0)
    def _():
        h_ref[...] = _rms(x_ref[...], g_ref[...]).astype(BF16)

    acc = _dot(h_ref[...], w_ref[...])
    o_ref[...] = acc.astype(BF16)

    j_qb, j_kb = cols.qb // tn, cols.kb // tn
    kb_chunks = WIN_KV_WIDTH // LANES

    @pl.when(j >= j_qb)
    def _():
        c, sa, sb = c_ref[...], sa_ref[...], sb_ref[...]

        def rotate(k):
            sl = slice(k * LANES, (k + 1) * LANES)
            o_ref[:, sl] = _rope_chunk(acc[:, sl], c, sa, sb).astype(BF16)

        for k in range(nchunk):
            if k < kb_chunks:
                rotate(k)
            else:
                pl.when(j < j_kb)(functools.partial(rotate, k))


def _in_proj(x2d, g, w_bf, rope_c, rope_sa, rope_sb, seq):
    t, d = x2d.shape
    n = w_bf.shape[1]
    cols = _cols(d)
    tm, tn = PROJ_TM, PROJ_TN
    assert t % tm == 0 and n % tn == 0 and seq % tm == 0 and n == cols.end
    assert all(c % tn == 0 for c in (cols.qa, cols.ka, cols.qb, cols.kb)) and cols.end == cols.kb + tn
    sblocks = seq // tm
    rope_spec = pl.BlockSpec((tm, LANES), lambda i, j: (i % sblocks, 0))
    return pl.pallas_call(
        functools.partial(_inproj_kernel, cols=cols),
        grid=(t // tm, n // tn),
        in_specs=[
            pl.BlockSpec((tm, d), lambda i, j: (i, 0)),
            pl.BlockSpec((1, d), lambda i, j: (0, 0)),
            pl.BlockSpec((d, tn), lambda i, j: (0, j)),
            rope_spec, rope_spec, rope_spec,
        ],
        out_specs=pl.BlockSpec((tm, tn), lambda i, j: (i, j)),
        out_shape=jax.ShapeDtypeStruct((t, n), BF16),
        scratch_shapes=[pltpu.VMEM((tm, d), BF16)],
        compiler_params=_cparams(("parallel", "arbitrary")),
        name="in_proj",
    )(x2d, g, w_bf, rope_c, rope_sa, rope_sb)


def _natten_kernel(q_ref, k_ref, v_ref, b_ref, o_ref, s_ref, p_ref, l_ref, *, rows):
    kr = NA_ROWS
    nu = NA_UNROLL
    lane = lax.broadcasted_iota(jnp.int32, (GRID_W, LANES), 1)
    first = lane < HEAD_DIM

    def window(r):
        rs = jnp.clip(r - kr // 2, 0, rows - kr)
        return pl.multiple_of(rs * GRID_W, GRID_W), rs - r + (NA_ROWS - 1)

    def stage_a(r, slot):
        k0, dr0 = window(r)
        q2 = q_ref[pl.ds(pl.multiple_of(r * GRID_W, GRID_W), GRID_W), :]
        k2 = k_ref[pl.ds(k0, kr * GRID_W), :]
        zero = jnp.zeros_like(q2)
        qs = jnp.concatenate([jnp.where(first, q2, zero), jnp.where(first, zero, q2)], axis=0)
        s_ref[slot] = _dot_nt(qs, k2) + b_ref[dr0]

    def stage_b(slot):
        s = s_ref[slot]
        m = jnp.max(s, axis=-1, keepdims=True)
        p = jnp.exp(s - m)
        l_ref[slot] = jnp.broadcast_to(1.0 / jnp.sum(p, axis=-1, keepdims=True), l_ref.shape[1:])
        p_ref[slot] = p.astype(BF16)

    def stage_c(r, slot):
        k0, _ = window(r)
        v2 = v_ref[pl.ds(k0, kr * GRID_W), :]
        o = _dot(p_ref[slot], v2) * l_ref[slot]
        o_ref[pl.ds(pl.multiple_of(r * GRID_W, GRID_W), GRID_W), :] = (
            jnp.where(first, o[:GRID_W], o[GRID_W:]).astype(BF16))

    def iteration(t, do_a, do_b, do_c):
        for j in range(nu):
            if do_c:
                stage_c((t - 2) * nu + j, j)
        for j in range(nu):
            if do_b:
                stage_b(j)
        for j in range(nu):
            if do_a:
                stage_a(t * nu + j, j)

    nt = rows // nu
    iteration(0, True, False, False)
    iteration(1, True, True, False)

    def body(t, carry):
        iteration(t, True, True, True)
        return carry

    lax.fori_loop(2, nt, body, 0)
    iteration(nt, False, True, True)
    iteration(nt + 1, False, False, True)


def _natten(proj, bias_tab, cols, batch, seq):
    npairs = NA_HEADS // 2
    rows = seq // GRID_W
    nu = NA_UNROLL
    assert rows >= NA_ROWS and rows % nu == 0 and rows // nu >= 2
    kq, kk, kv = cols.qa // LANES, cols.ka // LANES, cols.va // LANES
    blk = (seq, LANES)
    return pl.pallas_call(
        functools.partial(_natten_kernel, rows=rows),
        grid=(batch, npairs),
        in_specs=[
            pl.BlockSpec(blk, lambda b, p: (b, kq + p)),
            pl.BlockSpec(blk, lambda b, p: (b, kk + p)),
            pl.BlockSpec(blk, lambda b, p: (b, kv + p)),
            pl.BlockSpec((None, NA_ROWS, 2 * GRID_W, NA_ROWS * GRID_W), lambda b, p: (p, 0, 0, 0)),
        ],
        out_specs=pl.BlockSpec(blk, lambda b, p: (b, p)),
        out_shape=jax.ShapeDtypeStruct((batch * seq, NA_WIDTH), BF16),
        scratch_shapes=[
            pltpu.VMEM((nu, 2 * GRID_W, NA_ROWS * GRID_W), F32),
            pltpu.VMEM((nu, 2 * GRID_W, NA_ROWS * GRID_W), BF16),
            pltpu.VMEM((nu, 2 * GRID_W, LANES), F32),
        ],
        compiler_params=_cparams(("parallel", "parallel")),
        name="natten",
    )(proj, proj, proj, bias_tab)


def _natten_bias_table(rpb):
    cols = jnp.arange(GRID_W)
    col_start = jnp.clip(cols - NA_COLS // 2, 0, GRID_W - NA_COLS)
    kc = jnp.arange(GRID_W)
    inwin = (kc[None, :] >= col_start[:, None]) & (kc[None, :] < col_start[:, None] + NA_COLS)
    dc = kc[None, :] - cols[:, None] + (NA_COLS - 1)
    onehot = (dc[:, :, None] == jnp.arange(2 * NA_COLS - 1)[None, None, :]).astype(F32)
    exp_c = jnp.einsum("ckd,hrd->hrck", onehot, rpb.astype(F32), precision=lax.Precision.HIGHEST)
    tab = jnp.stack([exp_c[:, v:v + NA_ROWS] for v in range(NA_ROWS)], axis=1)
    tab = jnp.where(inwin[None, None, None], tab, NEG_INF)
    tab = tab.transpose(0, 1, 3, 2, 4).reshape(NA_HEADS, NA_ROWS, GRID_W, NA_ROWS * GRID_W)
    tab = tab.reshape(NA_HEADS // 2, 2, NA_ROWS, GRID_W, NA_ROWS * GRID_W).transpose(0, 2, 1, 3, 4)
    return tab.reshape(NA_HEADS // 2, NA_ROWS, 2 * GRID_W, NA_ROWS * GRID_W)


def _swa_kernel(sink_ref, q_ref, k_ref, v_ref, mask_ref, o_ref, s_ref, p_ref, l_ref, *, nb):
    kp = pl.program_id(1)
    w = WINDOW
    grp = WIN_Q_HEADS // WIN_KV_HEADS
    lane = lax.broadcasted_iota(jnp.int32, (w, LANES), 1)
    first = lane < HEAD_DIM
    rowhalf = lax.broadcasted_iota(jnp.int32, (2 * w, 1), 0) < w

    def window(n):
        start = jnp.clip(n - 1, 0, nb - 3)
        return pl.multiple_of(start * w, w), n - start

    def stage_a(n, c):
        k0, variant = window(n)
        q2 = q_ref[pl.ds(pl.multiple_of(n * w, w), w), c * LANES:(c + 1) * LANES]
        zero = jnp.zeros_like(q2)
        qs = jnp.concatenate([jnp.where(first, q2, zero), jnp.where(first, zero, q2)], axis=0)
        s_ref[c] = _dot_nt(qs, k_ref[pl.ds(k0, 3 * w), :]) + mask_ref[variant]

    def stage_b(c):
        s = s_ref[c]
        sink = jnp.where(rowhalf, sink_ref[(2 * kp) * grp + c], sink_ref[(2 * kp + 1) * grp + c])
        m = jnp.maximum(jnp.max(s, axis=-1, keepdims=True), sink)
        p = jnp.exp(s - m)
        denom = jnp.sum(p, axis=-1, keepdims=True) + jnp.exp(sink - m)
        l_ref[c] = jnp.broadcast_to(1.0 / denom, l_ref.shape[1:])
        p_ref[c] = p.astype(BF16)

    def stage_c(n, c):
        k0, _ = window(n)
        o = _dot(p_ref[c], v_ref[pl.ds(k0, 3 * w), :]) * l_ref[c]
        o_ref[pl.ds(pl.multiple_of(n * w, w), w), c * LANES:(c + 1) * LANES] = (
            jnp.where(first, o[:w], o[w:]).astype(BF16))

    def iteration(n, do_a, do_b, do_c):
        for c in range(grp):
            if do_c:
                stage_c(n - 2, c)
        for c in range(grp):
            if do_b:
                stage_b(c)
        for c in range(grp):
            if do_a:
                stage_a(n, c)

    iteration(0, True, False, False)
    iteration(1, True, True, False)

    def body(n, carry):
        iteration(n, True, True, True)
        return carry

    lax.fori_loop(2, nb, body, 0)
    iteration(nb, False, True, True)
    iteration(nb + 1, False, False, True)


def _swa_mask_table():
    w = WINDOW
    qi = jnp.arange(2 * w) % w
    kj = jnp.arange(3 * w)
    off = kj[None, None, :] - w * jnp.arange(3)[:, None, None] - qi[None, :, None]
    return jnp.where(jnp.abs(off) <= w, 0.0, NEG_INF).astype(F32)


def _swa(proj, sinks, cols, batch, seq):
    w = WINDOW
    nb = seq // w
    grp = WIN_Q_HEADS // WIN_KV_HEADS
    qw = grp * LANES
    assert cols.qb % qw == 0 and nb >= 3
    cq, ck, cv = cols.qb // qw, cols.kb // LANES, cols.vb // LANES
    return pl.pallas_call(
        functools.partial(_swa_kernel, nb=nb),
        grid=(batch, WIN_KV_HEADS // 2),
        in_specs=[
            pl.BlockSpec(memory_space=pltpu.SMEM),
            pl.BlockSpec((seq, qw), lambda b, kp: (b, cq + kp)),
            pl.BlockSpec((seq, LANES), lambda b, kp: (b, ck + kp)),
            pl.BlockSpec((seq, LANES), lambda b, kp: (b, cv + kp)),
            pl.BlockSpec((3, 2 * w, 3 * w), lambda b, kp: (0, 0, 0)),
        ],
        out_specs=pl.BlockSpec((seq, qw), lambda b, kp: (b, kp)),
        out_shape=jax.ShapeDtypeStruct((batch * seq, WIN_Q_WIDTH), BF16),
        scratch_shapes=[
            pltpu.VMEM((grp, 2 * w, 3 * w), F32),
            pltpu.VMEM((grp, 2 * w, 3 * w), BF16),
            pltpu.VMEM((grp, 2 * w, LANES), F32),
        ],
        compiler_params=_cparams(("parallel", "parallel")),
        name="swa",
    )(sinks, proj, proj, proj, _swa_mask_table())


def _swa_head_order(w, axis):
    grp = WIN_Q_HEADS // WIN_KV_HEADS
    shape = w.shape
    split = shape[:axis] + (WIN_KV_HEADS // 2, 2, grp, HEAD_DIM) + shape[axis + 1:]
    return jnp.swapaxes(w.reshape(split), axis + 1, axis + 2).reshape(shape)


def _merge_kernel(ya_ref, yb_ref, ga_ref, gb_ref, x_ref, wna_ref, wwin_ref, wout_ref, o_ref):
    a = _dot(ya_ref[...], wna_ref[...])
    b = _dot(yb_ref[...], wwin_ref[...])
    merged = jax.nn.sigmoid(ga_ref[...].astype(F32)) * a + jax.nn.sigmoid(gb_ref[...].astype(F32)) * b
    o_ref[...] = x_ref[...] + _dot(merged.astype(BF16), wout_ref[...])


def _resident(shape):
    return pl.BlockSpec(shape, lambda i: (0,) * len(shape), pipeline_mode=pl.Buffered(1))


def _merge(ya, yb, proj, x2d, wna, wwin, wout):
    t, d = x2d.shape
    tm = MERGE_TM
    return pl.pallas_call(
        _merge_kernel,
        grid=(t // tm,),
        in_specs=[
            pl.BlockSpec((tm, NA_WIDTH), lambda i: (i, 0)),
            pl.BlockSpec((tm, WIN_Q_WIDTH), lambda i: (i, 0)),
            pl.BlockSpec((tm, d), lambda i: (i, 0)),
            pl.BlockSpec((tm, d), lambda i: (i, 1)),
            pl.BlockSpec((tm, d), lambda i: (i, 0)),
            _resident(wna.shape), _resident(wwin.shape), _resident(wout.shape),
        ],
        out_specs=pl.BlockSpec((tm, d), lambda i: (i, 0)),
        out_shape=jax.ShapeDtypeStruct((t, d), F32),
        compiler_params=_cparams(("parallel",)),
        name="merge",
    )(ya, yb, proj, proj, x2d, wna, wwin, wout)


def _memkv_kernel(m_ref, g_ref, w_ref, o_ref):
    o_ref[...] = _dot(_rms(m_ref[...], g_ref[...]).astype(BF16), w_ref[...]).astype(BF16)


def _memkv(mem2d, g, w_bf, mem_len):
    t, d = mem2d.shape
    n = w_bf.shape[1]
    return pl.pallas_call(
        _memkv_kernel,
        grid=(t // mem_len,),
        in_specs=[
            pl.BlockSpec((mem_len, d), lambda i: (i, 0)),
            pl.BlockSpec((1, d), lambda i: (0, 0)),
            pl.BlockSpec((d, n), lambda i: (0, 0)),
        ],
        out_specs=pl.BlockSpec((mem_len, n), lambda i: (i, 0)),
        out_shape=jax.ShapeDtypeStruct((t, n), BF16),
        compiler_params=_cparams(("parallel",)),
        name="memkv",
    )(mem2d, g, w_bf)


def _cross_kernel(x_ref, gc_ref, wcq_ref, k_ref, v_ref, wco_ref, gm_ref, wrh_ref, wrl_ref, br_ref,
                  x2_ref, h_ref, lg_ref):
    x = x_ref[...]
    q = _dot(_rms(x, gc_ref[...]).astype(BF16), wcq_ref[...]).astype(BF16)
    scale = CROSS_HEAD_DIM ** -0.5
    outs = []
    for h in range(CROSS_HEADS):
        sl = slice(h * CROSS_HEAD_DIM, (h + 1) * CROSS_HEAD_DIM)
        s = _dot_nt(q[:, sl], k_ref[:, sl]) * scale
        m = jnp.max(s, axis=-1, keepdims=True)
        p = jnp.exp(s - m)
        l = jnp.sum(p, axis=-1, keepdims=True)
        outs.append((_dot(p.astype(BF16), v_ref[:, sl]) * (1.0 / l)).astype(BF16))
    o = jnp.concatenate(outs, axis=-1)
    x2 = x + _dot(o, wco_ref[...])
    x2_ref[...] = x2
    h = _rms(x2, gm_ref[...])
    h_hi = h.astype(BF16)
    h_lo = (h - h_hi.astype(F32)).astype(BF16)
    _store_token_tiles(h_ref, _pack_halves(h_hi.astype(F32)))
    lg_ref[...] = (_dot(h_hi, wrh_ref[...]) + _dot(h_hi, wrl_ref[...]) + _dot(h_lo, wrh_ref[...])
                   + br_ref[...])


def _cross(x1, gc, wcq, memkv, wco, gm, wr_hi, wr_lo, br, seq, mem_len):
    t, d = x1.shape
    tm = CROSS_TM
    ne = wr_hi.shape[1]
    per_b = seq // tm
    row = lambda i: (i, 0)
    return pl.pallas_call(
        _cross_kernel,
        grid=(t // tm,),
        in_specs=[
            pl.BlockSpec((tm, d), row),
            _resident((1, d)),
            _resident(wcq.shape),
            pl.BlockSpec((mem_len, CROSS_WIDTH), lambda i: (i // per_b, 0)),
            pl.BlockSpec((mem_len, CROSS_WIDTH), lambda i: (i // per_b, 1)),
            _resident(wco.shape),
            _resident((1, d)),
            _resident(wr_hi.shape), _resident(wr_lo.shape),
            _resident((1, ne)),
        ],
        out_specs=[
            pl.BlockSpec((tm, d), row),
            pl.BlockSpec((tm * SUBLANES, LANES), row),
            pl.BlockSpec((tm, ne), row),
        ],
        out_shape=[
            jax.ShapeDtypeStruct((t, d), F32),
            jax.ShapeDtypeStruct((t * SUBLANES, LANES), U32),
            jax.ShapeDtypeStruct((t, ne), F32),
        ],
        compiler_params=_cparams(("parallel",)),
        name="cross",
    )(x1, gc, wcq, memkv, memkv, wco, gm, wr_hi, wr_lo, br)


def _moe_kernel(te_ref, tv_ref, tc_ref, tok_ref, h_hbm, wg_ref, wu_ref, wd_ref, bg_ref, bu_ref, bd_ref, y_hbm,
                raw_ref, xb_ref, a_ref, ys_ref, cnt_ref, gsem, osem, *, nf, n_tiles, n_assign):
    i = pl.program_id(0)
    s = pl.program_id(1)
    tsb = MOE_TSB
    rpt = SUBLANES
    tm = xb_ref.shape[0]
    tf = wg_ref.shape[1]
    d = xb_ref.shape[1]
    half = d // 2
    n_blk = tm // tsb
    valid = tv_ref[i]
    n_sb = (valid + tsb - 1) // tsb
    nxt = jnp.minimum(i + 1, n_tiles - 1)
    next_active = (i + 1 < n_tiles) & (tv_ref[nxt] > 0)
    next_c0 = tc_ref[nxt]

    def row_copy(c0, k):
        tok = tok_ref[jnp.minimum(c0 + k, n_assign - 1)]
        src = h_hbm.at[pl.ds(pl.multiple_of(tok * rpt, rpt), rpt)]
        dst = raw_ref.at[pl.ds(pl.multiple_of(k * rpt, rpt), rpt)]
        return pltpu.make_async_copy(src, dst, gsem)

    def issue_rows(c0, k0, count):
        for u in range(count):
            row_copy(c0, k0 + u).start(priority=u % 2)

    def issue_loop(c0, k0):
        def group(gi, c):
            issue_rows(c0, k0 + gi * GATHER_UNROLL, GATHER_UNROLL)
            return c

        lax.fori_loop(0, lax.shift_right_logical(tm - k0, GATHER_UNROLL.bit_length() - 1), group, 0)

    def block_rows(ref, sb):
        return ref.at[pl.ds(pl.multiple_of(sb * (tsb * rpt), tsb * rpt), tsb * rpt)]

    def out_copy(t):
        dst = y_hbm.at[pl.ds(pl.multiple_of(t * (tm * rpt), tm * rpt), tm * rpt)]
        return pltpu.make_async_copy(ys_ref, dst, osem)

    @pl.when((i == 0) & (s == 0))
    def _():
        ys_ref[...] = jnp.zeros(ys_ref.shape, U32)

        @pl.when(valid > 0)
        def _():
            issue_loop(tc_ref[0], 0)

    @pl.when(s == 0)
    def _():
        cnt_ref[0] = 0

        @pl.when(valid > 0)
        def _():
            for sb in range(n_blk):
                pltpu.make_async_copy(block_rows(h_hbm, 0), block_rows(raw_ref, sb), gsem).wait()

        def unpack(sb, c):
            r0 = pl.multiple_of(sb * tsb, tsb)
            for sl in range(rpt):
                p = raw_ref[pl.ds(r0 * rpt + sl, tsb, stride=rpt), :]
                xb_ref[pl.ds(r0, tsb), sl * LANES:(sl + 1) * LANES] = _unpack_lo(p).astype(BF16)
                xb_ref[pl.ds(r0, tsb), half + sl * LANES:half + (sl + 1) * LANES] = _unpack_hi(p).astype(BF16)
            return c

        lax.fori_loop(0, n_sb, unpack, 0)

    def for_row_blocks(fn):
        big = MOE_BIG_BLOCK * tsb

        def big_block(b, c):
            r0 = pl.multiple_of(b * big, big)
            k0 = cnt_ref[0]
            issue = next_active & (k0 + MOE_BLOCK_COPIES <= tm)

            @pl.when(issue)
            def _():
                fn(r0, big, k0)
                cnt_ref[0] = k0 + MOE_BLOCK_COPIES

            @pl.when(jnp.logical_not(issue))
            def _():
                fn(r0, big, None)

            return c

        n_big = n_sb // MOE_BIG_BLOCK
        lax.fori_loop(0, n_big, big_block, 0)
        rem = n_sb - n_big * MOE_BIG_BLOCK

        @pl.when(rem >= 4)
        def _():
            fn(pl.multiple_of(n_big * big, 4 * tsb), 4 * tsb, None)

        def single(b, c):
            fn(pl.multiple_of(b * tsb, tsb), tsb, None)
            return c

        lax.fori_loop(n_sb - (rem & 3), n_sb, single, 0)

    @pl.when((s < nf) & (valid > 0))
    def _():
        def gate_up(r0, rows, k0):
            wg = wg_ref[...].astype(BF16)
            wu = wu_ref[...].astype(BF16)
            n_split = MOE_GATE_SPLIT
            piece = d // n_split
            per = MOE_BLOCK_COPIES // (2 * n_split)
            g = bg_ref[...]
            u = bu_ref[...]
            for j in range(n_split):
                xs = xb_ref[pl.ds(r0, rows), j * piece:(j + 1) * piece]
                if k0 is not None:
                    issue_rows(next_c0, k0 + (2 * j) * per, per)
                g = g + _dot(xs, wg[j * piece:(j + 1) * piece, :])
                if k0 is not None:
                    issue_rows(next_c0, k0 + (2 * j + 1) * per, per)
                u = u + _dot(xs, wu[j * piece:(j + 1) * piece, :])
            g = jnp.minimum(g, SWIGLU_LIMIT)
            u = jnp.clip(u, -SWIGLU_LIMIT, SWIGLU_LIMIT)
            a_ref[s, pl.ds(r0, rows), :] = ((u + 1.0) * (g * jax.nn.sigmoid(SWIGLU_ALPHA * g))).astype(BF16)

        for_row_blocks(gate_up)

    @pl.when((s == nf) & (i > 0))
    def _():
        out_copy(i - 1).wait()

    @pl.when((s >= nf) & (valid > 0))
    def _():
        n = s - nf

        def down(r0, rows, k0):
            wd = wd_ref[...].astype(BF16)
            per = MOE_BLOCK_COPIES // nf
            acc = bd_ref[...]
            for f in range(nf):
                if k0 is not None:
                    issue_rows(next_c0, k0 + f * per, per)
                acc = acc + _dot(a_ref[f, pl.ds(r0, rows), :], wd[f * tf:(f + 1) * tf, :])
            packed = _pack_halves(acc.astype(BF16).astype(F32))
            for j in range(tf // 2 // LANES):
                ys_ref[pl.ds(r0 * rpt + n * (tf // 2 // LANES) + j, rows, stride=rpt), :] = (
                    packed[:, j * LANES:(j + 1) * LANES])

        for_row_blocks(down)

    @pl.when(s == 2 * nf - 1)
    def _():
        @pl.when(next_active)
        def _():
            issue_loop(next_c0, cnt_ref[0])

        out_copy(i).start()

        @pl.when(i == n_tiles - 1)
        def _():
            out_copy(i).wait()


def _moe_ffn(tile_e, tile_valid, tile_c0, toks, h_tiles, w_gate, b_gate, w_up, b_up, w_down, b_down):
    e, d, dff = w_gate.shape
    tm, tf = MOE_TM, MOE_TF
    rpt = SUBLANES
    assert d == dff
    assert d == 2 * rpt * LANES, "a packed token row must be exactly one (8, 128) tile"
    assert tm % MOE_TSB == 0 and dff % tf == 0 and (tf // 2) % LANES == 0
    nf = dff // tf
    assert nf >= 2
    assert MOE_BLOCK_COPIES % (2 * MOE_GATE_SPLIT) == 0 and MOE_BLOCK_COPIES % nf == 0
    assert tm % MOE_BLOCK_COPIES == 0 and MOE_BLOCK_COPIES % GATHER_UNROLL == 0 and tm % GATHER_UNROLL == 0
    assert d % MOE_GATE_SPLIT == 0 and 1 <= MOE_BIG_BLOCK <= 8
    n_tiles = tile_e.shape[0]
    up = lambda i, s, te, tv, tc, tk: (te[i], 0, jnp.minimum(s, nf - 1))
    dn = lambda i, s, te, tv, tc, tk: (te[i], 0, jnp.maximum(s - nf, 0))
    grid_spec = pltpu.PrefetchScalarGridSpec(
        num_scalar_prefetch=4,
        grid=(n_tiles, 2 * nf),
        in_specs=[
            pl.BlockSpec(memory_space=pl.ANY),
            pl.BlockSpec((None, d, tf), up),
            pl.BlockSpec((None, d, tf), up),
            pl.BlockSpec((None, dff, tf), dn),
            pl.BlockSpec((None, 1, tf), up),
            pl.BlockSpec((None, 1, tf), up),
            pl.BlockSpec((None, 1, tf), dn),
        ],
        out_specs=pl.BlockSpec(memory_space=pl.ANY),
        scratch_shapes=[
            pltpu.VMEM((tm * rpt, LANES), U32),
            pltpu.VMEM((tm, d), BF16),
            pltpu.VMEM((nf, tm, tf), BF16),
            pltpu.VMEM((tm * rpt, LANES), U32),
            pltpu.SMEM((1,), jnp.int32),
            pltpu.SemaphoreType.DMA(()),
            pltpu.SemaphoreType.DMA(()),
        ],
    )
    return pl.pallas_call(
        functools.partial(_moe_kernel, nf=nf, n_tiles=n_tiles, n_assign=toks.shape[0]),
        grid_spec=grid_spec,
        out_shape=jax.ShapeDtypeStruct((n_tiles * tm * rpt, LANES), U32),
        compiler_params=_cparams(("arbitrary", "arbitrary")),
        name="moe_ffn",
    )(tile_e, tile_valid, tile_c0, toks, h_tiles, w_gate, w_up, w_down,
      b_gate.reshape(e, 1, dff), b_up.reshape(e, 1, dff), b_down.reshape(e, 1, d))


def _combine_kernel(pos_ref, x_ref, y_hbm, w_ref, g_ref, o_ref, buf_ref, sem, *, final, n_steps, tf):
    i = pl.program_id(0)
    tm, d = x_ref.shape
    rpt = SUBLANES
    slot = i % 2

    def issue(step, sl_):
        def body(pair, c):
            for u in range(2):
                tt = pair * 2 + u
                for k in range(TOP_K):
                    p = pos_ref[(step * tm + tt) * TOP_K + k]
                    src = y_hbm.at[pl.ds(pl.multiple_of(p * rpt, rpt), rpt)]
                    dst = buf_ref.at[sl_, pl.ds(pl.multiple_of((k * tm + tt) * rpt, rpt), rpt)]
                    pltpu.make_async_copy(src, dst, sem.at[sl_]).start(priority=k % 2)
            return c

        lax.fori_loop(0, tm // 2, body, 0)

    @pl.when(i == 0)
    def _():
        issue(0, 0)

    @pl.when(i + 1 < n_steps)
    def _():
        issue(i + 1, 1 - slot)

    pltpu.make_async_copy(y_hbm.at[pl.ds(0, TOP_K * tm * rpt)], buf_ref.at[slot], sem.at[slot]).wait()

    w = w_ref[...]
    wk = [jnp.broadcast_to(w[:, k:k + 1], (tm, LANES)) for k in range(TOP_K)]
    per_chunk = tf // 2 // LANES
    slabs = [None] * (d // LANES)
    for sl in range(rpt):
        n, j = sl // per_chunk, sl % per_chunk
        c_lo = (n * tf) // LANES + j
        c_hi = c_lo + per_chunk
        lo = x_ref[:, c_lo * LANES:(c_lo + 1) * LANES]
        hi = x_ref[:, c_hi * LANES:(c_hi + 1) * LANES]
        for k in range(TOP_K):
            p = buf_ref[slot, pl.ds(k * tm * rpt + sl, tm, stride=rpt), :]
            lo = lo + wk[k] * _unpack_lo(p)
            hi = hi + wk[k] * _unpack_hi(p)
        slabs[c_lo], slabs[c_hi] = lo, hi
    acc = jnp.concatenate(slabs, axis=1)
    o_ref[...] = _rms(acc, g_ref[...]) if final else acc


def _combine(pos, x2, y_tiles, top_w, g, final):
    t, d = x2.shape
    tm = COMBINE_TM
    rpt = SUBLANES
    n_steps = t // tm
    grid_spec = pltpu.PrefetchScalarGridSpec(
        num_scalar_prefetch=1,
        grid=(n_steps,),
        in_specs=[
            pl.BlockSpec((tm, d), lambda i, pos: (i, 0)),
            pl.BlockSpec(memory_space=pl.ANY),
            pl.BlockSpec((tm, TOP_K), lambda i, pos: (i, 0)),
            pl.BlockSpec((1, d), lambda i, pos: (0, 0)),
        ],
        out_specs=pl.BlockSpec((tm, d), lambda i, pos: (i, 0)),
        scratch_shapes=[
            pltpu.VMEM((2, TOP_K * tm * rpt, LANES), U32),
            pltpu.SemaphoreType.DMA((2,)),
        ],
    )
    return pl.pallas_call(
        functools.partial(_combine_kernel, final=final, n_steps=n_steps, tf=MOE_TF),
        grid_spec=grid_spec,
        out_shape=jax.ShapeDtypeStruct((t, d), F32),
        compiler_params=_cparams(("arbitrary",)),
        name="combine",
    )(pos, x2, y_tiles, top_w, g)


def _rope_tables(seq):
    half = ROT_DIM // 2
    inv = ROPE_THETA ** (-(jnp.arange(half, dtype=F32) * 2.0 / ROT_DIM))
    ang = jnp.arange(seq, dtype=F32)[:, None] * inv[None, :]
    cos, sin = jnp.cos(ang), jnp.sin(ang)
    ones = jnp.ones((seq, HEAD_DIM - ROT_DIM), F32)
    zeros = jnp.zeros((seq, HEAD_DIM - ROT_DIM), F32)
    zh = jnp.zeros((seq, half), F32)
    c = jnp.concatenate([cos, cos, ones], axis=1)
    sa = jnp.concatenate([-sin, zh, zeros], axis=1)
    sb = jnp.concatenate([zh, sin, zeros], axis=1)
    rep = LANES // HEAD_DIM
    return jnp.tile(c, (1, rep)), jnp.tile(sa, (1, rep)), jnp.tile(sb, (1, rep))


def _route(logits, tm):
    t = logits.shape[0]
    i32 = jnp.int32
    top_val, top_idx = lax.top_k(logits, TOP_K)
    top_w = jax.nn.softmax(top_val, axis=-1)
    n_assign = t * TOP_K
    flat_e = top_idx.reshape(n_assign).astype(i32)
    order = jnp.argsort(flat_e).astype(i32)
    rank = jnp.argsort(order).astype(i32)
    onehot = flat_e[:, None] == jnp.arange(N_EXPERTS, dtype=i32)[None, :]
    counts = jnp.sum(onehot, axis=0, dtype=i32)
    start = jnp.cumsum(counts) - counts
    tiles_e = (counts + tm - 1) // tm
    tile_end = jnp.cumsum(tiles_e)
    tile_first = tile_end - tiles_e
    n_tiles = n_assign // tm + N_EXPERTS
    j = jnp.arange(n_tiles, dtype=i32)
    tile_e = jnp.minimum(jnp.searchsorted(tile_end, j, side="right"), N_EXPERTS - 1).astype(i32)
    local = j - tile_first[tile_e]
    tile_valid = jnp.clip(counts[tile_e] - local * tm, 0, tm).astype(i32)
    tile_c0 = jnp.where(tile_valid > 0, start[tile_e] + local * tm, 0).astype(i32)
    last_e = jnp.max(jnp.where(tile_valid > 0, tile_e, 0))
    tile_e = jnp.where(tile_valid > 0, tile_e, last_e).astype(i32)
    row_base = jnp.sum(jnp.where(onehot, (tile_first * tm - start)[None, :], 0), axis=1, dtype=i32)
    pos = rank + row_base
    return top_w, order // TOP_K, pos, tile_e, tile_valid, tile_c0


def kernel(x, mem, g_mix, w_in, rpb_na, sinks, w_na_o, w_win_o, w_out, g_cross, g_mem, w_cq, w_ckv, w_co,
           g_moe, w_router, b_router, w_gate, b_gate, w_up, b_up, w_down, b_down, g_final):
    b, s, d = x.shape
    mem_len = mem.shape[1]
    t = b * s
    depth = w_in.shape[0]
    xc = x.reshape(t, d)
    mem2d = mem.reshape(b * mem_len, d)
    rope_c, rope_sa, rope_sb = _rope_tables(s)

    cols = _cols(d)
    n_attn = cols.end - cols.qa
    scale = HEAD_DIM ** -0.5
    o_qb, o_kb = 3 * NA_WIDTH, 3 * NA_WIDTH + WIN_Q_WIDTH

    for l in range(depth):
        w = w_in[l]
        w_in_l = jnp.concatenate([w[:, n_attn:], w[:, :NA_WIDTH] * scale, w[:, NA_WIDTH:o_qb],
                                  _swa_head_order(w[:, o_qb:o_kb] * scale, axis=1), w[:, o_kb:n_attn]],
                                 axis=1).astype(BF16)
        proj = _in_proj(xc, g_mix[l].reshape(1, d), w_in_l, rope_c, rope_sa, rope_sb, s)
        ya = _natten(proj, _natten_bias_table(rpb_na[l]), cols, b, s)
        yb = _swa(proj, sinks[l].astype(F32), cols, b, s)
        x1 = _merge(ya, yb, proj, xc, w_na_o[l].astype(BF16), _swa_head_order(w_win_o[l], axis=0).astype(BF16),
                    w_out[l].astype(BF16))
        memkv = _memkv(mem2d, g_mem[l].reshape(1, d), w_ckv[l].astype(BF16), mem_len)
        wr_hi = w_router[l].astype(BF16)
        wr_lo = (w_router[l] - wr_hi.astype(F32)).astype(BF16)
        x2, h3, logits = _cross(x1, g_cross[l].reshape(1, d), w_cq[l].astype(BF16), memkv, w_co[l].astype(BF16),
                                g_moe[l].reshape(1, d), wr_hi, wr_lo, b_router[l].reshape(1, -1), s, mem_len)
        top_w, toks, pos, tile_e, tile_valid, tile_c0 = _route(logits, MOE_TM)
        ys = _moe_ffn(tile_e, tile_valid, tile_c0, toks, h3, w_gate[l], b_gate[l], w_up[l], b_up[l],
                      w_down[l], b_down[l])
        xc = _combine(pos, x2, ys, top_w, g_final.reshape(1, d), final=(l + 1 == depth))
    return xc.reshape(b, s, d)
```

```python
import functools
from typing import NamedTuple

import jax
import jax.numpy as jnp
from jax import lax
from jax.experimental import pallas as pl
from jax.experimental.pallas import tpu as pltpu

F32 = jnp.float32
BF16 = jnp.bfloat16
U32 = jnp.uint32

GRID_W = 64
HEAD_DIM = 64
NA_HEADS = 16
NA_ROWS = 8
NA_COLS = 16
WIN_Q_HEADS = 16
WIN_KV_HEADS = 4
WINDOW = 128
ROT_DIM = HEAD_DIM // 4
ROPE_THETA = 500000.0
CROSS_HEADS = 4
CROSS_HEAD_DIM = 128
N_EXPERTS = 32
TOP_K = 4
SWIGLU_LIMIT = 7.0
SWIGLU_ALPHA = 1.702
NORM_EPS = 1e-5
NEG_INF = -1e30

LANES = 128
SUBLANES = 8
VMEM_LIMIT = 56 * 1024 * 1024

NA_WIDTH = NA_HEADS * HEAD_DIM
WIN_Q_WIDTH = WIN_Q_HEADS * HEAD_DIM
WIN_KV_WIDTH = WIN_KV_HEADS * HEAD_DIM
CROSS_WIDTH = CROSS_HEADS * CROSS_HEAD_DIM


class _Cols(NamedTuple):
    ga: int
    gb: int
    qa: int
    ka: int
    va: int
    qb: int
    kb: int
    vb: int
    end: int


def _cols(d):
    widths = [d, d, NA_WIDTH, NA_WIDTH, NA_WIDTH, WIN_Q_WIDTH, WIN_KV_WIDTH, WIN_KV_WIDTH]
    offs = [0]
    for w in widths:
        offs.append(offs[-1] + w)
    return _Cols(*offs)


PROJ_TM = 1024
PROJ_TN = 512
NA_UNROLL = 2
MERGE_TM = 256
CROSS_TM = 256
MOE_TM = 2304
MOE_TSB = 256
MOE_TF = 256
GATHER_UNROLL = 8
MOE_BIG_BLOCK = 8
MOE_BLOCK_COPIES = 192
MOE_GATE_SPLIT = 4
COMBINE_TM = 256


def _cparams(sem):
    return pltpu.CompilerParams(dimension_semantics=sem, vmem_limit_bytes=VMEM_LIMIT)


def _rms(x, g):
    ms = jnp.mean(x * x, axis=-1, keepdims=True)
    return x * lax.rsqrt(ms + NORM_EPS) * g


def _dot(a, b):
    return jnp.dot(a, b, preferred_element_type=F32)


def _dot_nt(a, b):
    return lax.dot_general(a, b, (((1,), (1,)), ((), ())), preferred_element_type=F32)


def _pack_halves(v):
    half = v.shape[1] // 2
    lo = lax.shift_right_logical(pltpu.bitcast(v[:, :half], U32), jnp.uint32(16))
    hi = pltpu.bitcast(v[:, half:], U32) & jnp.uint32(0xFFFF0000)
    return hi | lo


def _unpack_lo(p):
    return pltpu.bitcast(lax.shift_left(p, jnp.uint32(16)), F32)


def _unpack_hi(p):
    return pltpu.bitcast(p & jnp.uint32(0xFFFF0000), F32)


def _store_token_tiles(ref, packed):
    m, w = packed.shape
    n_sl = w // LANES
    for s in range(n_sl):
        ref[pl.ds(s, m, stride=n_sl), :] = packed[:, s * LANES:(s + 1) * LANES]


def _rope_chunk(x, c, sa, sb):
    return x * c + pltpu.roll(x, LANES - ROT_DIM // 2, 1) * sa + pltpu.roll(x, ROT_DIM // 2, 1) * sb


def _inproj_kernel(x_ref, g_ref, w_ref, c_ref, sa_ref, sb_ref, o_ref, h_ref, *, cols):
    j = pl.program_id(1)
    tn = o_ref.shape[1]
    nchunk = tn // LANES

    @pl.when(j == 0)
    def _():
        h_ref[...] = _rms(x_ref[...], g_ref[...]).astype(BF16)

    acc = _dot(h_ref[...], w_ref[...])
    o_ref[...] = acc.astype(BF16)

    j_qb, j_kb = cols.qb // tn, cols.kb // tn
    kb_chunks = WIN_KV_WIDTH // LANES

    @pl.when(j >= j_qb)
    def _():
        c, sa, sb = c_ref[...], sa_ref[...], sb_ref[...]

        def rotate(k):
            sl = slice(k * LANES, (k + 1) * LANES)
            o_ref[:, sl] = _rope_chunk(acc[:, sl], c, sa, sb).astype(BF16)

        for k in range(nchunk):
            if k < kb_chunks:
                rotate(k)
            else:
                pl.when(j < j_kb)(functools.partial(rotate, k))


def _in_proj(x2d, g, w_bf, rope_c, rope_sa, rope_sb, seq):
    t, d = x2d.shape
    n = w_bf.shape[1]
    cols = _cols(d)
    tm, tn = PROJ_TM, PROJ_TN
    assert t % tm == 0 and n % tn == 0 and seq % tm == 0 and n == cols.end
    assert all(c % tn == 0 for c in (cols.qa, cols.ka, cols.qb, cols.kb)) and cols.end == cols.kb + tn
    sblocks = seq // tm
    rope_spec = pl.BlockSpec((tm, LANES), lambda i, j: (i % sblocks, 0))
    return pl.pallas_call(
        functools.partial(_inproj_kernel, cols=cols),
        grid=(t // tm, n // tn),
        in_specs=[
            pl.BlockSpec((tm, d), lambda i, j: (i, 0)),
            pl.BlockSpec((1, d), lambda i, j: (0, 0)),
            pl.BlockSpec((d, tn), lambda i, j: (0, j)),
            rope_spec, rope_spec, rope_spec,
        ],
        out_specs=pl.BlockSpec((tm, tn), lambda i, j: (i, j)),
        out_shape=jax.ShapeDtypeStruct((t, n), BF16),
        scratch_shapes=[pltpu.VMEM((tm, d), BF16)],
        compiler_params=_cparams(("parallel", "arbitrary")),
        name="in_proj",
    )(x2d, g, w_bf, rope_c, rope_sa, rope_sb)


def _natten_kernel(q_ref, k_ref, v_ref, b_ref, o_ref, s_ref, p_ref, l_ref, *, rows):
    kr = NA_ROWS
    nu = NA_UNROLL
    lane = lax.broadcasted_iota(jnp.int32, (GRID_W, LANES), 1)
    first = lane < HEAD_DIM

    def window(r):
        rs = jnp.clip(r - kr // 2, 0, rows - kr)
        return pl.multiple_of(rs * GRID_W, GRID_W), rs - r + (NA_ROWS - 1)

    def stage_a(r, slot):
        k0, dr0 = window(r)
        q2 = q_ref[pl.ds(pl.multiple_of(r * GRID_W, GRID_W), GRID_W), :]
        k2 = k_ref[pl.ds(k0, kr * GRID_W), :]
        zero = jnp.zeros_like(q2)
        qs = jnp.concatenate([jnp.where(first, q2, zero), jnp.where(first, zero, q2)], axis=0)
        s_ref[slot] = _dot_nt(qs, k2) + b_ref[dr0]

    def stage_b(slot):
        s = s_ref[slot]
        m = jnp.max(s, axis=-1, keepdims=True)
        p = jnp.exp(s - m)
        l_ref[slot] = jnp.broadcast_to(1.0 / jnp.sum(p, axis=-1, keepdims=True), l_ref.shape[1:])
        p_ref[slot] = p.astype(BF16)

    def stage_c(r, slot):
        k0, _ = window(r)
        v2 = v_ref[pl.ds(k0, kr * GRID_W), :]
        o = _dot(p_ref[slot], v2) * l_ref[slot]
        o_ref[pl.ds(pl.multiple_of(r * GRID_W, GRID_W), GRID_W), :] = (
            jnp.where(first, o[:GRID_W], o[GRID_W:]).astype(BF16))

    def iteration(t, do_a, do_b, do_c):
        for j in range(nu):
            if do_c:
                stage_c((t - 2) * nu + j, j)
        for j in range(nu):
            if do_b:
                stage_b(j)
        for j in range(nu):
            if do_a:
                stage_a(t * nu + j, j)

    nt = rows // nu
    iteration(0, True, False, False)
    iteration(1, True, True, False)

    def body(t, carry):
        iteration(t, True, True, True)
        return carry

    lax.fori_loop(2, nt, body, 0)
    iteration(nt, False, True, True)
    iteration(nt + 1, False, False, True)


def _natten(proj, bias_tab, cols, batch, seq):
    npairs = NA_HEADS // 2
    rows = seq // GRID_W
    nu = NA_UNROLL
    assert rows >= NA_ROWS and rows % nu == 0 and rows // nu >= 2
    kq, kk, kv = cols.qa // LANES, cols.ka // LANES, cols.va // LANES
    blk = (seq, LANES)
    return pl.pallas_call(
        functools.partial(_natten_kernel, rows=rows),
        grid=(batch, npairs),
        in_specs=[
            pl.BlockSpec(blk, lambda b, p: (b, kq + p)),
            pl.BlockSpec(blk, lambda b, p: (b, kk + p)),
            pl.BlockSpec(blk, lambda b, p: (b, kv + p)),
            pl.BlockSpec((None, NA_ROWS, 2 * GRID_W, NA_ROWS * GRID_W), lambda b, p: (p, 0, 0, 0)),
        ],
        out_specs=pl.BlockSpec(blk, lambda b, p: (b, p)),
        out_shape=jax.ShapeDtypeStruct((batch * seq, NA_WIDTH), BF16),
        scratch_shapes=[
            pltpu.VMEM((nu, 2 * GRID_W, NA_ROWS * GRID_W), F32),
            pltpu.VMEM((nu, 2 * GRID_W, NA_ROWS * GRID_W), BF16),
            pltpu.VMEM((nu, 2 * GRID_W, LANES), F32),
        ],
        compiler_params=_cparams(("parallel", "parallel")),
        name="natten",
    )(proj, proj, proj, bias_tab)


def _natten_bias_table(rpb):
    cols = jnp.arange(GRID_W)
    col_start = jnp.clip(cols - NA_COLS // 2, 0, GRID_W - NA_COLS)
    kc = jnp.arange(GRID_W)
    inwin = (kc[None, :] >= col_start[:, None]) & (kc[None, :] < col_start[:, None] + NA_COLS)
    dc = kc[None, :] - cols[:, None] + (NA_COLS - 1)
    onehot = (dc[:, :, None] == jnp.arange(2 * NA_COLS - 1)[None, None, :]).astype(F32)
    exp_c = jnp.einsum("ckd,hrd->hrck", onehot, rpb.astype(F32), precision=lax.Precision.HIGHEST)
    tab = jnp.stack([exp_c[:, v:v + NA_ROWS] for v in range(NA_ROWS)], axis=1)
    tab = jnp.where(inwin[None, None, None], tab, NEG_INF)
    tab = tab.transpose(0, 1, 3, 2, 4).reshape(NA_HEADS, NA_ROWS, GRID_W, NA_ROWS * GRID_W)
    tab = tab.reshape(NA_HEADS // 2, 2, NA_ROWS, GRID_W, NA_ROWS * GRID_W).transpose(0, 2, 1, 3, 4)
    return tab.reshape(NA_HEADS // 2, NA_ROWS, 2 * GRID_W, NA_ROWS * GRID_W)


def _swa_kernel(sink_ref, q_ref, k_ref, v_ref, mask_ref, o_ref, s_ref, p_ref, l_ref, *, nb):
    kp = pl.program_id(1)
    w = WINDOW
    grp = WIN_Q_HEADS // WIN_KV_HEADS
    lane = lax.broadcasted_iota(jnp.int32, (w, LANES), 1)
    first = lane < HEAD_DIM
    rowhalf = lax.broadcasted_iota(jnp.int32, (2 * w, 1), 0) < w

    def window(n):
        start = jnp.clip(n - 1, 0, nb - 3)
        return pl.multiple_of(start * w, w), n - start

    def stage_a(n, c):
        k0, variant = window(n)
        q2 = q_ref[pl.ds(pl.multiple_of(n * w, w), w), c * LANES:(c + 1) * LANES]
        zero = jnp.zeros_like(q2)
        qs = jnp.concatenate([jnp.where(first, q2, zero), jnp.where(first, zero, q2)], axis=0)
        s_ref[c] = _dot_nt(qs, k_ref[pl.ds(k0, 3 * w), :]) + mask_ref[variant]

    def stage_b(c):
        s = s_ref[c]
        sink = jnp.where(rowhalf, sink_ref[(2 * kp) * grp + c], sink_ref[(2 * kp + 1) * grp + c])
        m = jnp.maximum(jnp.max(s, axis=-1, keepdims=True), sink)
        p = jnp.exp(s - m)
        denom = jnp.sum(p, axis=-1, keepdims=True) + jnp.exp(sink - m)
        l_ref[c] = jnp.broadcast_to(1.0 / denom, l_ref.shape[1:])
        p_ref[c] = p.astype(BF16)

    def stage_c(n, c):
        k0, _ = window(n)
        o = _dot(p_ref[c], v_ref[pl.ds(k0, 3 * w), :]) * l_ref[c]
        o_ref[pl.ds(pl.multiple_of(n * w, w), w), c * LANES:(c + 1) * LANES] = (
            jnp.where(first, o[:w], o[w:]).astype(BF16))

    def iteration(n, do_a, do_b, do_c):
        for c in range(grp):
            if do_c:
                stage_c(n - 2, c)
        for c in range(grp):
            if do_b:
                stage_b(c)
        for c in range(grp):
            if do_a:
                stage_a(n, c)

    iteration(0, True, False, False)
    iteration(1, True, True, False)

    def body(n, carry):
        iteration(n, True, True, True)
        return carry

    lax.fori_loop(2, nb, body, 0)
    iteration(nb, False, True, True)
    iteration(nb + 1, False, False, True)


def _swa_mask_table():
    w = WINDOW
    qi = jnp.arange(2 * w) % w
    kj = jnp.arange(3 * w)
    off = kj[None, None, :] - w * jnp.arange(3)[:, None, None] - qi[None, :, None]
    return jnp.where(jnp.abs(off) <= w, 0.0, NEG_INF).astype(F32)


def _swa(proj, sinks, cols, batch, seq):
    w = WINDOW
    nb = seq // w
    grp = WIN_Q_HEADS // WIN_KV_HEADS
    qw = grp * LANES
    assert cols.qb % qw == 0 and nb >= 3
    cq, ck, cv = cols.qb // qw, cols.kb // LANES, cols.vb // LANES
    return pl.pallas_call(
        functools.partial(_swa_kernel, nb=nb),
        grid=(batch, WIN_KV_HEADS // 2),
        in_specs=[
            pl.BlockSpec(memory_space=pltpu.SMEM),
            pl.BlockSpec((seq, qw), lambda b, kp: (b, cq + kp)),
            pl.BlockSpec((seq, LANES), lambda b, kp: (b, ck + kp)),
            pl.BlockSpec((seq, LANES), lambda b, kp: (b, cv + kp)),
            pl.BlockSpec((3, 2 * w, 3 * w), lambda b, kp: (0, 0, 0)),
        ],
        out_specs=pl.BlockSpec((seq, qw), lambda b, kp: (b, kp)),
        out_shape=jax.ShapeDtypeStruct((batch * seq, WIN_Q_WIDTH), BF16),
        scratch_shapes=[
            pltpu.VMEM((grp, 2 * w, 3 * w), F32),
            pltpu.VMEM((grp, 2 * w, 3 * w), BF16),
            pltpu.VMEM((grp, 2 * w, LANES), F32),
        ],
        compiler_params=_cparams(("parallel", "parallel")),
        name="swa",
    )(sinks, proj, proj, proj, _swa_mask_table())


def _swa_head_order(w, axis):
    grp = WIN_Q_HEADS // WIN_KV_HEADS
    shape = w.shape
    split = shape[:axis] + (WIN_KV_HEADS // 2, 2, grp, HEAD_DIM) + shape[axis + 1:]
    return jnp.swapaxes(w.reshape(split), axis + 1, axis + 2).reshape(shape)


def _merge_kernel(ya_ref, yb_ref, ga_ref, gb_ref, x_ref, wna_ref, wwin_ref, wout_ref, o_ref):
    a = _dot(ya_ref[...], wna_ref[...])
    b = _dot(yb_ref[...], wwin_ref[...])
    merged = jax.nn.sigmoid(ga_ref[...].astype(F32)) * a + jax.nn.sigmoid(gb_ref[...].astype(F32)) * b
    o_ref[...] = x_ref[...] + _dot(merged.astype(BF16), wout_ref[...])


def _resident(shape):
    return pl.BlockSpec(shape, lambda i: (0,) * len(shape), pipeline_mode=pl.Buffered(1))


def _merge(ya, yb, proj, x2d, wna, wwin, wout):
    t, d = x2d.shape
    tm = MERGE_TM
    return pl.pallas_call(
        _merge_kernel,
        grid=(t // tm,),
        in_specs=[
            pl.BlockSpec((tm, NA_WIDTH), lambda i: (i, 0)),
            pl.BlockSpec((tm, WIN_Q_WIDTH), lambda i: (i, 0)),
            pl.BlockSpec((tm, d), lambda i: (i, 0)),
            pl.BlockSpec((tm, d), lambda i: (i, 1)),
            pl.BlockSpec((tm, d), lambda i: (i, 0)),
            _resident(wna.shape), _resident(wwin.shape), _resident(wout.shape),
        ],
        out_specs=pl.BlockSpec((tm, d), lambda i: (i, 0)),
        out_shape=jax.ShapeDtypeStruct((t, d), F32),
        compiler_params=_cparams(("parallel",)),
        name="merge",
    )(ya, yb, proj, proj, x2d, wna, wwin, wout)


def _memkv_kernel(m_ref, g_ref, w_ref, o_ref):
    o_ref[...] = _dot(_rms(m_ref[...], g_ref[...]).astype(BF16), w_ref[...]).astype(BF16)


def _memkv(mem2d, g, w_bf, mem_len):
    t, d = mem2d.shape
    n = w_bf.shape[1]
    return pl.pallas_call(
        _memkv_kernel,
        grid=(t // mem_len,),
        in_specs=[
            pl.BlockSpec((mem_len, d), lambda i: (i, 0)),
            pl.BlockSpec((1, d), lambda i: (0, 0)),
            pl.BlockSpec((d, n), lambda i: (0, 0)),
        ],
        out_specs=pl.BlockSpec((mem_len, n), lambda i: (i, 0)),
        out_shape=jax.ShapeDtypeStruct((t, n), BF16),
        compiler_params=_cparams(("parallel",)),
        name="memkv",
    )(mem2d, g, w_bf)


def _cross_kernel(x_ref, gc_ref, wcq_ref, k_ref, v_ref, wco_ref, gm_ref, wrh_ref, wrl_ref, br_ref,
                  x2_ref, h_ref, lg_ref):
    x = x_ref[...]
    q = _dot(_rms(x, gc_ref[...]).astype(BF16), wcq_ref[...]).astype(BF16)
    scale = CROSS_HEAD_DIM ** -0.5
    outs = []
    for h in range(CROSS_HEADS):
        sl = slice(h * CROSS_HEAD_DIM, (h + 1) * CROSS_HEAD_DIM)
        s = _dot_nt(q[:, sl], k_ref[:, sl]) * scale
        m = jnp.max(s, axis=-1, keepdims=True)
        p = jnp.exp(s - m)
        l = jnp.sum(p, axis=-1, keepdims=True)
        outs.append((_dot(p.astype(BF16), v_ref[:, sl]) * (1.0 / l)).astype(BF16))
    o = jnp.concatenate(outs, axis=-1)
    x2 = x + _dot(o, wco_ref[...])
    x2_ref[...] = x2
    h = _rms(x2, gm_ref[...])
    h_hi = h.astype(BF16)
    h_lo = (h - h_hi.astype(F32)).astype(BF16)
    _store_token_tiles(h_ref, _pack_halves(h_hi.astype(F32)))
    lg_ref[...] = (_dot(h_hi, wrh_ref[...]) + _dot(h_hi, wrl_ref[...]) + _dot(h_lo, wrh_ref[...])
                   + br_ref[...])


def _cross(x1, gc, wcq, memkv, wco, gm, wr_hi, wr_lo, br, seq, mem_len):
    t, d = x1.shape
    tm = CROSS_TM
    ne = wr_hi.shape[1]
    per_b = seq // tm
    row = lambda i: (i, 0)
    return pl.pallas_call(
        _cross_kernel,
        grid=(t // tm,),
        in_specs=[
            pl.BlockSpec((tm, d), row),
            _resident((1, d)),
            _resident(wcq.shape),
            pl.BlockSpec((mem_len, CROSS_WIDTH), lambda i: (i // per_b, 0)),
            pl.BlockSpec((mem_len, CROSS_WIDTH), lambda i: (i // per_b, 1)),
            _resident(wco.shape),
            _resident((1, d)),
            _resident(wr_hi.shape), _resident(wr_lo.shape),
            _resident((1, ne)),
        ],
        out_specs=[
            pl.BlockSpec((tm, d), row),
            pl.BlockSpec((tm * SUBLANES, LANES), row),
            pl.BlockSpec((tm, ne), row),
        ],
        out_shape=[
            jax.ShapeDtypeStruct((t, d), F32),
            jax.ShapeDtypeStruct((t * SUBLANES, LANES), U32),
            jax.ShapeDtypeStruct((t, ne), F32),
        ],
        compiler_params=_cparams(("parallel",)),
        name="cross",
    )(x1, gc, wcq, memkv, memkv, wco, gm, wr_hi, wr_lo, br)


def _moe_kernel(te_ref, tv_ref, tc_ref, tok_ref, h_hbm, wg_ref, wu_ref, wd_ref, bg_ref, bu_ref, bd_ref, y_hbm,
                raw_ref, xb_ref, a_ref, ys_ref, cnt_ref, gsem, osem, *, nf, n_tiles, n_assign):
    i = pl.program_id(0)
    s = pl.program_id(1)
    tsb = MOE_TSB
    rpt = SUBLANES
    tm = xb_ref.shape[0]
    tf = wg_ref.shape[1]
    d = xb_ref.shape[1]
    half = d // 2
    n_blk = tm // tsb
    valid = tv_ref[i]
    n_sb = (valid + tsb - 1) // tsb
    nxt = jnp.minimum(i + 1, n_tiles - 1)
    next_active = (i + 1 < n_tiles) & (tv_ref[nxt] > 0)
    next_c0 = tc_ref[nxt]

    def row_copy(c0, k):
        tok = tok_ref[jnp.minimum(c0 + k, n_assign - 1)]
        src = h_hbm.at[pl.ds(pl.multiple_of(tok * rpt, rpt), rpt)]
        dst = raw_ref.at[pl.ds(pl.multiple_of(k * rpt, rpt), rpt)]
        return pltpu.make_async_copy(src, dst, gsem)

    def issue_rows(c0, k0, count):
        for u in range(count):
            row_copy(c0, k0 + u).start(priority=u % 2)

    def issue_loop(c0, k0):
        def group(gi, c):
            issue_rows(c0, k0 + gi * GATHER_UNROLL, GATHER_UNROLL)
            return c

        lax.fori_loop(0, lax.shift_right_logical(tm - k0, GATHER_UNROLL.bit_length() - 1), group, 0)

    def block_rows(ref, sb):
        return ref.at[pl.ds(pl.multiple_of(sb * (tsb * rpt), tsb * rpt), tsb * rpt)]

    def out_copy(t):
        dst = y_hbm.at[pl.ds(pl.multiple_of(t * (tm * rpt), tm * rpt), tm * rpt)]
        return pltpu.make_async_copy(ys_ref, dst, osem)

    @pl.when((i == 0) & (s == 0))
    def _():
        ys_ref[...] = jnp.zeros(ys_ref.shape, U32)

        @pl.when(valid > 0)
        def _():
            issue_loop(tc_ref[0], 0)

    @pl.when(s == 0)
    def _():
        cnt_ref[0] = 0

        @pl.when(valid > 0)
        def _():
            for sb in range(n_blk):
                pltpu.make_async_copy(block_rows(h_hbm, 0), block_rows(raw_ref, sb), gsem).wait()

        def unpack(sb, c):
            r0 = pl.multiple_of(sb * tsb, tsb)
            for sl in range(rpt):
                p = raw_ref[pl.ds(r0 * rpt + sl, tsb, stride=rpt), :]
                xb_ref[pl.ds(r0, tsb), sl * LANES:(sl + 1) * LANES] = _unpack_lo(p).astype(BF16)
                xb_ref[pl.ds(r0, tsb), half + sl * LANES:half + (sl + 1) * LANES] = _unpack_hi(p).astype(BF16)
            return c

        lax.fori_loop(0, n_sb, unpack, 0)

    def for_row_blocks(fn):
        big = MOE_BIG_BLOCK * tsb

        def big_block(b, c):
            r0 = pl.multiple_of(b * big, big)
            k0 = cnt_ref[0]
            issue = next_active & (k0 + MOE_BLOCK_COPIES <= tm)

            @pl.when(issue)
            def _():
                fn(r0, big, k0)
                cnt_ref[0] = k0 + MOE_BLOCK_COPIES

            @pl.when(jnp.logical_not(issue))
            def _():
                fn(r0, big, None)

            return c

        n_big = n_sb // MOE_BIG_BLOCK
        lax.fori_loop(0, n_big, big_block, 0)
        rem = n_sb - n_big * MOE_BIG_BLOCK

        @pl.when(rem >= 4)
        def _():
            fn(pl.multiple_of(n_big * big, 4 * tsb), 4 * tsb, None)

        def single(b, c):
            fn(pl.multiple_of(b * tsb, tsb), tsb, None)
            return c

        lax.fori_loop(n_sb - (rem & 3), n_sb, single, 0)

    @pl.when((s < nf) & (valid > 0))
    def _():
        def gate_up(r0, rows, k0):
            wg = wg_ref[...].astype(BF16)
            wu = wu_ref[...].astype(BF16)
            n_split = MOE_GATE_SPLIT
            piece = d // n_split
            per = MOE_BLOCK_COPIES // (2 * n_split)
            g = bg_ref[...]
            u = bu_ref[...]
            for j in range(n_split):
                xs = xb_ref[pl.ds(r0, rows), j * piece:(j + 1) * piece]
                if k0 is not None:
                    issue_rows(next_c0, k0 + (2 * j) * per, per)
                g = g + _dot(xs, wg[j * piece:(j + 1) * piece, :])
                if k0 is not None:
                    issue_rows(next_c0, k0 + (2 * j + 1) * per, per)
                u = u + _dot(xs, wu[j * piece:(j + 1) * piece, :])
            g = jnp.minimum(g, SWIGLU_LIMIT)
            u = jnp.clip(u, -SWIGLU_LIMIT, SWIGLU_LIMIT)
            a_ref[s, pl.ds(r0, rows), :] = ((u + 1.0) * (g * jax.nn.sigmoid(SWIGLU_ALPHA * g))).astype(BF16)

        for_row_blocks(gate_up)

    prev_active = (i > 0) & (tv_ref[jnp.maximum(i - 1, 0)] > 0)

    @pl.when((s == nf) & (valid > 0) & prev_active)
    def _():
        out_copy(i - 1).wait()

    @pl.when((valid == 0) & (s == 0))
    def _():
        @pl.when(prev_active)
        def _():
            out_copy(i - 1).wait()

        out_copy(i).start()

    @pl.when((valid == 0) & (s == 2 * nf - 1))
    def _():
        out_copy(i).wait()

    @pl.when((s >= nf) & (valid > 0))
    def _():
        n = s - nf

        def down(r0, rows, k0):
            wd = wd_ref[...].astype(BF16)
            per = MOE_BLOCK_COPIES // nf
            acc = bd_ref[...]
            for f in range(nf):
                if k0 is not None:
                    issue_rows(next_c0, k0 + f * per, per)
                acc = acc + _dot(a_ref[f, pl.ds(r0, rows), :], wd[f * tf:(f + 1) * tf, :])
            packed = _pack_halves(acc.astype(BF16).astype(F32))
            for j in range(tf // 2 // LANES):
                ys_ref[pl.ds(r0 * rpt + n * (tf // 2 // LANES) + j, rows, stride=rpt), :] = (
                    packed[:, j * LANES:(j + 1) * LANES])

        for_row_blocks(down)

    @pl.when((s == 2 * nf - 1) & (valid > 0))
    def _():
        @pl.when(next_active)
        def _():
            issue_loop(next_c0, cnt_ref[0])

        out_copy(i).start()

        @pl.when(i == n_tiles - 1)
        def _():
            out_copy(i).wait()


def _moe_ffn(tile_e, tile_valid, tile_c0, toks, h_tiles, w_gate, b_gate, w_up, b_up, w_down, b_down):
    e, d, dff = w_gate.shape
    tm, tf = MOE_TM, MOE_TF
    rpt = SUBLANES
    assert d == dff
    assert d == 2 * rpt * LANES, "a packed token row must be exactly one (8, 128) tile"
    assert tm % MOE_TSB == 0 and dff % tf == 0 and (tf // 2) % LANES == 0
    nf = dff // tf
    assert nf >= 2
    assert MOE_BLOCK_COPIES % (2 * MOE_GATE_SPLIT) == 0 and MOE_BLOCK_COPIES % nf == 0
    assert tm % MOE_BLOCK_COPIES == 0 and MOE_BLOCK_COPIES % GATHER_UNROLL == 0 and tm % GATHER_UNROLL == 0
    assert d % MOE_GATE_SPLIT == 0 and 1 <= MOE_BIG_BLOCK <= 8
    n_tiles = tile_e.shape[0]
    up = lambda i, s, te, tv, tc, tk: (te[i], 0, jnp.where(tv[i] > 0, jnp.minimum(s, nf - 1), nf - 1))
    dn = lambda i, s, te, tv, tc, tk: (te[i], 0, jnp.where(tv[i] > 0, jnp.maximum(s - nf, 0), nf - 1))
    grid_spec = pltpu.PrefetchScalarGridSpec(
        num_scalar_prefetch=4,
        grid=(n_tiles, 2 * nf),
        in_specs=[
            pl.BlockSpec(memory_space=pl.ANY),
            pl.BlockSpec((None, d, tf), up),
            pl.BlockSpec((None, d, tf), up),
            pl.BlockSpec((None, dff, tf), dn),
            pl.BlockSpec((None, 1, tf), up),
            pl.BlockSpec((None, 1, tf), up),
            pl.BlockSpec((None, 1, tf), dn),
        ],
        out_specs=pl.BlockSpec(memory_space=pl.ANY),
        scratch_shapes=[
            pltpu.VMEM((tm * rpt, LANES), U32),
            pltpu.VMEM((tm, d), BF16),
            pltpu.VMEM((nf, tm, tf), BF16),
            pltpu.VMEM((tm * rpt, LANES), U32),
            pltpu.SMEM((1,), jnp.int32),
            pltpu.SemaphoreType.DMA(()),
            pltpu.SemaphoreType.DMA(()),
        ],
    )
    return pl.pallas_call(
        functools.partial(_moe_kernel, nf=nf, n_tiles=n_tiles, n_assign=toks.shape[0]),
        grid_spec=grid_spec,
        out_shape=jax.ShapeDtypeStruct((n_tiles * tm * rpt, LANES), U32),
        compiler_params=_cparams(("arbitrary", "arbitrary")),
        name="moe_ffn",
    )(tile_e, tile_valid, tile_c0, toks, h_tiles, w_gate, w_up, w_down,
      b_gate.reshape(e, 1, dff), b_up.reshape(e, 1, dff), b_down.reshape(e, 1, d))


def _combine_kernel(pos_ref, x_ref, y_hbm, w_ref, g_ref, o_ref, buf_ref, sem, *, final, n_steps, tf):
    i = pl.program_id(0)
    tm, d = x_ref.shape
    rpt = SUBLANES
    slot = i % 2

    def issue(step, sl_):
        def body(pair, c):
            for u in range(2):
                tt = pair * 2 + u
                for k in range(TOP_K):
                    p = pos_ref[(step * tm + tt) * TOP_K + k]
                    src = y_hbm.at[pl.ds(pl.multiple_of(p * rpt, rpt), rpt)]
                    dst = buf_ref.at[sl_, pl.ds(pl.multiple_of((k * tm + tt) * rpt, rpt), rpt)]
                    pltpu.make_async_copy(src, dst, sem.at[sl_]).start(priority=k % 2)
            return c

        lax.fori_loop(0, tm // 2, body, 0)

    @pl.when(i == 0)
    def _():
        issue(0, 0)

    @pl.when(i + 1 < n_steps)
    def _():
        issue(i + 1, 1 - slot)

    pltpu.make_async_copy(y_hbm.at[pl.ds(0, TOP_K * tm * rpt)], buf_ref.at[slot], sem.at[slot]).wait()

    w = w_ref[...]
    wk = [jnp.broadcast_to(w[:, k:k + 1], (tm, LANES)) for k in range(TOP_K)]
    per_chunk = tf // 2 // LANES
    slabs = [None] * (d // LANES)
    for sl in range(rpt):
        n, j = sl // per_chunk, sl % per_chunk
        c_lo = (n * tf) // LANES + j
        c_hi = c_lo + per_chunk
        lo = x_ref[:, c_lo * LANES:(c_lo + 1) * LANES]
        hi = x_ref[:, c_hi * LANES:(c_hi + 1) * LANES]
        for k in range(TOP_K):
            p = buf_ref[slot, pl.ds(k * tm * rpt + sl, tm, stride=rpt), :]
            lo = lo + wk[k] * _unpack_lo(p)
            hi = hi + wk[k] * _unpack_hi(p)
        slabs[c_lo], slabs[c_hi] = lo, hi
    acc = jnp.concatenate(slabs, axis=1)
    o_ref[...] = _rms(acc, g_ref[...]) if final else acc


def _combine(pos, x2, y_tiles, top_w, g, final):
    t, d = x2.shape
    tm = COMBINE_TM
    rpt = SUBLANES
    n_steps = t // tm
    grid_spec = pltpu.PrefetchScalarGridSpec(
        num_scalar_prefetch=1,
        grid=(n_steps,),
        in_specs=[
            pl.BlockSpec((tm, d), lambda i, pos: (i, 0)),
            pl.BlockSpec(memory_space=pl.ANY),
            pl.BlockSpec((tm, TOP_K), lambda i, pos: (i, 0)),
            pl.BlockSpec((1, d), lambda i, pos: (0, 0)),
        ],
        out_specs=pl.BlockSpec((tm, d), lambda i, pos: (i, 0)),
        scratch_shapes=[
            pltpu.VMEM((2, TOP_K * tm * rpt, LANES), U32),
            pltpu.SemaphoreType.DMA((2,)),
        ],
    )
    return pl.pallas_call(
        functools.partial(_combine_kernel, final=final, n_steps=n_steps, tf=MOE_TF),
        grid_spec=grid_spec,
        out_shape=jax.ShapeDtypeStruct((t, d), F32),
        compiler_params=_cparams(("arbitrary",)),
        name="combine",
    )(pos, x2, y_tiles, top_w, g)


def _rope_tables(seq):
    half = ROT_DIM // 2
    inv = ROPE_THETA ** (-(jnp.arange(half, dtype=F32) * 2.0 / ROT_DIM))
    ang = jnp.arange(seq, dtype=F32)[:, None] * inv[None, :]
    cos, sin = jnp.cos(ang), jnp.sin(ang)
    ones = jnp.ones((seq, HEAD_DIM - ROT_DIM), F32)
    zeros = jnp.zeros((seq, HEAD_DIM - ROT_DIM), F32)
    zh = jnp.zeros((seq, half), F32)
    c = jnp.concatenate([cos, cos, ones], axis=1)
    sa = jnp.concatenate([-sin, zh, zeros], axis=1)
    sb = jnp.concatenate([zh, sin, zeros], axis=1)
    rep = LANES // HEAD_DIM
    return jnp.tile(c, (1, rep)), jnp.tile(sa, (1, rep)), jnp.tile(sb, (1, rep))


def _route(logits, tm):
    t = logits.shape[0]
    i32 = jnp.int32
    top_val, top_idx = lax.top_k(logits, TOP_K)
    top_w = jax.nn.softmax(top_val, axis=-1)
    n_assign = t * TOP_K
    flat_e = top_idx.reshape(n_assign).astype(i32)
    order = jnp.argsort(flat_e).astype(i32)
    rank = jnp.argsort(order).astype(i32)
    onehot = flat_e[:, None] == jnp.arange(N_EXPERTS, dtype=i32)[None, :]
    counts = jnp.sum(onehot, axis=0, dtype=i32)
    start = jnp.cumsum(counts) - counts
    tiles_e = (counts + tm - 1) // tm
    tile_end = jnp.cumsum(tiles_e)
    tile_first = tile_end - tiles_e
    n_tiles = n_assign // tm + N_EXPERTS
    j = jnp.arange(n_tiles, dtype=i32)
    tile_e = jnp.minimum(jnp.searchsorted(tile_end, j, side="right"), N_EXPERTS - 1).astype(i32)
    local = j - tile_first[tile_e]
    tile_valid = jnp.clip(counts[tile_e] - local * tm, 0, tm).astype(i32)
    tile_c0 = jnp.where(tile_valid > 0, start[tile_e] + local * tm, 0).astype(i32)
    last_e = jnp.max(jnp.where(tile_valid > 0, tile_e, 0))
    tile_e = jnp.where(tile_valid > 0, tile_e, last_e).astype(i32)
    row_base = jnp.sum(jnp.where(onehot, (tile_first * tm - start)[None, :], 0), axis=1, dtype=i32)
    pos = rank + row_base
    return top_w, order // TOP_K, pos, tile_e, tile_valid, tile_c0


def kernel(x, mem, g_mix, w_in, rpb_na, sinks, w_na_o, w_win_o, w_out, g_cross, g_mem, w_cq, w_ckv, w_co,
           g_moe, w_router, b_router, w_gate, b_gate, w_up, b_up, w_down, b_down, g_final):
    b, s, d = x.shape
    mem_len = mem.shape[1]
    t = b * s
    depth = w_in.shape[0]
    xc = x.reshape(t, d)
    mem2d = mem.reshape(b * mem_len, d)
    rope_c, rope_sa, rope_sb = _rope_tables(s)

    cols = _cols(d)
    n_attn = cols.end - cols.qa
    scale = HEAD_DIM ** -0.5
    o_qb, o_kb = 3 * NA_WIDTH, 3 * NA_WIDTH + WIN_Q_WIDTH

    for l in range(depth):
        w = w_in[l]
        w_in_l = jnp.concatenate([w[:, n_attn:], w[:, :NA_WIDTH] * scale, w[:, NA_WIDTH:o_qb],
                                  _swa_head_order(w[:, o_qb:o_kb] * scale, axis=1), w[:, o_kb:n_attn]],
                                 axis=1).astype(BF16)
        proj = _in_proj(xc, g_mix[l].reshape(1, d), w_in_l, rope_c, rope_sa, rope_sb, s)
        ya = _natten(proj, _natten_bias_table(rpb_na[l]), cols, b, s)
        yb = _swa(proj, sinks[l].astype(F32), cols, b, s)
        x1 = _merge(ya, yb, proj, xc, w_na_o[l].astype(BF16), _swa_head_order(w_win_o[l], axis=0).astype(BF16),
                    w_out[l].astype(BF16))
        memkv = _memkv(mem2d, g_mem[l].reshape(1, d), w_ckv[l].astype(BF16), mem_len)
        wr_hi = w_router[l].astype(BF16)
        wr_lo = (w_router[l] - wr_hi.astype(F32)).astype(BF16)
        x2, h3, logits = _cross(x1, g_cross[l].reshape(1, d), w_cq[l].astype(BF16), memkv, w_co[l].astype(BF16),
                                g_moe[l].reshape(1, d), wr_hi, wr_lo, b_router[l].reshape(1, -1), s, mem_len)
        top_w, toks, pos, tile_e, tile_valid, tile_c0 = _route(logits, MOE_TM)
        ys = _moe_ffn(tile_e, tile_valid, tile_c0, toks, h3, w_gate[l], b_gate[l], w_up[l], b_up[l],
                      w_down[l], b_down[l])
        xc = _combine(pos, x2, ys, top_w, g_final.reshape(1, d), final=(l + 1 == depth))
    return xc.reshape(b, s, d)
```

```python
import functools
from typing import NamedTuple

import jax
import jax.numpy as jnp
from jax import lax
from jax.experimental import pallas as pl
from jax.experimental.pallas import tpu as pltpu

F32 = jnp.float32
BF16 = jnp.bfloat16
U32 = jnp.uint32

GRID_W = 64
HEAD_DIM = 64
NA_HEADS = 16
NA_ROWS = 8
NA_COLS = 16
WIN_Q_HEADS = 16
WIN_KV_HEADS = 4
WINDOW = 128
ROT_DIM = HEAD_DIM // 4
ROPE_THETA = 500000.0
CROSS_HEADS = 4
CROSS_HEAD_DIM = 128
N_EXPERTS = 32
TOP_K = 4
SWIGLU_LIMIT = 7.0
SWIGLU_ALPHA = 1.702
NORM_EPS = 1e-5
NEG_INF = -1e30

LANES = 128
SUBLANES = 8
VMEM_LIMIT = 56 * 1024 * 1024

NA_WIDTH = NA_HEADS * HEAD_DIM
WIN_Q_WIDTH = WIN_Q_HEADS * HEAD_DIM
WIN_KV_WIDTH = WIN_KV_HEADS * HEAD_DIM
CROSS_WIDTH = CROSS_HEADS * CROSS_HEAD_DIM


class _Cols(NamedTuple):
    ga: int
    gb: int
    qa: int
    ka: int
    va: int
    qb: int
    kb: int
    vb: int
    end: int


def _cols(d):
    widths = [d, d, NA_WIDTH, NA_WIDTH, NA_WIDTH, WIN_Q_WIDTH, WIN_KV_WIDTH, WIN_KV_WIDTH]
    offs = [0]
    for w in widths:
        offs.append(offs[-1] + w)
    return _Cols(*offs)


PROJ_TM = 1024
PROJ_TN = 512
NA_UNROLL = 4
MERGE_TM = 256
CROSS_TM = 256
MOE_TM = 2304
MOE_TSB = 256
MOE_TF = 256
GATHER_UNROLL = 8
MOE_BIG_BLOCK = 8
MOE_BLOCK_COPIES = 192
MOE_GATE_SPLIT = 4
COMBINE_TM = 256


def _cparams(sem):
    return pltpu.CompilerParams(dimension_semantics=sem, vmem_limit_bytes=VMEM_LIMIT)


def _rms(x, g):
    ms = jnp.mean(x * x, axis=-1, keepdims=True)
    return x * lax.rsqrt(ms + NORM_EPS) * g


def _dot(a, b):
    return jnp.dot(a, b, preferred_element_type=F32)


def _dot_nt(a, b):
    return lax.dot_general(a, b, (((1,), (1,)), ((), ())), preferred_element_type=F32)


def _pack_halves(v):
    half = v.shape[1] // 2
    lo = lax.shift_right_logical(pltpu.bitcast(v[:, :half], U32), jnp.uint32(16))
    hi = pltpu.bitcast(v[:, half:], U32) & jnp.uint32(0xFFFF0000)
    return hi | lo


def _unpack_lo(p):
    return pltpu.bitcast(lax.shift_left(p, jnp.uint32(16)), F32)


def _unpack_hi(p):
    return pltpu.bitcast(p & jnp.uint32(0xFFFF0000), F32)


def _store_token_tiles(ref, packed):
    m, w = packed.shape
    n_sl = w // LANES
    for s in range(n_sl):
        ref[pl.ds(s, m, stride=n_sl), :] = packed[:, s * LANES:(s + 1) * LANES]


def _rope_chunk(x, c, sa, sb):
    return x * c + pltpu.roll(x, LANES - ROT_DIM // 2, 1) * sa + pltpu.roll(x, ROT_DIM // 2, 1) * sb


def _inproj_kernel(x_ref, g_ref, w_ref, c_ref, sa_ref, sb_ref, o_ref, h_ref, *, cols):
    j = pl.program_id(1)
    tn = o_ref.shape[1]
    nchunk = tn // LANES

    @pl.when(j == 0)
    def _():
        h_ref[...] = _rms(x_ref[...], g_ref[...]).astype(BF16)

    acc = _dot(h_ref[...], w_ref[...])
    o_ref[...] = acc.astype(BF16)

    j_qb, j_kb = cols.qb // tn, cols.kb // tn
    kb_chunks = WIN_KV_WIDTH // LANES

    @pl.when(j >= j_qb)
    def _():
        c, sa, sb = c_ref[...], sa_ref[...], sb_ref[...]

        def rotate(k):
            sl = slice(k * LANES, (k + 1) * LANES)
            o_ref[:, sl] = _rope_chunk(acc[:, sl], c, sa, sb).astype(BF16)

        for k in range(nchunk):
            if k < kb_chunks:
                rotate(k)
            else:
                pl.when(j < j_kb)(functools.partial(rotate, k))


def _in_proj(x2d, g, w_bf, rope_c, rope_sa, rope_sb, seq):
    t, d = x2d.shape
    n = w_bf.shape[1]
    cols = _cols(d)
    tm, tn = PROJ_TM, PROJ_TN
    assert t % tm == 0 and n % tn == 0 and seq % tm == 0 and n == cols.end
    assert all(c % tn == 0 for c in (cols.qa, cols.ka, cols.qb, cols.kb)) and cols.end == cols.kb + tn
    sblocks = seq // tm
    rope_spec = pl.BlockSpec((tm, LANES), lambda i, j: (i % sblocks, 0))
    return pl.pallas_call(
        functools.partial(_inproj_kernel, cols=cols),
        grid=(t // tm, n // tn),
        in_specs=[
            pl.BlockSpec((tm, d), lambda i, j: (i, 0)),
            pl.BlockSpec((1, d), lambda i, j: (0, 0)),
            pl.BlockSpec((d, tn), lambda i, j: (0, j)),
            rope_spec, rope_spec, rope_spec,
        ],
        out_specs=pl.BlockSpec((tm, tn), lambda i, j: (i, j)),
        out_shape=jax.ShapeDtypeStruct((t, n), BF16),
        scratch_shapes=[pltpu.VMEM((tm, d), BF16)],
        compiler_params=_cparams(("parallel", "arbitrary")),
        name="in_proj",
    )(x2d, g, w_bf, rope_c, rope_sa, rope_sb)


def _natten_kernel(q_ref, k_ref, v_ref, b_ref, o_ref, s_ref, p_ref, l_ref, *, rows):
    kr = NA_ROWS
    nu = NA_UNROLL
    lane = lax.broadcasted_iota(jnp.int32, (GRID_W, LANES), 1)
    first = lane < HEAD_DIM

    def window(r):
        rs = jnp.clip(r - kr // 2, 0, rows - kr)
        return pl.multiple_of(rs * GRID_W, GRID_W), rs - r + (NA_ROWS - 1)

    def stage_a(r, slot):
        k0, dr0 = window(r)
        q2 = q_ref[pl.ds(pl.multiple_of(r * GRID_W, GRID_W), GRID_W), :]
        k2 = k_ref[pl.ds(k0, kr * GRID_W), :]
        zero = jnp.zeros_like(q2)
        qs = jnp.concatenate([jnp.where(first, q2, zero), jnp.where(first, zero, q2)], axis=0)
        s_ref[slot] = _dot_nt(qs, k2) + b_ref[dr0]

    def stage_b(slot):
        s = s_ref[slot]
        m = jnp.max(s, axis=-1, keepdims=True)
        p = jnp.exp(s - m)
        l_ref[slot] = jnp.broadcast_to(1.0 / jnp.sum(p, axis=-1, keepdims=True), l_ref.shape[1:])
        p_ref[slot] = p.astype(BF16)

    def stage_c(r, slot):
        k0, _ = window(r)
        v2 = v_ref[pl.ds(k0, kr * GRID_W), :]
        o = _dot(p_ref[slot], v2) * l_ref[slot]
        o_ref[pl.ds(pl.multiple_of(r * GRID_W, GRID_W), GRID_W), :] = (
            jnp.where(first, o[:GRID_W], o[GRID_W:]).astype(BF16))

    def iteration(t, do_a, do_b, do_c):
        for j in range(nu):
            if do_c:
                stage_c((t - 2) * nu + j, j)
        for j in range(nu):
            if do_b:
                stage_b(j)
        for j in range(nu):
            if do_a:
                stage_a(t * nu + j, j)

    nt = rows // nu
    iteration(0, True, False, False)
    iteration(1, True, True, False)

    def body(t, carry):
        iteration(t, True, True, True)
        return carry

    lax.fori_loop(2, nt, body, 0)
    iteration(nt, False, True, True)
    iteration(nt + 1, False, False, True)


def _natten(proj, bias_tab, cols, batch, seq):
    npairs = NA_HEADS // 2
    rows = seq // GRID_W
    nu = NA_UNROLL
    assert rows >= NA_ROWS and rows % nu == 0 and rows // nu >= 2
    kq, kk, kv = cols.qa // LANES, cols.ka // LANES, cols.va // LANES
    blk = (seq, LANES)
    return pl.pallas_call(
        functools.partial(_natten_kernel, rows=rows),
        grid=(batch, npairs),
        in_specs=[
            pl.BlockSpec(blk, lambda b, p: (b, kq + p)),
            pl.BlockSpec(blk, lambda b, p: (b, kk + p)),
            pl.BlockSpec(blk, lambda b, p: (b, kv + p)),
            pl.BlockSpec((None, NA_ROWS, 2 * GRID_W, NA_ROWS * GRID_W), lambda b, p: (p, 0, 0, 0)),
        ],
        out_specs=pl.BlockSpec(blk, lambda b, p: (b, p)),
        out_shape=jax.ShapeDtypeStruct((batch * seq, NA_WIDTH), BF16),
        scratch_shapes=[
            pltpu.VMEM((nu, 2 * GRID_W, NA_ROWS * GRID_W), F32),
            pltpu.VMEM((nu, 2 * GRID_W, NA_ROWS * GRID_W), BF16),
            pltpu.VMEM((nu, 2 * GRID_W, LANES), F32),
        ],
        compiler_params=_cparams(("parallel", "parallel")),
        name="natten",
    )(proj, proj, proj, bias_tab)


def _natten_bias_table(rpb):
    cols = jnp.arange(GRID_W)
    col_start = jnp.clip(cols - NA_COLS // 2, 0, GRID_W - NA_COLS)
    kc = jnp.arange(GRID_W)
    inwin = (kc[None, :] >= col_start[:, None]) & (kc[None, :] < col_start[:, None] + NA_COLS)
    dc = kc[None, :] - cols[:, None] + (NA_COLS - 1)
    onehot = (dc[:, :, None] == jnp.arange(2 * NA_COLS - 1)[None, None, :]).astype(F32)
    exp_c = jnp.einsum("ckd,hrd->hrck", onehot, rpb.astype(F32), precision=lax.Precision.HIGHEST)
    tab = jnp.stack([exp_c[:, v:v + NA_ROWS] for v in range(NA_ROWS)], axis=1)
    tab = jnp.where(inwin[None, None, None], tab, NEG_INF)
    tab = tab.transpose(0, 1, 3, 2, 4).reshape(NA_HEADS, NA_ROWS, GRID_W, NA_ROWS * GRID_W)
    tab = tab.reshape(NA_HEADS // 2, 2, NA_ROWS, GRID_W, NA_ROWS * GRID_W).transpose(0, 2, 1, 3, 4)
    return tab.reshape(NA_HEADS // 2, NA_ROWS, 2 * GRID_W, NA_ROWS * GRID_W)


def _swa_kernel(sink_ref, q_ref, k_ref, v_ref, mask_ref, o_ref, s_ref, p_ref, l_ref, *, nb):
    kp = pl.program_id(1)
    w = WINDOW
    grp = WIN_Q_HEADS // WIN_KV_HEADS
    lane = lax.broadcasted_iota(jnp.int32, (w, LANES), 1)
    first = lane < HEAD_DIM
    rowhalf = lax.broadcasted_iota(jnp.int32, (2 * w, 1), 0) < w

    def window(n):
        start = jnp.clip(n - 1, 0, nb - 3)
        return pl.multiple_of(start * w, w), n - start

    def stage_a(n, c):
        k0, variant = window(n)
        q2 = q_ref[pl.ds(pl.multiple_of(n * w, w), w), c * LANES:(c + 1) * LANES]
        zero = jnp.zeros_like(q2)
        qs = jnp.concatenate([jnp.where(first, q2, zero), jnp.where(first, zero, q2)], axis=0)
        s_ref[c] = _dot_nt(qs, k_ref[pl.ds(k0, 3 * w), :]) + mask_ref[variant]

    def stage_b(c):
        s = s_ref[c]
        sink = jnp.where(rowhalf, sink_ref[(2 * kp) * grp + c], sink_ref[(2 * kp + 1) * grp + c])
        m = jnp.maximum(jnp.max(s, axis=-1, keepdims=True), sink)
        p = jnp.exp(s - m)
        denom = jnp.sum(p, axis=-1, keepdims=True) + jnp.exp(sink - m)
        l_ref[c] = jnp.broadcast_to(1.0 / denom, l_ref.shape[1:])
        p_ref[c] = p.astype(BF16)

    def stage_c(n, c):
        k0, _ = window(n)
        o = _dot(p_ref[c], v_ref[pl.ds(k0, 3 * w), :]) * l_ref[c]
        o_ref[pl.ds(pl.multiple_of(n * w, w), w), c * LANES:(c + 1) * LANES] = (
            jnp.where(first, o[:w], o[w:]).astype(BF16))

    def iteration(n, do_a, do_b, do_c):
        for c in range(grp):
            if do_c:
                stage_c(n - 2, c)
        for c in range(grp):
            if do_b:
                stage_b(c)
        for c in range(grp):
            if do_a:
                stage_a(n, c)

    iteration(0, True, False, False)
    iteration(1, True, True, False)

    def body(n, carry):
        iteration(n, True, True, True)
        return carry

    lax.fori_loop(2, nb, body, 0)
    iteration(nb, False, True, True)
    iteration(nb + 1, False, False, True)


def _swa_mask_table():
    w = WINDOW
    qi = jnp.arange(2 * w) % w
    kj = jnp.arange(3 * w)
    off = kj[None, None, :] - w * jnp.arange(3)[:, None, None] - qi[None, :, None]
    return jnp.where(jnp.abs(off) <= w, 0.0, NEG_INF).astype(F32)


def _swa(proj, sinks, cols, batch, seq):
    w = WINDOW
    nb = seq // w
    grp = WIN_Q_HEADS // WIN_KV_HEADS
    qw = grp * LANES
    assert cols.qb % qw == 0 and nb >= 3
    cq, ck, cv = cols.qb // qw, cols.kb // LANES, cols.vb // LANES
    return pl.pallas_call(
        functools.partial(_swa_kernel, nb=nb),
        grid=(batch, WIN_KV_HEADS // 2),
        in_specs=[
            pl.BlockSpec(memory_space=pltpu.SMEM),
            pl.BlockSpec((seq, qw), lambda b, kp: (b, cq + kp)),
            pl.BlockSpec((seq, LANES), lambda b, kp: (b, ck + kp)),
            pl.BlockSpec((seq, LANES), lambda b, kp: (b, cv + kp)),
            pl.BlockSpec((3, 2 * w, 3 * w), lambda b, kp: (0, 0, 0)),
        ],
        out_specs=pl.BlockSpec((seq, qw), lambda b, kp: (b, kp)),
        out_shape=jax.ShapeDtypeStruct((batch * seq, WIN_Q_WIDTH), BF16),
        scratch_shapes=[
            pltpu.VMEM((grp, 2 * w, 3 * w), F32),
            pltpu.VMEM((grp, 2 * w, 3 * w), BF16),
            pltpu.VMEM((grp, 2 * w, LANES), F32),
        ],
        compiler_params=_cparams(("parallel", "parallel")),
        name="swa",
    )(sinks, proj, proj, proj, _swa_mask_table())


def _swa_head_order(w, axis):
    grp = WIN_Q_HEADS // WIN_KV_HEADS
    shape = w.shape
    split = shape[:axis] + (WIN_KV_HEADS // 2, 2, grp, HEAD_DIM) + shape[axis + 1:]
    return jnp.swapaxes(w.reshape(split), axis + 1, axis + 2).reshape(shape)


def _merge_kernel(ya_ref, yb_ref, ga_ref, gb_ref, x_ref, wna_ref, wwin_ref, wout_ref, o_ref):
    a = _dot(ya_ref[...], wna_ref[...])
    b = _dot(yb_ref[...], wwin_ref[...])
    merged = jax.nn.sigmoid(ga_ref[...].astype(F32)) * a + jax.nn.sigmoid(gb_ref[...].astype(F32)) * b
    o_ref[...] = x_ref[...] + _dot(merged.astype(BF16), wout_ref[...])


def _resident(shape):
    return pl.BlockSpec(shape, lambda i: (0,) * len(shape), pipeline_mode=pl.Buffered(1))


def _merge(ya, yb, proj, x2d, wna, wwin, wout):
    t, d = x2d.shape
    tm = MERGE_TM
    return pl.pallas_call(
        _merge_kernel,
        grid=(t // tm,),
        in_specs=[
            pl.BlockSpec((tm, NA_WIDTH), lambda i: (i, 0)),
            pl.BlockSpec((tm, WIN_Q_WIDTH), lambda i: (i, 0)),
            pl.BlockSpec((tm, d), lambda i: (i, 0)),
            pl.BlockSpec((tm, d), lambda i: (i, 1)),
            pl.BlockSpec((tm, d), lambda i: (i, 0)),
            _resident(wna.shape), _resident(wwin.shape), _resident(wout.shape),
        ],
        out_specs=pl.BlockSpec((tm, d), lambda i: (i, 0)),
        out_shape=jax.ShapeDtypeStruct((t, d), F32),
        compiler_params=_cparams(("parallel",)),
        name="merge",
    )(ya, yb, proj, proj, x2d, wna, wwin, wout)


def _memkv_kernel(m_ref, g_ref, w_ref, o_ref):
    o_ref[...] = _dot(_rms(m_ref[...], g_ref[...]).astype(BF16), w_ref[...]).astype(BF16)


def _memkv(mem2d, g, w_bf, mem_len):
    t, d = mem2d.shape
    n = w_bf.shape[1]
    return pl.pallas_call(
        _memkv_kernel,
        grid=(t // mem_len,),
        in_specs=[
            pl.BlockSpec((mem_len, d), lambda i: (i, 0)),
            pl.BlockSpec((1, d), lambda i: (0, 0)),
            pl.BlockSpec((d, n), lambda i: (0, 0)),
        ],
        out_specs=pl.BlockSpec((mem_len, n), lambda i: (i, 0)),
        out_shape=jax.ShapeDtypeStruct((t, n), BF16),
        compiler_params=_cparams(("parallel",)),
        name="memkv",
    )(mem2d, g, w_bf)


def _cross_kernel(x_ref, gc_ref, wcq_ref, k_ref, v_ref, wco_ref, gm_ref, wrc_ref, br_ref,
                  x2_ref, h_ref, tw_ref, ti_ref):
    x = x_ref[...]
    q = _dot(_rms(x, gc_ref[...]).astype(BF16), wcq_ref[...]).astype(BF16)
    scale = CROSS_HEAD_DIM ** -0.5
    outs = []
    for h in range(CROSS_HEADS):
        sl = slice(h * CROSS_HEAD_DIM, (h + 1) * CROSS_HEAD_DIM)
        s = _dot_nt(q[:, sl], k_ref[:, sl]) * scale
        m = jnp.max(s, axis=-1, keepdims=True)
        p = jnp.exp(s - m)
        l = jnp.sum(p, axis=-1, keepdims=True)
        outs.append((_dot(p.astype(BF16), v_ref[:, sl]) * (1.0 / l)).astype(BF16))
    o = jnp.concatenate(outs, axis=-1)
    x2 = x + _dot(o, wco_ref[...])
    x2_ref[...] = x2
    h = _rms(x2, gm_ref[...])
    h_hi = h.astype(BF16)
    h_lo = (h - h_hi.astype(F32)).astype(BF16)
    _store_token_tiles(h_ref, _pack_halves(h_hi.astype(F32)))
    ne = br_ref.shape[1]
    hh = _dot(h_hi, wrc_ref[...])
    lg = hh[:, :ne] + hh[:, ne:] + _dot(h_lo, wrc_ref[:, :ne]) + br_ref[...]
    col = lax.broadcasted_iota(jnp.int32, lg.shape, 1)
    vals, idxs = [], []
    for _ in range(TOP_K):
        m = jnp.max(lg, axis=-1, keepdims=True)
        idx = jnp.min(jnp.where(lg == m, col, ne), axis=-1, keepdims=True)
        vals.append(m)
        idxs.append(idx)
        lg = jnp.where(col == idx, -jnp.inf, lg)
    top = jnp.concatenate(vals, axis=-1)
    e = jnp.exp(top - vals[0])
    tw_ref[...] = e / jnp.sum(e, axis=-1, keepdims=True)
    ti_ref[...] = jnp.concatenate(idxs, axis=-1)


def _cross(x1, gc, wcq, memkv, wco, gm, wr_cat, br, seq, mem_len):
    t, d = x1.shape
    tm = CROSS_TM
    ne = br.shape[1]
    per_b = seq // tm
    row = lambda i: (i, 0)
    return pl.pallas_call(
        _cross_kernel,
        grid=(t // tm,),
        in_specs=[
            pl.BlockSpec((tm, d), row),
            _resident((1, d)),
            _resident(wcq.shape),
            pl.BlockSpec((mem_len, CROSS_WIDTH), lambda i: (i // per_b, 0)),
            pl.BlockSpec((mem_len, CROSS_WIDTH), lambda i: (i // per_b, 1)),
            _resident(wco.shape),
            _resident((1, d)),
            _resident(wr_cat.shape),
            _resident((1, ne)),
        ],
        out_specs=[
            pl.BlockSpec((tm, d), row),
            pl.BlockSpec((tm * SUBLANES, LANES), row),
            pl.BlockSpec((tm, TOP_K), row),
            pl.BlockSpec((tm, TOP_K), row),
        ],
        out_shape=[
            jax.ShapeDtypeStruct((t, d), F32),
            jax.ShapeDtypeStruct((t * SUBLANES, LANES), U32),
            jax.ShapeDtypeStruct((t, TOP_K), F32),
            jax.ShapeDtypeStruct((t, TOP_K), jnp.int32),
        ],
        compiler_params=_cparams(("parallel",)),
        name="cross",
    )(x1, gc, wcq, memkv, memkv, wco, gm, wr_cat, br)


def _moe_kernel(te_ref, tv_ref, tc_ref, tok_ref, h_hbm, wg_ref, wu_ref, wd_ref, bg_ref, bu_ref, bd_ref, y_hbm,
                raw_ref, xb_ref, a_ref, ys_ref, cnt_ref, gsem, osem, *, nf, n_tiles, n_assign):
    i = pl.program_id(0)
    s = pl.program_id(1)
    tsb = MOE_TSB
    rpt = SUBLANES
    tm = xb_ref.shape[0]
    tf = wg_ref.shape[1]
    d = xb_ref.shape[1]
    half = d // 2
    n_blk = tm // tsb
    valid = tv_ref[i]
    n_sb = (valid + tsb - 1) // tsb
    nxt = jnp.minimum(i + 1, n_tiles - 1)
    next_active = (i + 1 < n_tiles) & (tv_ref[nxt] > 0)
    next_c0 = tc_ref[nxt]

    def row_copy(c0, k):
        tok = tok_ref[jnp.minimum(c0 + k, n_assign - 1)]
        src = h_hbm.at[pl.ds(pl.multiple_of(tok * rpt, rpt), rpt)]
        dst = raw_ref.at[pl.ds(pl.multiple_of(k * rpt, rpt), rpt)]
        return pltpu.make_async_copy(src, dst, gsem)

    def issue_rows(c0, k0, count):
        for u in range(count):
            row_copy(c0, k0 + u).start(priority=u % 2)

    def issue_loop(c0, k0):
        def group(gi, c):
            issue_rows(c0, k0 + gi * GATHER_UNROLL, GATHER_UNROLL)
            return c

        lax.fori_loop(0, lax.shift_right_logical(tm - k0, GATHER_UNROLL.bit_length() - 1), group, 0)

    def block_rows(ref, sb):
        return ref.at[pl.ds(pl.multiple_of(sb * (tsb * rpt), tsb * rpt), tsb * rpt)]

    def out_copy_at(first_row):
        dst = y_hbm.at[pl.ds(pl.multiple_of(first_row * rpt, rpt), tm * rpt)]
        return pltpu.make_async_copy(ys_ref, dst, osem)

    def out_copy(t):
        return out_copy_at(tc_ref[t])

    @pl.when((i == 0) & (s == 0))
    def _():
        ys_ref[...] = jnp.zeros(ys_ref.shape, U32)
        out_copy_at(n_assign).start()
        out_copy_at(n_assign).wait()

        @pl.when(valid > 0)
        def _():
            issue_loop(tc_ref[0], 0)

    @pl.when(s == 0)
    def _():
        cnt_ref[0] = 0

        @pl.when(valid > 0)
        def _():
            for sb in range(n_blk):
                pltpu.make_async_copy(block_rows(h_hbm, 0), block_rows(raw_ref, sb), gsem).wait()

        def unpack(sb, c):
            r0 = pl.multiple_of(sb * tsb, tsb)
            for sl in range(rpt):
                p = raw_ref[pl.ds(r0 * rpt + sl, tsb, stride=rpt), :]
                xb_ref[pl.ds(r0, tsb), sl * LANES:(sl + 1) * LANES] = _unpack_lo(p).astype(BF16)
                xb_ref[pl.ds(r0, tsb), half + sl * LANES:half + (sl + 1) * LANES] = _unpack_hi(p).astype(BF16)
            return c

        lax.fori_loop(0, n_sb, unpack, 0)

    def for_row_blocks(fn):
        big = MOE_BIG_BLOCK * tsb

        def big_block(b, c):
            r0 = pl.multiple_of(b * big, big)
            k0 = cnt_ref[0]
            issue = next_active & (k0 + MOE_BLOCK_COPIES <= tm)

            @pl.when(issue)
            def _():
                fn(r0, big, k0)
                cnt_ref[0] = k0 + MOE_BLOCK_COPIES

            @pl.when(jnp.logical_not(issue))
            def _():
                fn(r0, big, None)

            return c

        n_big = n_sb // MOE_BIG_BLOCK
        lax.fori_loop(0, n_big, big_block, 0)
        rem = n_sb - n_big * MOE_BIG_BLOCK

        @pl.when(rem >= 4)
        def _():
            fn(pl.multiple_of(n_big * big, 4 * tsb), 4 * tsb, None)

        def single(b, c):
            fn(pl.multiple_of(b * tsb, tsb), tsb, None)
            return c

        lax.fori_loop(n_sb - (rem & 3), n_sb, single, 0)

    @pl.when((s < nf) & (valid > 0))
    def _():
        def gate_up(r0, rows, k0):
            wg = wg_ref[...].astype(BF16)
            wu = wu_ref[...].astype(BF16)
            n_split = MOE_GATE_SPLIT
            piece = d // n_split
            per = MOE_BLOCK_COPIES // (2 * n_split)
            g = bg_ref[...]
            u = bu_ref[...]
            for j in range(n_split):
                xs = xb_ref[pl.ds(r0, rows), j * piece:(j + 1) * piece]
                if k0 is not None:
                    issue_rows(next_c0, k0 + (2 * j) * per, per)
                g = g + _dot(xs, wg[j * piece:(j + 1) * piece, :])
                if k0 is not None:
                    issue_rows(next_c0, k0 + (2 * j + 1) * per, per)
                u = u + _dot(xs, wu[j * piece:(j + 1) * piece, :])
            g = jnp.minimum(g, SWIGLU_LIMIT)
            u = jnp.clip(u, -SWIGLU_LIMIT, SWIGLU_LIMIT)
            a_ref[s, pl.ds(r0, rows), :] = ((u + 1.0) * (g * jax.nn.sigmoid(SWIGLU_ALPHA * g))).astype(BF16)

        for_row_blocks(gate_up)

    prev_active = (i > 0) & (tv_ref[jnp.maximum(i - 1, 0)] > 0)

    @pl.when((s == nf) & (valid > 0) & prev_active)
    def _():
        out_copy(i - 1).wait()

    @pl.when((valid == 0) & (s == 0) & prev_active)
    def _():
        out_copy(i - 1).wait()

    @pl.when((s >= nf) & (valid > 0))
    def _():
        n = s - nf

        def down(r0, rows, k0):
            wd = wd_ref[...].astype(BF16)
            per = MOE_BLOCK_COPIES // nf
            acc = bd_ref[...]
            for f in range(nf):
                if k0 is not None:
                    issue_rows(next_c0, k0 + f * per, per)
                acc = acc + _dot(a_ref[f, pl.ds(r0, rows), :], wd[f * tf:(f + 1) * tf, :])
            packed = _pack_halves(acc.astype(BF16).astype(F32))
            for j in range(tf // 2 // LANES):
                ys_ref[pl.ds(r0 * rpt + n * (tf // 2 // LANES) + j, rows, stride=rpt), :] = (
                    packed[:, j * LANES:(j + 1) * LANES])

        for_row_blocks(down)

    @pl.when((s == 2 * nf - 1) & (valid > 0))
    def _():
        @pl.when(next_active)
        def _():
            issue_loop(next_c0, cnt_ref[0])

        out_copy(i).start()

        @pl.when(i == n_tiles - 1)
        def _():
            out_copy(i).wait()


def _moe_ffn(tile_e, tile_valid, tile_c0, toks, h_tiles, w_gate, b_gate, w_up, b_up, w_down, b_down):
    e, d, dff = w_gate.shape
    tm, tf = MOE_TM, MOE_TF
    rpt = SUBLANES
    assert d == dff
    assert d == 2 * rpt * LANES, "a packed token row must be exactly one (8, 128) tile"
    assert tm % MOE_TSB == 0 and dff % tf == 0 and (tf // 2) % LANES == 0
    nf = dff // tf
    assert nf >= 2
    assert MOE_BLOCK_COPIES % (2 * MOE_GATE_SPLIT) == 0 and MOE_BLOCK_COPIES % nf == 0
    assert tm % MOE_BLOCK_COPIES == 0 and MOE_BLOCK_COPIES % GATHER_UNROLL == 0 and tm % GATHER_UNROLL == 0
    assert d % MOE_GATE_SPLIT == 0 and 1 <= MOE_BIG_BLOCK <= 8
    n_tiles = tile_e.shape[0]
    up = lambda i, s, te, tv, tc, tk: (te[i], 0, jnp.where(tv[i] > 0, jnp.minimum(s, nf - 1), nf - 1))
    dn = lambda i, s, te, tv, tc, tk: (te[i], 0, jnp.where(tv[i] > 0, jnp.maximum(s - nf, 0), nf - 1))
    grid_spec = pltpu.PrefetchScalarGridSpec(
        num_scalar_prefetch=4,
        grid=(n_tiles, 2 * nf),
        in_specs=[
            pl.BlockSpec(memory_space=pl.ANY),
            pl.BlockSpec((None, d, tf), up),
            pl.BlockSpec((None, d, tf), up),
            pl.BlockSpec((None, dff, tf), dn),
            pl.BlockSpec((None, 1, tf), up),
            pl.BlockSpec((None, 1, tf), up),
            pl.BlockSpec((None, 1, tf), dn),
        ],
        out_specs=pl.BlockSpec(memory_space=pl.ANY),
        scratch_shapes=[
            pltpu.VMEM((tm * rpt, LANES), U32),
            pltpu.VMEM((tm, d), BF16),
            pltpu.VMEM((nf, tm, tf), BF16),
            pltpu.VMEM((tm * rpt, LANES), U32),
            pltpu.SMEM((1,), jnp.int32),
            pltpu.SemaphoreType.DMA(()),
            pltpu.SemaphoreType.DMA(()),
        ],
    )
    return pl.pallas_call(
        functools.partial(_moe_kernel, nf=nf, n_tiles=n_tiles, n_assign=toks.shape[0]),
        grid_spec=grid_spec,
        out_shape=jax.ShapeDtypeStruct(((toks.shape[0] + tm) * rpt, LANES), U32),
        compiler_params=_cparams(("arbitrary", "arbitrary")),
        name="moe_ffn",
    )(tile_e, tile_valid, tile_c0, toks, h_tiles, w_gate, w_up, w_down,
      b_gate.reshape(e, 1, dff), b_up.reshape(e, 1, dff), b_down.reshape(e, 1, d))


def _combine_kernel(pos_ref, x_ref, y_hbm, w_ref, g_ref, o_ref, buf_ref, sem, *, final, n_steps, tf):
    i = pl.program_id(0)
    tm, d = x_ref.shape
    rpt = SUBLANES
    slot = i % 2

    def issue(step, sl_):
        def body(pair, c):
            for u in range(2):
                tt = pair * 2 + u
                for k in range(TOP_K):
                    p = pos_ref[(step * tm + tt) * TOP_K + k]
                    src = y_hbm.at[pl.ds(pl.multiple_of(p * rpt, rpt), rpt)]
                    dst = buf_ref.at[sl_, pl.ds(pl.multiple_of((k * tm + tt) * rpt, rpt), rpt)]
                    pltpu.make_async_copy(src, dst, sem.at[sl_]).start(priority=k % 2)
            return c

        lax.fori_loop(0, tm // 2, body, 0)

    @pl.when(i == 0)
    def _():
        issue(0, 0)

    @pl.when(i + 1 < n_steps)
    def _():
        issue(i + 1, 1 - slot)

    pltpu.make_async_copy(y_hbm.at[pl.ds(0, TOP_K * tm * rpt)], buf_ref.at[slot], sem.at[slot]).wait()

    w = w_ref[...]
    wk = [jnp.broadcast_to(w[:, k:k + 1], (tm, LANES)) for k in range(TOP_K)]
    per_chunk = tf // 2 // LANES
    slabs = [None] * (d // LANES)
    for sl in range(rpt):
        n, j = sl // per_chunk, sl % per_chunk
        c_lo = (n * tf) // LANES + j
        c_hi = c_lo + per_chunk
        lo = x_ref[:, c_lo * LANES:(c_lo + 1) * LANES]
        hi = x_ref[:, c_hi * LANES:(c_hi + 1) * LANES]
        for k in range(TOP_K):
            p = buf_ref[slot, pl.ds(k * tm * rpt + sl, tm, stride=rpt), :]
            lo = lo + wk[k] * _unpack_lo(p)
            hi = hi + wk[k] * _unpack_hi(p)
        slabs[c_lo], slabs[c_hi] = lo, hi
    acc = jnp.concatenate(slabs, axis=1)
    o_ref[...] = _rms(acc, g_ref[...]) if final else acc


def _combine(pos, x2, y_tiles, top_w, g, final):
    t, d = x2.shape
    tm = COMBINE_TM
    rpt = SUBLANES
    n_steps = t // tm
    grid_spec = pltpu.PrefetchScalarGridSpec(
        num_scalar_prefetch=1,
        grid=(n_steps,),
        in_specs=[
            pl.BlockSpec((tm, d), lambda i, pos: (i, 0)),
            pl.BlockSpec(memory_space=pl.ANY),
            pl.BlockSpec((tm, TOP_K), lambda i, pos: (i, 0)),
            pl.BlockSpec((1, d), lambda i, pos: (0, 0)),
        ],
        out_specs=pl.BlockSpec((tm, d), lambda i, pos: (i, 0)),
        scratch_shapes=[
            pltpu.VMEM((2, TOP_K * tm * rpt, LANES), U32),
            pltpu.SemaphoreType.DMA((2,)),
        ],
    )
    return pl.pallas_call(
        functools.partial(_combine_kernel, final=final, n_steps=n_steps, tf=MOE_TF),
        grid_spec=grid_spec,
        out_shape=jax.ShapeDtypeStruct((t, d), F32),
        compiler_params=_cparams(("arbitrary",)),
        name="combine",
    )(pos, x2, y_tiles, top_w, g)


def _rope_tables(seq):
    half = ROT_DIM // 2
    inv = ROPE_THETA ** (-(jnp.arange(half, dtype=F32) * 2.0 / ROT_DIM))
    ang = jnp.arange(seq, dtype=F32)[:, None] * inv[None, :]
    cos, sin = jnp.cos(ang), jnp.sin(ang)
    ones = jnp.ones((seq, HEAD_DIM - ROT_DIM), F32)
    zeros = jnp.zeros((seq, HEAD_DIM - ROT_DIM), F32)
    zh = jnp.zeros((seq, half), F32)
    c = jnp.concatenate([cos, cos, ones], axis=1)
    sa = jnp.concatenate([-sin, zh, zeros], axis=1)
    sb = jnp.concatenate([zh, sin, zeros], axis=1)
    rep = LANES // HEAD_DIM
    return jnp.tile(c, (1, rep)), jnp.tile(sa, (1, rep)), jnp.tile(sb, (1, rep))


def _route(top_idx, tm):
    t = top_idx.shape[0]
    i32 = jnp.int32
    n_assign = t * TOP_K
    flat_e = top_idx.reshape(n_assign).astype(i32)
    order = jnp.argsort(flat_e).astype(i32)
    rank = jnp.argsort(order).astype(i32)
    onehot = flat_e[:, None] == jnp.arange(N_EXPERTS, dtype=i32)[None, :]
    counts = jnp.sum(onehot, axis=0, dtype=i32)
    start = jnp.cumsum(counts) - counts
    tiles_e = (counts + tm - 1) // tm
    tile_end = jnp.cumsum(tiles_e)
    tile_first = tile_end - tiles_e
    n_tiles = n_assign // tm + N_EXPERTS
    j = jnp.arange(n_tiles, dtype=i32)
    tile_e = jnp.minimum(jnp.searchsorted(tile_end, j, side="right"), N_EXPERTS - 1).astype(i32)
    local = j - tile_first[tile_e]
    tile_valid = jnp.clip(counts[tile_e] - local * tm, 0, tm).astype(i32)
    tile_c0 = jnp.where(tile_valid > 0, start[tile_e] + local * tm, 0).astype(i32)
    last_e = jnp.max(jnp.where(tile_valid > 0, tile_e, 0))
    tile_e = jnp.where(tile_valid > 0, tile_e, last_e).astype(i32)
    return order // TOP_K, rank, tile_e, tile_valid, tile_c0


def kernel(x, mem, g_mix, w_in, rpb_na, sinks, w_na_o, w_win_o, w_out, g_cross, g_mem, w_cq, w_ckv, w_co,
           g_moe, w_router, b_router, w_gate, b_gate, w_up, b_up, w_down, b_down, g_final):
    b, s, d = x.shape
    mem_len = mem.shape[1]
    t = b * s
    depth = w_in.shape[0]
    xc = x.reshape(t, d)
    mem2d = mem.reshape(b * mem_len, d)
    rope_c, rope_sa, rope_sb = _rope_tables(s)

    cols = _cols(d)
    n_attn = cols.end - cols.qa
    scale = HEAD_DIM ** -0.5
    o_qb, o_kb = 3 * NA_WIDTH, 3 * NA_WIDTH + WIN_Q_WIDTH

    for l in range(depth):
        w = w_in[l]
        w_in_l = jnp.concatenate([w[:, n_attn:], w[:, :NA_WIDTH] * scale, w[:, NA_WIDTH:o_qb],
                                  _swa_head_order(w[:, o_qb:o_kb] * scale, axis=1), w[:, o_kb:n_attn]],
                                 axis=1).astype(BF16)
        proj = _in_proj(xc, g_mix[l].reshape(1, d), w_in_l, rope_c, rope_sa, rope_sb, s)
        ya = _natten(proj, _natten_bias_table(rpb_na[l]), cols, b, s)
        yb = _swa(proj, sinks[l].astype(F32), cols, b, s)
        x1 = _merge(ya, yb, proj, xc, w_na_o[l].astype(BF16), _swa_head_order(w_win_o[l], axis=0).astype(BF16),
                    w_out[l].astype(BF16))
        memkv = _memkv(mem2d, g_mem[l].reshape(1, d), w_ckv[l].astype(BF16), mem_len)
        wr_hi = w_router[l].astype(BF16)
        wr_lo = (w_router[l] - wr_hi.astype(F32)).astype(BF16)
        x2, h3, top_w, top_idx = _cross(x1, g_cross[l].reshape(1, d), w_cq[l].astype(BF16), memkv,
                                        w_co[l].astype(BF16), g_moe[l].reshape(1, d),
                                        jnp.concatenate([wr_hi, wr_lo], axis=1), b_router[l].reshape(1, -1),
                                        s, mem_len)
        toks, pos, tile_e, tile_valid, tile_c0 = _route(top_idx, MOE_TM)
        ys = _moe_ffn(tile_e, tile_valid, tile_c0, toks, h3, w_gate[l], b_gate[l], w_up[l], b_up[l],
                      w_down[l], b_down[l])
        xc = _combine(pos, x2, ys, top_w, g_final.reshape(1, d), final=(l + 1 == depth))
    return xc.reshape(b, s, d)
```

```python
import functools
from typing import NamedTuple

import jax
import jax.numpy as jnp
from jax import lax
from jax.experimental import pallas as pl
from jax.experimental.pallas import tpu as pltpu

F32 = jnp.float32
BF16 = jnp.bfloat16
U32 = jnp.uint32

GRID_W = 64
HEAD_DIM = 64
NA_HEADS = 16
NA_ROWS = 8
NA_COLS = 16
WIN_Q_HEADS = 16
WIN_KV_HEADS = 4
WINDOW = 128
ROT_DIM = HEAD_DIM // 4
ROPE_THETA = 500000.0
CROSS_HEADS = 4
CROSS_HEAD_DIM = 128
N_EXPERTS = 32
TOP_K = 4
SWIGLU_LIMIT = 7.0
SWIGLU_ALPHA = 1.702
NORM_EPS = 1e-5
NEG_INF = -1e30

LANES = 128
SUBLANES = 8
VMEM_LIMIT = 56 * 1024 * 1024

NA_WIDTH = NA_HEADS * HEAD_DIM
WIN_Q_WIDTH = WIN_Q_HEADS * HEAD_DIM
WIN_KV_WIDTH = WIN_KV_HEADS * HEAD_DIM
CROSS_WIDTH = CROSS_HEADS * CROSS_HEAD_DIM


class _Cols(NamedTuple):
    ga: int
    gb: int
    qa: int
    ka: int
    va: int
    qb: int
    kb: int
    vb: int
    end: int


def _cols(d):
    widths = [d, d, NA_WIDTH, NA_WIDTH, NA_WIDTH, WIN_Q_WIDTH, WIN_KV_WIDTH, WIN_KV_WIDTH]
    offs = [0]
    for w in widths:
        offs.append(offs[-1] + w)
    return _Cols(*offs)


PROJ_TM = 1024
PROJ_TN = 512
NA_UNROLL = 4
MERGE_TM = 256
CROSS_TM = 512
MOE_TM = 2304
MOE_TSB = 256
MOE_TF = 256
GATHER_UNROLL = 8
MOE_BIG_BLOCK = 8
MOE_BLOCK_COPIES = 192
MOE_GATE_SPLIT = 4
COMBINE_TM = 256


def _cparams(sem):
    return pltpu.CompilerParams(dimension_semantics=sem, vmem_limit_bytes=VMEM_LIMIT)


def _rms(x, g):
    ms = jnp.mean(x * x, axis=-1, keepdims=True)
    return x * lax.rsqrt(ms + NORM_EPS) * g


def _dot(a, b):
    return jnp.dot(a, b, preferred_element_type=F32)


def _dot_nt(a, b):
    return lax.dot_general(a, b, (((1,), (1,)), ((), ())), preferred_element_type=F32)


def _pack_halves(v):
    half = v.shape[1] // 2
    lo = lax.shift_right_logical(pltpu.bitcast(v[:, :half], U32), jnp.uint32(16))
    hi = pltpu.bitcast(v[:, half:], U32) & jnp.uint32(0xFFFF0000)
    return hi | lo


def _unpack_lo(p):
    return pltpu.bitcast(lax.shift_left(p, jnp.uint32(16)), F32)


def _unpack_hi(p):
    return pltpu.bitcast(p & jnp.uint32(0xFFFF0000), F32)


def _store_token_tiles(ref, packed):
    m, w = packed.shape
    n_sl = w // LANES
    for s in range(n_sl):
        ref[pl.ds(s, m, stride=n_sl), :] = packed[:, s * LANES:(s + 1) * LANES]


def _rope_chunk(x, c, sa, sb):
    return x * c + pltpu.roll(x, LANES - ROT_DIM // 2, 1) * sa + pltpu.roll(x, ROT_DIM // 2, 1) * sb


def _inproj_kernel(x_ref, g_ref, w_ref, c_ref, sa_ref, sb_ref, o_ref, h_ref, *, cols):
    j = pl.program_id(1)
    tn = o_ref.shape[1]
    nchunk = tn // LANES

    @pl.when(j == 0)
    def _():
        h_ref[...] = _rms(x_ref[...], g_ref[...]).astype(BF16)

    acc = _dot(h_ref[...], w_ref[...])
    o_ref[...] = acc.astype(BF16)

    j_qb, j_kb = cols.qb // tn, cols.kb // tn
    kb_chunks = WIN_KV_WIDTH // LANES

    @pl.when(j >= j_qb)
    def _():
        c, sa, sb = c_ref[...], sa_ref[...], sb_ref[...]

        def rotate(k):
            sl = slice(k * LANES, (k + 1) * LANES)
            o_ref[:, sl] = _rope_chunk(acc[:, sl], c, sa, sb).astype(BF16)

        for k in range(nchunk):
            if k < kb_chunks:
                rotate(k)
            else:
                pl.when(j < j_kb)(functools.partial(rotate, k))


def _in_proj(x2d, g, w_bf, rope_c, rope_sa, rope_sb, seq):
    t, d = x2d.shape
    n = w_bf.shape[1]
    cols = _cols(d)
    tm, tn = PROJ_TM, PROJ_TN
    assert t % tm == 0 and n % tn == 0 and seq % tm == 0 and n == cols.end
    assert all(c % tn == 0 for c in (cols.qa, cols.ka, cols.qb, cols.kb)) and cols.end == cols.kb + tn
    sblocks = seq // tm
    rope_spec = pl.BlockSpec((tm, LANES), lambda i, j: (i % sblocks, 0))
    return pl.pallas_call(
        functools.partial(_inproj_kernel, cols=cols),
        grid=(t // tm, n // tn),
        in_specs=[
            pl.BlockSpec((tm, d), lambda i, j: (i, 0)),
            pl.BlockSpec((1, d), lambda i, j: (0, 0)),
            pl.BlockSpec((d, tn), lambda i, j: (0, j)),
            rope_spec, rope_spec, rope_spec,
        ],
        out_specs=pl.BlockSpec((tm, tn), lambda i, j: (i, j)),
        out_shape=jax.ShapeDtypeStruct((t, n), BF16),
        scratch_shapes=[pltpu.VMEM((tm, d), BF16)],
        compiler_params=_cparams(("parallel", "arbitrary")),
        name="in_proj",
    )(x2d, g, w_bf, rope_c, rope_sa, rope_sb)


def _natten_kernel(q_ref, k_ref, v_ref, b_ref, o_ref, s_ref, p_ref, l_ref, *, rows):
    kr = NA_ROWS
    nu = NA_UNROLL
    lane = lax.broadcasted_iota(jnp.int32, (GRID_W, LANES), 1)
    first = lane < HEAD_DIM

    def window(r):
        rs = jnp.clip(r - kr // 2, 0, rows - kr)
        return pl.multiple_of(rs * GRID_W, GRID_W), rs - r + (NA_ROWS - 1)

    def stage_a(r, slot):
        k0, dr0 = window(r)
        q2 = q_ref[pl.ds(pl.multiple_of(r * GRID_W, GRID_W), GRID_W), :]
        k2 = k_ref[pl.ds(k0, kr * GRID_W), :]
        zero = jnp.zeros_like(q2)
        qs = jnp.concatenate([jnp.where(first, q2, zero), jnp.where(first, zero, q2)], axis=0)
        s_ref[slot] = _dot_nt(qs, k2) + b_ref[dr0]

    def stage_b(slot):
        s = s_ref[slot]
        m = jnp.max(s, axis=-1, keepdims=True)
        p = jnp.exp(s - m)
        l_ref[slot] = jnp.broadcast_to(1.0 / jnp.sum(p, axis=-1, keepdims=True), l_ref.shape[1:])
        p_ref[slot] = p.astype(BF16)

    def stage_c(r, slot):
        k0, _ = window(r)
        v2 = v_ref[pl.ds(k0, kr * GRID_W), :]
        o = _dot(p_ref[slot], v2) * l_ref[slot]
        o_ref[pl.ds(pl.multiple_of(r * GRID_W, GRID_W), GRID_W), :] = (
            jnp.where(first, o[:GRID_W], o[GRID_W:]).astype(BF16))

    def iteration(t, do_a, do_b, do_c):
        for j in range(nu):
            if do_c:
                stage_c((t - 2) * nu + j, j)
        for j in range(nu):
            if do_b:
                stage_b(j)
        for j in range(nu):
            if do_a:
                stage_a(t * nu + j, j)

    nt = rows // nu
    iteration(0, True, False, False)
    iteration(1, True, True, False)

    def body(t, carry):
        iteration(t, True, True, True)
        return carry

    lax.fori_loop(2, nt, body, 0)
    iteration(nt, False, True, True)
    iteration(nt + 1, False, False, True)


def _natten(proj, bias_tab, cols, batch, seq):
    npairs = NA_HEADS // 2
    rows = seq // GRID_W
    nu = NA_UNROLL
    assert rows >= NA_ROWS and rows % nu == 0 and rows // nu >= 2
    kq, kk, kv = cols.qa // LANES, cols.ka // LANES, cols.va // LANES
    blk = (seq, LANES)
    return pl.pallas_call(
        functools.partial(_natten_kernel, rows=rows),
        grid=(batch, npairs),
        in_specs=[
            pl.BlockSpec(blk, lambda b, p: (b, kq + p)),
            pl.BlockSpec(blk, lambda b, p: (b, kk + p)),
            pl.BlockSpec(blk, lambda b, p: (b, kv + p)),
            pl.BlockSpec((None, NA_ROWS, 2 * GRID_W, NA_ROWS * GRID_W), lambda b, p: (p, 0, 0, 0)),
        ],
        out_specs=pl.BlockSpec(blk, lambda b, p: (b, p)),
        out_shape=jax.ShapeDtypeStruct((batch * seq, NA_WIDTH), BF16),
        scratch_shapes=[
            pltpu.VMEM((nu, 2 * GRID_W, NA_ROWS * GRID_W), F32),
            pltpu.VMEM((nu, 2 * GRID_W, NA_ROWS * GRID_W), BF16),
            pltpu.VMEM((nu, 2 * GRID_W, LANES), F32),
        ],
        compiler_params=_cparams(("parallel", "parallel")),
        name="natten",
    )(proj, proj, proj, bias_tab)


def _natten_bias_kernel(rpb_ref, o_ref):
    p = pl.program_id(0)
    n_dr, n_dc = 2 * NA_ROWS - 1, 2 * NA_COLS - 1
    c = lax.broadcasted_iota(jnp.int32, (GRID_W, LANES), 0)
    lane = lax.broadcasted_iota(jnp.int32, (GRID_W, LANES), 1)
    kc = jnp.where(lane >= GRID_W, lane - GRID_W, lane)
    dc = kc - c + (NA_COLS - 1)
    c0 = jnp.clip(c - NA_COLS // 2, 0, GRID_W - NA_COLS)
    inwin = (kc >= c0) & (kc < c0 + NA_COLS)
    for hh in range(2):
        base = (2 * p + hh) * (n_dr * n_dc)
        rows = []
        for dr in range(n_dr):
            t = jnp.zeros((GRID_W, LANES), F32)
            for j in range(n_dc):
                t = jnp.where(dc == j, rpb_ref[base + dr * n_dc + j], t)
            rows.append(jnp.where(inwin, t, NEG_INF))
        for v in range(NA_ROWS):
            for i in range(0, NA_ROWS, 2):
                o_ref[v, hh * GRID_W:(hh + 1) * GRID_W, i * GRID_W:(i + 2) * GRID_W] = (
                    jnp.where(lane < GRID_W, rows[v + i], rows[v + i + 1]))


def _natten_bias_table(rpb):
    assert 2 * GRID_W == LANES and NA_ROWS % 2 == 0
    return pl.pallas_call(
        _natten_bias_kernel,
        grid=(NA_HEADS // 2,),
        in_specs=[pl.BlockSpec(memory_space=pltpu.SMEM)],
        out_specs=pl.BlockSpec((None, NA_ROWS, 2 * GRID_W, NA_ROWS * GRID_W), lambda p: (p, 0, 0, 0)),
        out_shape=jax.ShapeDtypeStruct((NA_HEADS // 2, NA_ROWS, 2 * GRID_W, NA_ROWS * GRID_W), F32),
        compiler_params=_cparams(("parallel",)),
        name="natten_bias",
    )(rpb.astype(F32).reshape(-1))


def _swa_kernel(sink_ref, q_ref, k_ref, v_ref, mask_ref, o_ref, s_ref, p_ref, l_ref, *, nb):
    kp = pl.program_id(1)
    w = WINDOW
    grp = WIN_Q_HEADS // WIN_KV_HEADS
    lane = lax.broadcasted_iota(jnp.int32, (w, LANES), 1)
    first = lane < HEAD_DIM
    rowhalf = lax.broadcasted_iota(jnp.int32, (2 * w, 1), 0) < w

    def window(n):
        start = jnp.clip(n - 1, 0, nb - 3)
        return pl.multiple_of(start * w, w), n - start

    def stage_a(n, c):
        k0, variant = window(n)
        q2 = q_ref[pl.ds(pl.multiple_of(n * w, w), w), c * LANES:(c + 1) * LANES]
        zero = jnp.zeros_like(q2)
        qs = jnp.concatenate([jnp.where(first, q2, zero), jnp.where(first, zero, q2)], axis=0)
        s_ref[c] = _dot_nt(qs, k_ref[pl.ds(k0, 3 * w), :]) + mask_ref[variant]

    def stage_b(c):
        s = s_ref[c]
        sink = jnp.where(rowhalf, sink_ref[(2 * kp) * grp + c], sink_ref[(2 * kp + 1) * grp + c])
        m = jnp.maximum(jnp.max(s, axis=-1, keepdims=True), sink)
        p = jnp.exp(s - m)
        denom = jnp.sum(p, axis=-1, keepdims=True) + jnp.exp(sink - m)
        l_ref[c] = jnp.broadcast_to(1.0 / denom, l_ref.shape[1:])
        p_ref[c] = p.astype(BF16)

    def stage_c(n, c):
        k0, _ = window(n)
        o = _dot(p_ref[c], v_ref[pl.ds(k0, 3 * w), :]) * l_ref[c]
        o_ref[pl.ds(pl.multiple_of(n * w, w), w), c * LANES:(c + 1) * LANES] = (
            jnp.where(first, o[:w], o[w:]).astype(BF16))

    def iteration(n, do_a, do_b, do_c):
        for c in range(grp):
            if do_c:
                stage_c(n - 2, c)
        for c in range(grp):
            if do_b:
                stage_b(c)
        for c in range(grp):
            if do_a:
                stage_a(n, c)

    iteration(0, True, False, False)
    iteration(1, True, True, False)

    def body(n, carry):
        iteration(n, True, True, True)
        return carry

    lax.fori_loop(2, nb, body, 0)
    iteration(nb, False, True, True)
    iteration(nb + 1, False, False, True)


def _swa_mask_table():
    w = WINDOW
    qi = jnp.arange(2 * w) % w
    kj = jnp.arange(3 * w)
    off = kj[None, None, :] - w * jnp.arange(3)[:, None, None] - qi[None, :, None]
    return jnp.where(jnp.abs(off) <= w, 0.0, NEG_INF).astype(F32)


def _swa(proj, sinks, cols, batch, seq):
    w = WINDOW
    nb = seq // w
    grp = WIN_Q_HEADS // WIN_KV_HEADS
    qw = grp * LANES
    assert cols.qb % qw == 0 and nb >= 3
    cq, ck, cv = cols.qb // qw, cols.kb // LANES, cols.vb // LANES
    return pl.pallas_call(
        functools.partial(_swa_kernel, nb=nb),
        grid=(batch, WIN_KV_HEADS // 2),
        in_specs=[
            pl.BlockSpec(memory_space=pltpu.SMEM),
            pl.BlockSpec((seq, qw), lambda b, kp: (b, cq + kp)),
            pl.BlockSpec((seq, LANES), lambda b, kp: (b, ck + kp)),
            pl.BlockSpec((seq, LANES), lambda b, kp: (b, cv + kp)),
            pl.BlockSpec((3, 2 * w, 3 * w), lambda b, kp: (0, 0, 0)),
        ],
        out_specs=pl.BlockSpec((seq, qw), lambda b, kp: (b, kp)),
        out_shape=jax.ShapeDtypeStruct((batch * seq, WIN_Q_WIDTH), BF16),
        scratch_shapes=[
            pltpu.VMEM((grp, 2 * w, 3 * w), F32),
            pltpu.VMEM((grp, 2 * w, 3 * w), BF16),
            pltpu.VMEM((grp, 2 * w, LANES), F32),
        ],
        compiler_params=_cparams(("parallel", "parallel")),
        name="swa",
    )(sinks, proj, proj, proj, _swa_mask_table())


def _swa_head_order(w, axis):
    grp = WIN_Q_HEADS // WIN_KV_HEADS
    shape = w.shape
    split = shape[:axis] + (WIN_KV_HEADS // 2, 2, grp, HEAD_DIM) + shape[axis + 1:]
    return jnp.swapaxes(w.reshape(split), axis + 1, axis + 2).reshape(shape)


def _merge_kernel(ya_ref, yb_ref, ga_ref, gb_ref, x_ref, wna_ref, wwin_ref, wout_ref, o_ref):
    a = _dot(ya_ref[...], wna_ref[...])
    b = _dot(yb_ref[...], wwin_ref[...])
    merged = jax.nn.sigmoid(ga_ref[...].astype(F32)) * a + jax.nn.sigmoid(gb_ref[...].astype(F32)) * b
    o_ref[...] = x_ref[...] + _dot(merged.astype(BF16), wout_ref[...])


def _resident(shape):
    return pl.BlockSpec(shape, lambda i: (0,) * len(shape), pipeline_mode=pl.Buffered(1))


def _merge(ya, yb, proj, x2d, wna, wwin, wout):
    t, d = x2d.shape
    tm = MERGE_TM
    return pl.pallas_call(
        _merge_kernel,
        grid=(t // tm,),
        in_specs=[
            pl.BlockSpec((tm, NA_WIDTH), lambda i: (i, 0)),
            pl.BlockSpec((tm, WIN_Q_WIDTH), lambda i: (i, 0)),
            pl.BlockSpec((tm, d), lambda i: (i, 0)),
            pl.BlockSpec((tm, d), lambda i: (i, 1)),
            pl.BlockSpec((tm, d), lambda i: (i, 0)),
            _resident(wna.shape), _resident(wwin.shape), _resident(wout.shape),
        ],
        out_specs=pl.BlockSpec((tm, d), lambda i: (i, 0)),
        out_shape=jax.ShapeDtypeStruct((t, d), F32),
        compiler_params=_cparams(("parallel",)),
        name="merge",
    )(ya, yb, proj, proj, x2d, wna, wwin, wout)


def _memkv_kernel(m_ref, g_ref, w_ref, o_ref):
    o_ref[...] = _dot(_rms(m_ref[...], g_ref[...]).astype(BF16), w_ref[...]).astype(BF16)


def _memkv(mem2d, g, w_bf, mem_len):
    t, d = mem2d.shape
    n = w_bf.shape[1]
    return pl.pallas_call(
        _memkv_kernel,
        grid=(t // mem_len,),
        in_specs=[
            pl.BlockSpec((mem_len, d), lambda i: (i, 0)),
            pl.BlockSpec((1, d), lambda i: (0, 0)),
            pl.BlockSpec((d, n), lambda i: (0, 0)),
        ],
        out_specs=pl.BlockSpec((mem_len, n), lambda i: (i, 0)),
        out_shape=jax.ShapeDtypeStruct((t, n), BF16),
        compiler_params=_cparams(("parallel",)),
        name="memkv",
    )(mem2d, g, w_bf)


def _cross_kernel(x_ref, gc_ref, wcq_ref, k_ref, v_ref, wco_ref, gm_ref, wrc_ref, br_ref,
                  x2_ref, h_ref, tw_ref, ti_ref):
    x = x_ref[...]
    q = _dot(_rms(x, gc_ref[...]).astype(BF16), wcq_ref[...]).astype(BF16)
    scale = CROSS_HEAD_DIM ** -0.5
    outs = []
    for h in range(CROSS_HEADS):
        sl = slice(h * CROSS_HEAD_DIM, (h + 1) * CROSS_HEAD_DIM)
        s = _dot_nt(q[:, sl], k_ref[:, sl]) * scale
        m = jnp.max(s, axis=-1, keepdims=True)
        p = jnp.exp(s - m)
        l = jnp.sum(p, axis=-1, keepdims=True)
        outs.append((_dot(p.astype(BF16), v_ref[:, sl]) * (1.0 / l)).astype(BF16))
    o = jnp.concatenate(outs, axis=-1)
    x2 = x + _dot(o, wco_ref[...])
    x2_ref[...] = x2
    h = _rms(x2, gm_ref[...])
    h_hi = h.astype(BF16)
    h_lo = (h - h_hi.astype(F32)).astype(BF16)
    _store_token_tiles(h_ref, _pack_halves(h_hi.astype(F32)))
    ne = br_ref.shape[1]
    hh = _dot(h_hi, wrc_ref[...])
    lg = hh[:, :ne] + hh[:, ne:] + _dot(h_lo, wrc_ref[:, :ne]) + br_ref[...]
    col = lax.broadcasted_iota(jnp.int32, lg.shape, 1)
    vals, idxs = [], []
    for _ in range(TOP_K):
        m = jnp.max(lg, axis=-1, keepdims=True)
        idx = jnp.min(jnp.where(lg == m, col, ne), axis=-1, keepdims=True)
        vals.append(m)
        idxs.append(idx)
        lg = jnp.where(col == idx, -jnp.inf, lg)
    top = jnp.concatenate(vals, axis=-1)
    e = jnp.exp(top - vals[0])
    tw_ref[...] = e / jnp.sum(e, axis=-1, keepdims=True)
    ti_ref[...] = jnp.concatenate(idxs, axis=-1)


def _cross(x1, gc, wcq, memkv, wco, gm, wr_cat, br, seq, mem_len):
    t, d = x1.shape
    tm = CROSS_TM
    ne = br.shape[1]
    per_b = seq // tm
    row = lambda i: (i, 0)
    return pl.pallas_call(
        _cross_kernel,
        grid=(t // tm,),
        in_specs=[
            pl.BlockSpec((tm, d), row),
            _resident((1, d)),
            _resident(wcq.shape),
            pl.BlockSpec((mem_len, CROSS_WIDTH), lambda i: (i // per_b, 0)),
            pl.BlockSpec((mem_len, CROSS_WIDTH), lambda i: (i // per_b, 1)),
            _resident(wco.shape),
            _resident((1, d)),
            _resident(wr_cat.shape),
            _resident((1, ne)),
        ],
        out_specs=[
            pl.BlockSpec((tm, d), row),
            pl.BlockSpec((tm * SUBLANES, LANES), row),
            pl.BlockSpec((tm, TOP_K), row),
            pl.BlockSpec((tm, TOP_K), row),
        ],
        out_shape=[
            jax.ShapeDtypeStruct((t, d), F32),
            jax.ShapeDtypeStruct((t * SUBLANES, LANES), U32),
            jax.ShapeDtypeStruct((t, TOP_K), F32),
            jax.ShapeDtypeStruct((t, TOP_K), jnp.int32),
        ],
        compiler_params=_cparams(("parallel",)),
        name="cross",
    )(x1, gc, wcq, memkv, memkv, wco, gm, wr_cat, br)


def _moe_kernel(te_ref, tv_ref, tc_ref, tok_ref, h_hbm, wg_ref, wu_ref, wd_ref, bg_ref, bu_ref, bd_ref, y_hbm,
                raw_ref, xb_ref, a_ref, ys_ref, cnt_ref, gsem, osem, *, nf, n_tiles, n_assign):
    i = pl.program_id(0)
    s = pl.program_id(1)
    tsb = MOE_TSB
    rpt = SUBLANES
    tm = xb_ref.shape[0]
    tf = wg_ref.shape[1]
    d = xb_ref.shape[1]
    half = d // 2
    n_blk = tm // tsb
    valid = tv_ref[i]
    n_sb = (valid + tsb - 1) // tsb
    nxt = jnp.minimum(i + 1, n_tiles - 1)
    next_active = (i + 1 < n_tiles) & (tv_ref[nxt] > 0)
    next_c0 = tc_ref[nxt]

    def row_copy(c0, k):
        tok = tok_ref[jnp.minimum(c0 + k, n_assign - 1)]
        src = h_hbm.at[pl.ds(pl.multiple_of(tok * rpt, rpt), rpt)]
        dst = raw_ref.at[pl.ds(pl.multiple_of(k * rpt, rpt), rpt)]
        return pltpu.make_async_copy(src, dst, gsem)

    def issue_rows(c0, k0, count):
        for u in range(count):
            row_copy(c0, k0 + u).start(priority=u % 2)

    def issue_loop(c0, k0):
        def group(gi, c):
            issue_rows(c0, k0 + gi * GATHER_UNROLL, GATHER_UNROLL)
            return c

        lax.fori_loop(0, lax.shift_right_logical(tm - k0, GATHER_UNROLL.bit_length() - 1), group, 0)

    def block_rows(ref, sb):
        return ref.at[pl.ds(pl.multiple_of(sb * (tsb * rpt), tsb * rpt), tsb * rpt)]

    def out_copy_at(first_row):
        dst = y_hbm.at[pl.ds(pl.multiple_of(first_row * rpt, rpt), tm * rpt)]
        return pltpu.make_async_copy(ys_ref, dst, osem)

    def out_copy(t):
        return out_copy_at(tc_ref[t])

    @pl.when((i == 0) & (s == 0))
    def _():
        ys_ref[...] = jnp.zeros(ys_ref.shape, U32)
        out_copy_at(n_assign).start()
        out_copy_at(n_assign).wait()

        @pl.when(valid > 0)
        def _():
            issue_loop(tc_ref[0], 0)

    @pl.when(s == 0)
    def _():
        cnt_ref[0] = 0

        @pl.when(valid > 0)
        def _():
            for sb in range(n_blk):
                pltpu.make_async_copy(block_rows(h_hbm, 0), block_rows(raw_ref, sb), gsem).wait()

        def unpack(sb, c):
            r0 = pl.multiple_of(sb * tsb, tsb)
            for sl in range(rpt):
                p = raw_ref[pl.ds(r0 * rpt + sl, tsb, stride=rpt), :]
                xb_ref[pl.ds(r0, tsb), sl * LANES:(sl + 1) * LANES] = _unpack_lo(p).astype(BF16)
                xb_ref[pl.ds(r0, tsb), half + sl * LANES:half + (sl + 1) * LANES] = _unpack_hi(p).astype(BF16)
            return c

        lax.fori_loop(0, n_sb, unpack, 0)

    def for_row_blocks(fn):
        big = MOE_BIG_BLOCK * tsb

        def big_block(b, c):
            r0 = pl.multiple_of(b * big, big)
            k0 = cnt_ref[0]
            issue = next_active & (k0 + MOE_BLOCK_COPIES <= tm)

            @pl.when(issue)
            def _():
                fn(r0, big, k0)
                cnt_ref[0] = k0 + MOE_BLOCK_COPIES

            @pl.when(jnp.logical_not(issue))
            def _():
                fn(r0, big, None)

            return c

        n_big = n_sb // MOE_BIG_BLOCK
        lax.fori_loop(0, n_big, big_block, 0)
        rem = n_sb - n_big * MOE_BIG_BLOCK

        @pl.when(rem >= 4)
        def _():
            fn(pl.multiple_of(n_big * big, 4 * tsb), 4 * tsb, None)

        def single(b, c):
            fn(pl.multiple_of(b * tsb, tsb), tsb, None)
            return c

        lax.fori_loop(n_sb - (rem & 3), n_sb, single, 0)

    @pl.when((s < nf) & (valid > 0))
    def _():
        def gate_up(r0, rows, k0):
            wg = wg_ref[...].astype(BF16)
            wu = wu_ref[...].astype(BF16)
            n_split = MOE_GATE_SPLIT
            piece = d // n_split
            per = MOE_BLOCK_COPIES // (2 * n_split)
            g = bg_ref[...]
            u = bu_ref[...]
            for j in range(n_split):
                xs = xb_ref[pl.ds(r0, rows), j * piece:(j + 1) * piece]
                if k0 is not None:
                    issue_rows(next_c0, k0 + (2 * j) * per, per)
                g = g + _dot(xs, wg[j * piece:(j + 1) * piece, :])
                if k0 is not None:
                    issue_rows(next_c0, k0 + (2 * j + 1) * per, per)
                u = u + _dot(xs, wu[j * piece:(j + 1) * piece, :])
            g = jnp.minimum(g, SWIGLU_LIMIT)
            u = jnp.clip(u, -SWIGLU_LIMIT, SWIGLU_LIMIT)
            a_ref[s, pl.ds(r0, rows), :] = ((u + 1.0) * (g * jax.nn.sigmoid(SWIGLU_ALPHA * g))).astype(BF16)

        for_row_blocks(gate_up)

    prev_active = (i > 0) & (tv_ref[jnp.maximum(i - 1, 0)] > 0)

    @pl.when((s == nf) & (valid > 0) & prev_active)
    def _():
        out_copy(i - 1).wait()

    @pl.when((valid == 0) & (s == 0) & prev_active)
    def _():
        out_copy(i - 1).wait()

    @pl.when((s >= nf) & (valid > 0))
    def _():
        n = s - nf

        def down(r0, rows, k0):
            wd = wd_ref[...].astype(BF16)
            per = MOE_BLOCK_COPIES // nf
            acc = bd_ref[...]
            for f in range(nf):
                if k0 is not None:
                    issue_rows(next_c0, k0 + f * per, per)
                acc = acc + _dot(a_ref[f, pl.ds(r0, rows), :], wd[f * tf:(f + 1) * tf, :])
            packed = _pack_halves(acc.astype(BF16).astype(F32))
            for j in range(tf // 2 // LANES):
                ys_ref[pl.ds(r0 * rpt + n * (tf // 2 // LANES) + j, rows, stride=rpt), :] = (
                    packed[:, j * LANES:(j + 1) * LANES])

        for_row_blocks(down)

    @pl.when((s == 2 * nf - 1) & (valid > 0))
    def _():
        @pl.when(next_active)
        def _():
            issue_loop(next_c0, cnt_ref[0])

        out_copy(i).start()

        @pl.when(i == n_tiles - 1)
        def _():
            out_copy(i).wait()


def _moe_ffn(tile_e, tile_valid, tile_c0, toks, h_tiles, w_gate, b_gate, w_up, b_up, w_down, b_down):
    e, d, dff = w_gate.shape
    tm, tf = MOE_TM, MOE_TF
    rpt = SUBLANES
    assert d == dff
    assert d == 2 * rpt * LANES, "a packed token row must be exactly one (8, 128) tile"
    assert tm % MOE_TSB == 0 and dff % tf == 0 and (tf // 2) % LANES == 0
    nf = dff // tf
    assert nf >= 2
    assert MOE_BLOCK_COPIES % (2 * MOE_GATE_SPLIT) == 0 and MOE_BLOCK_COPIES % nf == 0
    assert tm % MOE_BLOCK_COPIES == 0 and MOE_BLOCK_COPIES % GATHER_UNROLL == 0 and tm % GATHER_UNROLL == 0
    assert d % MOE_GATE_SPLIT == 0 and 1 <= MOE_BIG_BLOCK <= 8
    n_tiles = tile_e.shape[0]
    up = lambda i, s, te, tv, tc, tk: (te[i], 0, jnp.where(tv[i] > 0, jnp.minimum(s, nf - 1), nf - 1))
    dn = lambda i, s, te, tv, tc, tk: (te[i], 0, jnp.where(tv[i] > 0, jnp.maximum(s - nf, 0), nf - 1))
    grid_spec = pltpu.PrefetchScalarGridSpec(
        num_scalar_prefetch=4,
        grid=(n_tiles, 2 * nf),
        in_specs=[
            pl.BlockSpec(memory_space=pl.ANY),
            pl.BlockSpec((None, d, tf), up),
            pl.BlockSpec((None, d, tf), up),
            pl.BlockSpec((None, dff, tf), dn),
            pl.BlockSpec((None, 1, tf), up),
            pl.BlockSpec((None, 1, tf), up),
            pl.BlockSpec((None, 1, tf), dn),
        ],
        out_specs=pl.BlockSpec(memory_space=pl.ANY),
        scratch_shapes=[
            pltpu.VMEM((tm * rpt, LANES), U32),
            pltpu.VMEM((tm, d), BF16),
            pltpu.VMEM((nf, tm, tf), BF16),
            pltpu.VMEM((tm * rpt, LANES), U32),
            pltpu.SMEM((1,), jnp.int32),
            pltpu.SemaphoreType.DMA(()),
            pltpu.SemaphoreType.DMA(()),
        ],
    )
    return pl.pallas_call(
        functools.partial(_moe_kernel, nf=nf, n_tiles=n_tiles, n_assign=toks.shape[0]),
        grid_spec=grid_spec,
        out_shape=jax.ShapeDtypeStruct(((toks.shape[0] + tm) * rpt, LANES), U32),
        compiler_params=_cparams(("arbitrary", "arbitrary")),
        name="moe_ffn",
    )(tile_e, tile_valid, tile_c0, toks, h_tiles, w_gate, w_up, w_down,
      b_gate.reshape(e, 1, dff), b_up.reshape(e, 1, dff), b_down.reshape(e, 1, d))


def _combine_kernel(pos_ref, x_ref, y_hbm, w_ref, g_ref, o_ref, buf_ref, sem, *, final, n_steps, tf):
    i = pl.program_id(0)
    tm, d = x_ref.shape
    rpt = SUBLANES
    slot = i % 2

    def issue(step, sl_):
        def body(pair, c):
            for u in range(2):
                tt = pair * 2 + u
                for k in range(TOP_K):
                    p = pos_ref[(step * tm + tt) * TOP_K + k]
                    src = y_hbm.at[pl.ds(pl.multiple_of(p * rpt, rpt), rpt)]
                    dst = buf_ref.at[sl_, pl.ds(pl.multiple_of((k * tm + tt) * rpt, rpt), rpt)]
                    pltpu.make_async_copy(src, dst, sem.at[sl_]).start(priority=k % 2)
            return c

        lax.fori_loop(0, tm // 2, body, 0)

    @pl.when(i == 0)
    def _():
        issue(0, 0)

    @pl.when(i + 1 < n_steps)
    def _():
        issue(i + 1, 1 - slot)

    pltpu.make_async_copy(y_hbm.at[pl.ds(0, TOP_K * tm * rpt)], buf_ref.at[slot], sem.at[slot]).wait()

    w = w_ref[...]
    wk = [jnp.broadcast_to(w[:, k:k + 1], (tm, LANES)) for k in range(TOP_K)]
    per_chunk = tf // 2 // LANES
    slabs = [None] * (d // LANES)
    for sl in range(rpt):
        n, j = sl // per_chunk, sl % per_chunk
        c_lo = (n * tf) // LANES + j
        c_hi = c_lo + per_chunk
        lo = x_ref[:, c_lo * LANES:(c_lo + 1) * LANES]
        hi = x_ref[:, c_hi * LANES:(c_hi + 1) * LANES]
        for k in range(TOP_K):
            p = buf_ref[slot, pl.ds(k * tm * rpt + sl, tm, stride=rpt), :]
            lo = lo + wk[k] * _unpack_lo(p)
            hi = hi + wk[k] * _unpack_hi(p)
        slabs[c_lo], slabs[c_hi] = lo, hi
    acc = jnp.concatenate(slabs, axis=1)
    o_ref[...] = _rms(acc, g_ref[...]) if final else acc


def _combine(pos, x2, y_tiles, top_w, g, final):
    t, d = x2.shape
    tm = COMBINE_TM
    rpt = SUBLANES
    n_steps = t // tm
    grid_spec = pltpu.PrefetchScalarGridSpec(
        num_scalar_prefetch=1,
        grid=(n_steps,),
        in_specs=[
            pl.BlockSpec((tm, d), lambda i, pos: (i, 0)),
            pl.BlockSpec(memory_space=pl.ANY),
            pl.BlockSpec((tm, TOP_K), lambda i, pos: (i, 0)),
            pl.BlockSpec((1, d), lambda i, pos: (0, 0)),
        ],
        out_specs=pl.BlockSpec((tm, d), lambda i, pos: (i, 0)),
        scratch_shapes=[
            pltpu.VMEM((2, TOP_K * tm * rpt, LANES), U32),
            pltpu.SemaphoreType.DMA((2,)),
        ],
    )
    return pl.pallas_call(
        functools.partial(_combine_kernel, final=final, n_steps=n_steps, tf=MOE_TF),
        grid_spec=grid_spec,
        out_shape=jax.ShapeDtypeStruct((t, d), F32),
        compiler_params=_cparams(("arbitrary",)),
        name="combine",
    )(pos, x2, y_tiles, top_w, g)


def _rope_tables(seq):
    half = ROT_DIM // 2
    inv = ROPE_THETA ** (-(jnp.arange(half, dtype=F32) * 2.0 / ROT_DIM))
    ang = jnp.arange(seq, dtype=F32)[:, None] * inv[None, :]
    cos, sin = jnp.cos(ang), jnp.sin(ang)
    ones = jnp.ones((seq, HEAD_DIM - ROT_DIM), F32)
    zeros = jnp.zeros((seq, HEAD_DIM - ROT_DIM), F32)
    zh = jnp.zeros((seq, half), F32)
    c = jnp.concatenate([cos, cos, ones], axis=1)
    sa = jnp.concatenate([-sin, zh, zeros], axis=1)
    sb = jnp.concatenate([zh, sin, zeros], axis=1)
    rep = LANES // HEAD_DIM
    return jnp.tile(c, (1, rep)), jnp.tile(sa, (1, rep)), jnp.tile(sb, (1, rep))


def _route(top_idx, tm):
    t = top_idx.shape[0]
    i32 = jnp.int32
    n_assign = t * TOP_K
    flat_e = top_idx.reshape(n_assign).astype(i32)
    order = jnp.argsort(flat_e).astype(i32)
    rank = jnp.argsort(order).astype(i32)
    onehot = flat_e[:, None] == jnp.arange(N_EXPERTS, dtype=i32)[None, :]
    counts = jnp.sum(onehot, axis=0, dtype=i32)
    start = jnp.cumsum(counts) - counts
    tiles_e = (counts + tm - 1) // tm
    tile_end = jnp.cumsum(tiles_e)
    tile_first = tile_end - tiles_e
    n_tiles = n_assign // tm + N_EXPERTS
    j = jnp.arange(n_tiles, dtype=i32)
    tile_e = jnp.minimum(jnp.searchsorted(tile_end, j, side="right"), N_EXPERTS - 1).astype(i32)
    local = j - tile_first[tile_e]
    tile_valid = jnp.clip(counts[tile_e] - local * tm, 0, tm).astype(i32)
    tile_c0 = jnp.where(tile_valid > 0, start[tile_e] + local * tm, 0).astype(i32)
    last_e = jnp.max(jnp.where(tile_valid > 0, tile_e, 0))
    tile_e = jnp.where(tile_valid > 0, tile_e, last_e).astype(i32)
    return order // TOP_K, rank, tile_e, tile_valid, tile_c0


def kernel(x, mem, g_mix, w_in, rpb_na, sinks, w_na_o, w_win_o, w_out, g_cross, g_mem, w_cq, w_ckv, w_co,
           g_moe, w_router, b_router, w_gate, b_gate, w_up, b_up, w_down, b_down, g_final):
    b, s, d = x.shape
    mem_len = mem.shape[1]
    t = b * s
    depth = w_in.shape[0]
    xc = x.reshape(t, d)
    mem2d = mem.reshape(b * mem_len, d)
    rope_c, rope_sa, rope_sb = _rope_tables(s)

    cols = _cols(d)
    n_attn = cols.end - cols.qa
    scale = HEAD_DIM ** -0.5
    o_qb, o_kb = 3 * NA_WIDTH, 3 * NA_WIDTH + WIN_Q_WIDTH

    for l in range(depth):
        w = w_in[l]
        w_in_l = jnp.concatenate([w[:, n_attn:], w[:, :NA_WIDTH] * scale, w[:, NA_WIDTH:o_qb],
                                  _swa_head_order(w[:, o_qb:o_kb] * scale, axis=1), w[:, o_kb:n_attn]],
                                 axis=1).astype(BF16)
        proj = _in_proj(xc, g_mix[l].reshape(1, d), w_in_l, rope_c, rope_sa, rope_sb, s)
        ya = _natten(proj, _natten_bias_table(rpb_na[l]), cols, b, s)
        yb = _swa(proj, sinks[l].astype(F32), cols, b, s)
        x1 = _merge(ya, yb, proj, xc, w_na_o[l].astype(BF16), _swa_head_order(w_win_o[l], axis=0).astype(BF16),
                    w_out[l].astype(BF16))
        memkv = _memkv(mem2d, g_mem[l].reshape(1, d), w_ckv[l].astype(BF16), mem_len)
        wr_hi = w_router[l].astype(BF16)
        wr_lo = (w_router[l] - wr_hi.astype(F32)).astype(BF16)
        x2, h3, top_w, top_idx = _cross(x1, g_cross[l].reshape(1, d), w_cq[l].astype(BF16), memkv,
                                        w_co[l].astype(BF16), g_moe[l].reshape(1, d),
                                        jnp.concatenate([wr_hi, wr_lo], axis=1), b_router[l].reshape(1, -1),
                                        s, mem_len)
        toks, pos, tile_e, tile_valid, tile_c0 = _route(top_idx, MOE_TM)
        ys = _moe_ffn(tile_e, tile_valid, tile_c0, toks, h3, w_gate[l], b_gate[l], w_up[l], b_up[l],
                      w_down[l], b_down[l])
        xc = _combine(pos, x2, ys, top_w, g_final.reshape(1, d), final=(l + 1 == depth))
    return xc.reshape(b, s, d)
```

```python
import functools
from typing import NamedTuple

import jax
import jax.numpy as jnp
from jax import lax
from jax.experimental import pallas as pl
from jax.experimental.pallas import tpu as pltpu

F32 = jnp.float32
BF16 = jnp.bfloat16
U32 = jnp.uint32

GRID_W = 64
HEAD_DIM = 64
NA_HEADS = 16
NA_ROWS = 8
NA_COLS = 16
WIN_Q_HEADS = 16
WIN_KV_HEADS = 4
WINDOW = 128
ROT_DIM = HEAD_DIM // 4
ROPE_THETA = 500000.0
CROSS_HEADS = 4
CROSS_HEAD_DIM = 128
N_EXPERTS = 32
TOP_K = 4
SWIGLU_LIMIT = 7.0
SWIGLU_ALPHA = 1.702
NORM_EPS = 1e-5
NEG_INF = -1e30

LANES = 128
SUBLANES = 8
VMEM_LIMIT = 56 * 1024 * 1024

NA_WIDTH = NA_HEADS * HEAD_DIM
WIN_Q_WIDTH = WIN_Q_HEADS * HEAD_DIM
WIN_KV_WIDTH = WIN_KV_HEADS * HEAD_DIM
CROSS_WIDTH = CROSS_HEADS * CROSS_HEAD_DIM


class _Cols(NamedTuple):
    ga: int
    gb: int
    qa: int
    ka: int
    va: int
    qb: int
    kb: int
    vb: int
    end: int


def _cols(d):
    widths = [d, d, NA_WIDTH, NA_WIDTH, NA_WIDTH, WIN_Q_WIDTH, WIN_KV_WIDTH, WIN_KV_WIDTH]
    offs = [0]
    for w in widths:
        offs.append(offs[-1] + w)
    return _Cols(*offs)


PROJ_TM = 1024
PROJ_TN = 512
NA_UNROLL = 4
MERGE_TM = 256
CROSS_TM = 512
MOE_TM = 2304
MOE_TSB = 256
MOE_TF = 256
GATHER_UNROLL = 8
MOE_BIG_BLOCK = 8
MOE_BLOCK_COPIES = 192
MOE_GATE_SPLIT = 4
COMBINE_TM = 256


def _cparams(sem):
    return pltpu.CompilerParams(dimension_semantics=sem, vmem_limit_bytes=VMEM_LIMIT)


def _rms(x, g):
    ms = jnp.mean(x * x, axis=-1, keepdims=True)
    return x * lax.rsqrt(ms + NORM_EPS) * g


def _dot(a, b):
    return jnp.dot(a, b, preferred_element_type=F32)


def _dot_nt(a, b):
    return lax.dot_general(a, b, (((1,), (1,)), ((), ())), preferred_element_type=F32)


def _pack_halves(v):
    half = v.shape[1] // 2
    lo = lax.shift_right_logical(pltpu.bitcast(v[:, :half], U32), jnp.uint32(16))
    hi = pltpu.bitcast(v[:, half:], U32) & jnp.uint32(0xFFFF0000)
    return hi | lo


def _unpack_lo(p):
    return pltpu.bitcast(lax.shift_left(p, jnp.uint32(16)), F32)


def _unpack_hi(p):
    return pltpu.bitcast(p & jnp.uint32(0xFFFF0000), F32)


def _store_token_tiles(ref, packed):
    m, w = packed.shape
    n_sl = w // LANES
    for s in range(n_sl):
        ref[pl.ds(s, m, stride=n_sl), :] = packed[:, s * LANES:(s + 1) * LANES]


def _rope_chunk(x, c, sa, sb):
    return x * c + pltpu.roll(x, LANES - ROT_DIM // 2, 1) * sa + pltpu.roll(x, ROT_DIM // 2, 1) * sb


def _inproj_kernel(x_ref, g_ref, w_ref, c_ref, sa_ref, sb_ref, o_ref, h_ref, *, cols):
    j = pl.program_id(1)
    nchunk = o_ref.shape[0]
    tn = nchunk * LANES

    @pl.when(j == 0)
    def _():
        h_ref[...] = _rms(x_ref[...], g_ref[...]).astype(BF16)

    acc = _dot(h_ref[...], w_ref[...])
    for k in range(nchunk):
        o_ref[k] = acc[:, k * LANES:(k + 1) * LANES].astype(BF16)

    j_qb, j_kb = cols.qb // tn, cols.kb // tn
    kb_chunks = WIN_KV_WIDTH // LANES

    @pl.when(j >= j_qb)
    def _():
        c, sa, sb = c_ref[...], sa_ref[...], sb_ref[...]

        def rotate(k):
            sl = slice(k * LANES, (k + 1) * LANES)
            o_ref[k] = _rope_chunk(acc[:, sl], c, sa, sb).astype(BF16)

        for k in range(nchunk):
            if k < kb_chunks:
                rotate(k)
            else:
                pl.when(j < j_kb)(functools.partial(rotate, k))


def _in_proj(x2d, g, w_bf, rope_c, rope_sa, rope_sb, seq):
    t, d = x2d.shape
    n = w_bf.shape[1]
    cols = _cols(d)
    tm, tn = PROJ_TM, PROJ_TN
    assert t % tm == 0 and n % tn == 0 and seq % tm == 0 and n == cols.end
    assert all(c % tn == 0 for c in (cols.qa, cols.ka, cols.qb, cols.kb)) and cols.end == cols.kb + tn
    sblocks = seq // tm
    rope_spec = pl.BlockSpec((tm, LANES), lambda i, j: (i % sblocks, 0))
    return pl.pallas_call(
        functools.partial(_inproj_kernel, cols=cols),
        grid=(t // tm, n // tn),
        in_specs=[
            pl.BlockSpec((tm, d), lambda i, j: (i, 0)),
            pl.BlockSpec((1, d), lambda i, j: (0, 0)),
            pl.BlockSpec((d, tn), lambda i, j: (0, j)),
            rope_spec, rope_spec, rope_spec,
        ],
        out_specs=pl.BlockSpec((tn // LANES, tm, LANES), lambda i, j: (j, i, 0)),
        out_shape=jax.ShapeDtypeStruct((n // LANES, t, LANES), BF16),
        scratch_shapes=[pltpu.VMEM((tm, d), BF16)],
        compiler_params=_cparams(("parallel", "arbitrary")),
        name="in_proj",
    )(x2d, g, w_bf, rope_c, rope_sa, rope_sb)


def _natten_kernel(q_ref, k_ref, v_ref, b_ref, o_ref, s_ref, p_ref, l_ref, *, rows):
    kr = NA_ROWS
    nu = NA_UNROLL
    lane = lax.broadcasted_iota(jnp.int32, (GRID_W, LANES), 1)
    first = lane < HEAD_DIM

    def window(r):
        rs = jnp.clip(r - kr // 2, 0, rows - kr)
        return pl.multiple_of(rs * GRID_W, GRID_W), rs - r + (NA_ROWS - 1)

    def stage_a(r, slot):
        k0, dr0 = window(r)
        q2 = q_ref[pl.ds(pl.multiple_of(r * GRID_W, GRID_W), GRID_W), :]
        k2 = k_ref[pl.ds(k0, kr * GRID_W), :]
        zero = jnp.zeros_like(q2)
        qs = jnp.concatenate([jnp.where(first, q2, zero), jnp.where(first, zero, q2)], axis=0)
        s_ref[slot] = _dot_nt(qs, k2) + b_ref[dr0]

    def stage_b(slot):
        s = s_ref[slot]
        m = jnp.max(s, axis=-1, keepdims=True)
        p = jnp.exp(s - m)
        l_ref[slot] = jnp.broadcast_to(1.0 / jnp.sum(p, axis=-1, keepdims=True), l_ref.shape[1:])
        p_ref[slot] = p.astype(BF16)

    def stage_c(r, slot):
        k0, _ = window(r)
        v2 = v_ref[pl.ds(k0, kr * GRID_W), :]
        o = _dot(p_ref[slot], v2) * l_ref[slot]
        o_ref[pl.ds(pl.multiple_of(r * GRID_W, GRID_W), GRID_W), :] = (
            jnp.where(first, o[:GRID_W], o[GRID_W:]).astype(BF16))

    def iteration(t, do_a, do_b, do_c):
        for j in range(nu):
            if do_c:
                stage_c((t - 2) * nu + j, j)
        for j in range(nu):
            if do_b:
                stage_b(j)
        for j in range(nu):
            if do_a:
                stage_a(t * nu + j, j)

    nt = rows // nu
    iteration(0, True, False, False)
    iteration(1, True, True, False)

    def body(t, carry):
        iteration(t, True, True, True)
        return carry

    lax.fori_loop(2, nt, body, 0)
    iteration(nt, False, True, True)
    iteration(nt + 1, False, False, True)


def _natten(proj, bias_tab, cols, batch, seq):
    npairs = NA_HEADS // 2
    rows = seq // GRID_W
    nu = NA_UNROLL
    assert rows >= NA_ROWS and rows % nu == 0 and rows // nu >= 2
    kq, kk, kv = cols.qa // LANES, cols.ka // LANES, cols.va // LANES
    blk = (None, seq, LANES)
    return pl.pallas_call(
        functools.partial(_natten_kernel, rows=rows),
        grid=(batch, npairs),
        in_specs=[
            pl.BlockSpec(blk, lambda b, p: (kq + p, b, 0)),
            pl.BlockSpec(blk, lambda b, p: (kk + p, b, 0)),
            pl.BlockSpec(blk, lambda b, p: (kv + p, b, 0)),
            pl.BlockSpec((None, NA_ROWS, 2 * GRID_W, NA_ROWS * GRID_W), lambda b, p: (p, 0, 0, 0)),
        ],
        out_specs=pl.BlockSpec(blk, lambda b, p: (p, b, 0)),
        out_shape=jax.ShapeDtypeStruct((npairs, batch * seq, LANES), BF16),
        scratch_shapes=[
            pltpu.VMEM((nu, 2 * GRID_W, NA_ROWS * GRID_W), F32),
            pltpu.VMEM((nu, 2 * GRID_W, NA_ROWS * GRID_W), BF16),
            pltpu.VMEM((nu, 2 * GRID_W, LANES), F32),
        ],
        compiler_params=_cparams(("parallel", "parallel")),
        name="natten",
    )(proj, proj, proj, bias_tab)


def _natten_bias_kernel(rpb_ref, o_ref):
    p = pl.program_id(0)
    n_dr, n_dc = 2 * NA_ROWS - 1, 2 * NA_COLS - 1
    c = lax.broadcasted_iota(jnp.int32, (GRID_W, LANES), 0)
    lane = lax.broadcasted_iota(jnp.int32, (GRID_W, LANES), 1)
    kc = jnp.where(lane >= GRID_W, lane - GRID_W, lane)
    dc = kc - c + (NA_COLS - 1)
    c0 = jnp.clip(c - NA_COLS // 2, 0, GRID_W - NA_COLS)
    inwin = (kc >= c0) & (kc < c0 + NA_COLS)
    for hh in range(2):
        base = (2 * p + hh) * (n_dr * n_dc)
        rows = []
        for dr in range(n_dr):
            t = jnp.zeros((GRID_W, LANES), F32)
            for j in range(n_dc):
                t = jnp.where(dc == j, rpb_ref[base + dr * n_dc + j], t)
            rows.append(jnp.where(inwin, t, NEG_INF))
        for v in range(NA_ROWS):
            for i in range(0, NA_ROWS, 2):
                o_ref[v, hh * GRID_W:(hh + 1) * GRID_W, i * GRID_W:(i + 2) * GRID_W] = (
                    jnp.where(lane < GRID_W, rows[v + i], rows[v + i + 1]))


def _natten_bias_table(rpb):
    assert 2 * GRID_W == LANES and NA_ROWS % 2 == 0
    return pl.pallas_call(
        _natten_bias_kernel,
        grid=(NA_HEADS // 2,),
        in_specs=[pl.BlockSpec(memory_space=pltpu.SMEM)],
        out_specs=pl.BlockSpec((None, NA_ROWS, 2 * GRID_W, NA_ROWS * GRID_W), lambda p: (p, 0, 0, 0)),
        out_shape=jax.ShapeDtypeStruct((NA_HEADS // 2, NA_ROWS, 2 * GRID_W, NA_ROWS * GRID_W), F32),
        compiler_params=_cparams(("parallel",)),
        name="natten_bias",
    )(rpb.astype(F32).reshape(-1))


def _swa_kernel(sink_ref, q_ref, k_ref, v_ref, mask_ref, o_ref, s_ref, p_ref, l_ref, *, nb):
    kp = pl.program_id(1)
    w = WINDOW
    grp = WIN_Q_HEADS // WIN_KV_HEADS
    lane = lax.broadcasted_iota(jnp.int32, (w, LANES), 1)
    first = lane < HEAD_DIM
    rowhalf = lax.broadcasted_iota(jnp.int32, (2 * w, 1), 0) < w

    def window(n):
        start = jnp.clip(n - 1, 0, nb - 3)
        return pl.multiple_of(start * w, w), n - start

    def stage_a(n, c):
        k0, variant = window(n)
        q2 = q_ref[c, pl.ds(pl.multiple_of(n * w, w), w), :]
        zero = jnp.zeros_like(q2)
        qs = jnp.concatenate([jnp.where(first, q2, zero), jnp.where(first, zero, q2)], axis=0)
        s_ref[c] = _dot_nt(qs, k_ref[pl.ds(k0, 3 * w), :]) + mask_ref[variant]

    def stage_b(c):
        s = s_ref[c]
        sink = jnp.where(rowhalf, sink_ref[(2 * kp) * grp + c], sink_ref[(2 * kp + 1) * grp + c])
        m = jnp.maximum(jnp.max(s, axis=-1, keepdims=True), sink)
        p = jnp.exp(s - m)
        denom = jnp.sum(p, axis=-1, keepdims=True) + jnp.exp(sink - m)
        l_ref[c] = jnp.broadcast_to(1.0 / denom, l_ref.shape[1:])
        p_ref[c] = p.astype(BF16)

    def stage_c(n, c):
        k0, _ = window(n)
        o = _dot(p_ref[c], v_ref[pl.ds(k0, 3 * w), :]) * l_ref[c]
        o_ref[c, pl.ds(pl.multiple_of(n * w, w), w), :] = (
            jnp.where(first, o[:w], o[w:]).astype(BF16))

    def iteration(n, do_a, do_b, do_c):
        for c in range(grp):
            if do_c:
                stage_c(n - 2, c)
        for c in range(grp):
            if do_b:
                stage_b(c)
        for c in range(grp):
            if do_a:
                stage_a(n, c)

    iteration(0, True, False, False)
    iteration(1, True, True, False)

    def body(n, carry):
        iteration(n, True, True, True)
        return carry

    lax.fori_loop(2, nb, body, 0)
    iteration(nb, False, True, True)
    iteration(nb + 1, False, False, True)


def _swa_mask_table():
    w = WINDOW
    qi = jnp.arange(2 * w) % w
    kj = jnp.arange(3 * w)
    off = kj[None, None, :] - w * jnp.arange(3)[:, None, None] - qi[None, :, None]
    return jnp.where(jnp.abs(off) <= w, 0.0, NEG_INF).astype(F32)


def _swa(proj, sinks, cols, batch, seq):
    w = WINDOW
    nb = seq // w
    grp = WIN_Q_HEADS // WIN_KV_HEADS
    qw = grp * LANES
    assert cols.qb % qw == 0 and nb >= 3
    cq, ck, cv = cols.qb // qw, cols.kb // LANES, cols.vb // LANES
    return pl.pallas_call(
        functools.partial(_swa_kernel, nb=nb),
        grid=(batch, WIN_KV_HEADS // 2),
        in_specs=[
            pl.BlockSpec(memory_space=pltpu.SMEM),
            pl.BlockSpec((grp, seq, LANES), lambda b, kp: (cq + kp, b, 0)),
            pl.BlockSpec((None, seq, LANES), lambda b, kp: (ck + kp, b, 0)),
            pl.BlockSpec((None, seq, LANES), lambda b, kp: (cv + kp, b, 0)),
            pl.BlockSpec((3, 2 * w, 3 * w), lambda b, kp: (0, 0, 0)),
        ],
        out_specs=pl.BlockSpec((grp, seq, LANES), lambda b, kp: (kp, b, 0)),
        out_shape=jax.ShapeDtypeStruct((WIN_Q_WIDTH // LANES, batch * seq, LANES), BF16),
        scratch_shapes=[
            pltpu.VMEM((grp, 2 * w, 3 * w), F32),
            pltpu.VMEM((grp, 2 * w, 3 * w), BF16),
            pltpu.VMEM((grp, 2 * w, LANES), F32),
        ],
        compiler_params=_cparams(("parallel", "parallel")),
        name="swa",
    )(sinks, proj, proj, proj, _swa_mask_table())


def _swa_head_order(w, axis):
    grp = WIN_Q_HEADS // WIN_KV_HEADS
    shape = w.shape
    split = shape[:axis] + (WIN_KV_HEADS // 2, 2, grp, HEAD_DIM) + shape[axis + 1:]
    return jnp.swapaxes(w.reshape(split), axis + 1, axis + 2).reshape(shape)


def _merge_kernel(ya_ref, yb_ref, ga_ref, gb_ref, x_ref, wna_ref, wwin_ref, wout_ref, o_ref):
    def rows(ref):
        return jnp.concatenate([ref[k] for k in range(ref.shape[0])], axis=1)

    a = _dot(rows(ya_ref), wna_ref[...])
    b = _dot(rows(yb_ref), wwin_ref[...])
    merged = jax.nn.sigmoid(rows(ga_ref).astype(F32)) * a + jax.nn.sigmoid(rows(gb_ref).astype(F32)) * b
    o_ref[...] = x_ref[...] + _dot(merged.astype(BF16), wout_ref[...])


def _resident(shape):
    return pl.BlockSpec(shape, lambda i: (0,) * len(shape), pipeline_mode=pl.Buffered(1))


def _merge(ya, yb, proj, x2d, wna, wwin, wout):
    t, d = x2d.shape
    tm = MERGE_TM
    return pl.pallas_call(
        _merge_kernel,
        grid=(t // tm,),
        in_specs=[
            pl.BlockSpec((NA_WIDTH // LANES, tm, LANES), lambda i: (0, i, 0)),
            pl.BlockSpec((WIN_Q_WIDTH // LANES, tm, LANES), lambda i: (0, i, 0)),
            pl.BlockSpec((d // LANES, tm, LANES), lambda i: (0, i, 0)),
            pl.BlockSpec((d // LANES, tm, LANES), lambda i: (1, i, 0)),
            pl.BlockSpec((tm, d), lambda i: (i, 0)),
            _resident(wna.shape), _resident(wwin.shape), _resident(wout.shape),
        ],
        out_specs=pl.BlockSpec((tm, d), lambda i: (i, 0)),
        out_shape=jax.ShapeDtypeStruct((t, d), F32),
        compiler_params=_cparams(("parallel",)),
        name="merge",
    )(ya, yb, proj, proj, x2d, wna, wwin, wout)


def _memkv_kernel(m_ref, g_ref, w_ref, o_ref):
    o_ref[...] = _dot(_rms(m_ref[...], g_ref[...]).astype(BF16), w_ref[...]).astype(BF16)


def _memkv(mem2d, g, w_bf, mem_len):
    t, d = mem2d.shape
    n = w_bf.shape[1]
    return pl.pallas_call(
        _memkv_kernel,
        grid=(t // mem_len,),
        in_specs=[
            pl.BlockSpec((mem_len, d), lambda i: (i, 0)),
            pl.BlockSpec((1, d), lambda i: (0, 0)),
            pl.BlockSpec((d, n), lambda i: (0, 0)),
        ],
        out_specs=pl.BlockSpec((mem_len, n), lambda i: (i, 0)),
        out_shape=jax.ShapeDtypeStruct((t, n), BF16),
        compiler_params=_cparams(("parallel",)),
        name="memkv",
    )(mem2d, g, w_bf)


def _cross_kernel(x_ref, gc_ref, wcq_ref, k_ref, v_ref, wco_ref, gm_ref, wrc_ref, br_ref,
                  x2_ref, h_ref, tw_ref, ti_ref):
    x = x_ref[...]
    q = _dot(_rms(x, gc_ref[...]).astype(BF16), wcq_ref[...]).astype(BF16)
    scale = CROSS_HEAD_DIM ** -0.5
    outs = []
    for h in range(CROSS_HEADS):
        sl = slice(h * CROSS_HEAD_DIM, (h + 1) * CROSS_HEAD_DIM)
        s = _dot_nt(q[:, sl], k_ref[:, sl]) * scale
        m = jnp.max(s, axis=-1, keepdims=True)
        p = jnp.exp(s - m)
        l = jnp.sum(p, axis=-1, keepdims=True)
        outs.append((_dot(p.astype(BF16), v_ref[:, sl]) * (1.0 / l)).astype(BF16))
    o = jnp.concatenate(outs, axis=-1)
    x2 = x + _dot(o, wco_ref[...])
    x2_ref[...] = x2
    h = _rms(x2, gm_ref[...])
    h_hi = h.astype(BF16)
    h_lo = (h - h_hi.astype(F32)).astype(BF16)
    _store_token_tiles(h_ref, _pack_halves(h_hi.astype(F32)))
    ne = br_ref.shape[1]
    hh = _dot(h_hi, wrc_ref[...])
    lg = hh[:, :ne] + hh[:, ne:] + _dot(h_lo, wrc_ref[:, :ne]) + br_ref[...]
    col = lax.broadcasted_iota(jnp.int32, lg.shape, 1)
    vals, idxs = [], []
    for _ in range(TOP_K):
        m = jnp.max(lg, axis=-1, keepdims=True)
        idx = jnp.min(jnp.where(lg == m, col, ne), axis=-1, keepdims=True)
        vals.append(m)
        idxs.append(idx)
        lg = jnp.where(col == idx, -jnp.inf, lg)
    top = jnp.concatenate(vals, axis=-1)
    e = jnp.exp(top - vals[0])
    tw_ref[...] = e / jnp.sum(e, axis=-1, keepdims=True)
    ti_ref[...] = jnp.concatenate(idxs, axis=-1)


def _cross(x1, gc, wcq, memkv, wco, gm, wr_cat, br, seq, mem_len):
    t, d = x1.shape
    tm = CROSS_TM
    ne = br.shape[1]
    per_b = seq // tm
    row = lambda i: (i, 0)
    return pl.pallas_call(
        _cross_kernel,
        grid=(t // tm,),
        in_specs=[
            pl.BlockSpec((tm, d), row),
            _resident((1, d)),
            _resident(wcq.shape),
            pl.BlockSpec((mem_len, CROSS_WIDTH), lambda i: (i // per_b, 0)),
            pl.BlockSpec((mem_len, CROSS_WIDTH), lambda i: (i // per_b, 1)),
            _resident(wco.shape),
            _resident((1, d)),
            _resident(wr_cat.shape),
            _resident((1, ne)),
        ],
        out_specs=[
            pl.BlockSpec((tm, d), row),
            pl.BlockSpec((tm * SUBLANES, LANES), row),
            pl.BlockSpec((tm, TOP_K), row),
            pl.BlockSpec((tm, TOP_K), row),
        ],
        out_shape=[
            jax.ShapeDtypeStruct((t, d), F32),
            jax.ShapeDtypeStruct((t * SUBLANES, LANES), U32),
            jax.ShapeDtypeStruct((t, TOP_K), F32),
            jax.ShapeDtypeStruct((t, TOP_K), jnp.int32),
        ],
        compiler_params=_cparams(("parallel",)),
        name="cross",
    )(x1, gc, wcq, memkv, memkv, wco, gm, wr_cat, br)


def _moe_kernel(te_ref, tv_ref, tc_ref, tok_ref, h_hbm, wg_ref, wu_ref, wd_ref, bg_ref, bu_ref, bd_ref, y_hbm,
                raw_ref, xb_ref, a_ref, ys_ref, cnt_ref, gsem, osem, *, nf, n_tiles, n_assign):
    i = pl.program_id(0)
    s = pl.program_id(1)
    tsb = MOE_TSB
    rpt = SUBLANES
    tm = xb_ref.shape[0]
    tf = wg_ref.shape[1]
    d = xb_ref.shape[1]
    half = d // 2
    n_blk = tm // tsb
    valid = tv_ref[i]
    n_sb = (valid + tsb - 1) // tsb
    nxt = jnp.minimum(i + 1, n_tiles - 1)
    next_active = (i + 1 < n_tiles) & (tv_ref[nxt] > 0)
    next_c0 = tc_ref[nxt]

    def row_copy(c0, k):
        tok = tok_ref[jnp.minimum(c0 + k, n_assign - 1)]
        src = h_hbm.at[pl.ds(pl.multiple_of(tok * rpt, rpt), rpt)]
        dst = raw_ref.at[pl.ds(pl.multiple_of(k * rpt, rpt), rpt)]
        return pltpu.make_async_copy(src, dst, gsem)

    def issue_rows(c0, k0, count):
        for u in range(count):
            row_copy(c0, k0 + u).start(priority=u % 2)

    def issue_loop(c0, k0):
        def group(gi, c):
            issue_rows(c0, k0 + gi * GATHER_UNROLL, GATHER_UNROLL)
            return c

        lax.fori_loop(0, lax.shift_right_logical(tm - k0, GATHER_UNROLL.bit_length() - 1), group, 0)

    def block_rows(ref, sb):
        return ref.at[pl.ds(pl.multiple_of(sb * (tsb * rpt), tsb * rpt), tsb * rpt)]

    def out_copy_at(first_row):
        dst = y_hbm.at[pl.ds(pl.multiple_of(first_row * rpt, rpt), tm * rpt)]
        return pltpu.make_async_copy(ys_ref, dst, osem)

    def out_copy(t):
        return out_copy_at(tc_ref[t])

    @pl.when((i == 0) & (s == 0))
    def _():
        ys_ref[...] = jnp.zeros(ys_ref.shape, U32)
        out_copy_at(n_assign).start()
        out_copy_at(n_assign).wait()

        @pl.when(valid > 0)
        def _():
            issue_loop(tc_ref[0], 0)

    @pl.when(s == 0)
    def _():
        cnt_ref[0] = 0

        @pl.when(valid > 0)
        def _():
            for sb in range(n_blk):
                pltpu.make_async_copy(block_rows(h_hbm, 0), block_rows(raw_ref, sb), gsem).wait()

        def unpack(sb, c):
            r0 = pl.multiple_of(sb * tsb, tsb)
            for sl in range(rpt):
                p = raw_ref[pl.ds(r0 * rpt + sl, tsb, stride=rpt), :]
                xb_ref[pl.ds(r0, tsb), sl * LANES:(sl + 1) * LANES] = _unpack_lo(p).astype(BF16)
                xb_ref[pl.ds(r0, tsb), half + sl * LANES:half + (sl + 1) * LANES] = _unpack_hi(p).astype(BF16)
            return c

        lax.fori_loop(0, n_sb, unpack, 0)

    def for_row_blocks(fn):
        big = MOE_BIG_BLOCK * tsb

        def big_block(b, c):
            r0 = pl.multiple_of(b * big, big)
            k0 = cnt_ref[0]
            issue = next_active & (k0 + MOE_BLOCK_COPIES <= tm)

            @pl.when(issue)
            def _():
                fn(r0, big, k0)
                cnt_ref[0] = k0 + MOE_BLOCK_COPIES

            @pl.when(jnp.logical_not(issue))
            def _():
                fn(r0, big, None)

            return c

        n_big = n_sb // MOE_BIG_BLOCK
        lax.fori_loop(0, n_big, big_block, 0)
        rem = n_sb - n_big * MOE_BIG_BLOCK

        @pl.when(rem >= 4)
        def _():
            fn(pl.multiple_of(n_big * big, 4 * tsb), 4 * tsb, None)

        def single(b, c):
            fn(pl.multiple_of(b * tsb, tsb), tsb, None)
            return c

        lax.fori_loop(n_sb - (rem & 3), n_sb, single, 0)

    @pl.when((s < nf) & (valid > 0))
    def _():
        def gate_up(r0, rows, k0):
            wg = wg_ref[...].astype(BF16)
            wu = wu_ref[...].astype(BF16)
            n_split = MOE_GATE_SPLIT
            piece = d // n_split
            per = MOE_BLOCK_COPIES // (2 * n_split)
            g = bg_ref[...]
            u = bu_ref[...]
            for j in range(n_split):
                xs = xb_ref[pl.ds(r0, rows), j * piece:(j + 1) * piece]
                if k0 is not None:
                    issue_rows(next_c0, k0 + (2 * j) * per, per)
                g = g + _dot(xs, wg[j * piece:(j + 1) * piece, :])
                if k0 is not None:
                    issue_rows(next_c0, k0 + (2 * j + 1) * per, per)
                u = u + _dot(xs, wu[j * piece:(j + 1) * piece, :])
            g = jnp.minimum(g, SWIGLU_LIMIT)
            u = jnp.clip(u, -SWIGLU_LIMIT, SWIGLU_LIMIT)
            a_ref[s, pl.ds(r0, rows), :] = ((u + 1.0) * (g * jax.nn.sigmoid(SWIGLU_ALPHA * g))).astype(BF16)

        for_row_blocks(gate_up)

    prev_active = (i > 0) & (tv_ref[jnp.maximum(i - 1, 0)] > 0)

    @pl.when((s == nf) & (valid > 0) & prev_active)
    def _():
        out_copy(i - 1).wait()

    @pl.when((valid == 0) & (s == 0) & prev_active)
    def _():
        out_copy(i - 1).wait()

    @pl.when((s >= nf) & (valid > 0))
    def _():
        n = s - nf

        def down(r0, rows, k0):
            wd = wd_ref[...].astype(BF16)
            per = MOE_BLOCK_COPIES // nf
            acc = bd_ref[...]
            for f in range(nf):
                if k0 is not None:
                    issue_rows(next_c0, k0 + f * per, per)
                acc = acc + _dot(a_ref[f, pl.ds(r0, rows), :], wd[f * tf:(f + 1) * tf, :])
            packed = _pack_halves(acc.astype(BF16).astype(F32))
            for j in range(tf // 2 // LANES):
                ys_ref[pl.ds(r0 * rpt + n * (tf // 2 // LANES) + j, rows, stride=rpt), :] = (
                    packed[:, j * LANES:(j + 1) * LANES])

        for_row_blocks(down)

    @pl.when((s == 2 * nf - 1) & (valid > 0))
    def _():
        @pl.when(next_active)
        def _():
            issue_loop(next_c0, cnt_ref[0])

        out_copy(i).start()

        @pl.when(i == n_tiles - 1)
        def _():
            out_copy(i).wait()


def _moe_ffn(tile_e, tile_valid, tile_c0, toks, h_tiles, w_gate, b_gate, w_up, b_up, w_down, b_down):
    e, d, dff = w_gate.shape
    tm, tf = MOE_TM, MOE_TF
    rpt = SUBLANES
    assert d == dff
    assert d == 2 * rpt * LANES, "a packed token row must be exactly one (8, 128) tile"
    assert tm % MOE_TSB == 0 and dff % tf == 0 and (tf // 2) % LANES == 0
    nf = dff // tf
    assert nf >= 2
    assert MOE_BLOCK_COPIES % (2 * MOE_GATE_SPLIT) == 0 and MOE_BLOCK_COPIES % nf == 0
    assert tm % MOE_BLOCK_COPIES == 0 and MOE_BLOCK_COPIES % GATHER_UNROLL == 0 and tm % GATHER_UNROLL == 0
    assert d % MOE_GATE_SPLIT == 0 and 1 <= MOE_BIG_BLOCK <= 8
    n_tiles = tile_e.shape[0]
    up = lambda i, s, te, tv, tc, tk: (te[i], 0, jnp.where(tv[i] > 0, jnp.minimum(s, nf - 1), nf - 1))
    dn = lambda i, s, te, tv, tc, tk: (te[i], 0, jnp.where(tv[i] > 0, jnp.maximum(s - nf, 0), nf - 1))
    grid_spec = pltpu.PrefetchScalarGridSpec(
        num_scalar_prefetch=4,
        grid=(n_tiles, 2 * nf),
        in_specs=[
            pl.BlockSpec(memory_space=pl.ANY),
            pl.BlockSpec((None, d, tf), up),
            pl.BlockSpec((None, d, tf), up),
            pl.BlockSpec((None, dff, tf), dn),
            pl.BlockSpec((None, 1, tf), up),
            pl.BlockSpec((None, 1, tf), up),
            pl.BlockSpec((None, 1, tf), dn),
        ],
        out_specs=pl.BlockSpec(memory_space=pl.ANY),
        scratch_shapes=[
            pltpu.VMEM((tm * rpt, LANES), U32),
            pltpu.VMEM((tm, d), BF16),
            pltpu.VMEM((nf, tm, tf), BF16),
            pltpu.VMEM((tm * rpt, LANES), U32),
            pltpu.SMEM((1,), jnp.int32),
            pltpu.SemaphoreType.DMA(()),
            pltpu.SemaphoreType.DMA(()),
        ],
    )
    return pl.pallas_call(
        functools.partial(_moe_kernel, nf=nf, n_tiles=n_tiles, n_assign=toks.shape[0]),
        grid_spec=grid_spec,
        out_shape=jax.ShapeDtypeStruct(((toks.shape[0] + tm) * rpt, LANES), U32),
        compiler_params=_cparams(("arbitrary", "arbitrary")),
        name="moe_ffn",
    )(tile_e, tile_valid, tile_c0, toks, h_tiles, w_gate, w_up, w_down,
      b_gate.reshape(e, 1, dff), b_up.reshape(e, 1, dff), b_down.reshape(e, 1, d))


def _combine_kernel(pos_ref, x_ref, y_hbm, w_ref, g_ref, o_ref, buf_ref, sem, *, final, n_steps, tf):
    i = pl.program_id(0)
    tm, d = x_ref.shape
    rpt = SUBLANES
    slot = i % 2

    def issue(step, sl_):
        def body(pair, c):
            for u in range(2):
                tt = pair * 2 + u
                for k in range(TOP_K):
                    p = pos_ref[(step * tm + tt) * TOP_K + k]
                    src = y_hbm.at[pl.ds(pl.multiple_of(p * rpt, rpt), rpt)]
                    dst = buf_ref.at[sl_, pl.ds(pl.multiple_of((k * tm + tt) * rpt, rpt), rpt)]
                    pltpu.make_async_copy(src, dst, sem.at[sl_]).start(priority=k % 2)
            return c

        lax.fori_loop(0, tm // 2, body, 0)

    @pl.when(i == 0)
    def _():
        issue(0, 0)

    @pl.when(i + 1 < n_steps)
    def _():
        issue(i + 1, 1 - slot)

    pltpu.make_async_copy(y_hbm.at[pl.ds(0, TOP_K * tm * rpt)], buf_ref.at[slot], sem.at[slot]).wait()

    w = w_ref[...]
    wk = [jnp.broadcast_to(w[:, k:k + 1], (tm, LANES)) for k in range(TOP_K)]
    per_chunk = tf // 2 // LANES
    slabs = [None] * (d // LANES)
    for sl in range(rpt):
        n, j = sl // per_chunk, sl % per_chunk
        c_lo = (n * tf) // LANES + j
        c_hi = c_lo + per_chunk
        lo = x_ref[:, c_lo * LANES:(c_lo + 1) * LANES]
        hi = x_ref[:, c_hi * LANES:(c_hi + 1) * LANES]
        for k in range(TOP_K):
            p = buf_ref[slot, pl.ds(k * tm * rpt + sl, tm, stride=rpt), :]
            lo = lo + wk[k] * _unpack_lo(p)
            hi = hi + wk[k] * _unpack_hi(p)
        slabs[c_lo], slabs[c_hi] = lo, hi
    acc = jnp.concatenate(slabs, axis=1)
    o_ref[...] = _rms(acc, g_ref[...]) if final else acc


def _combine(pos, x2, y_tiles, top_w, g, final):
    t, d = x2.shape
    tm = COMBINE_TM
    rpt = SUBLANES
    n_steps = t // tm
    grid_spec = pltpu.PrefetchScalarGridSpec(
        num_scalar_prefetch=1,
        grid=(n_steps,),
        in_specs=[
            pl.BlockSpec((tm, d), lambda i, pos: (i, 0)),
            pl.BlockSpec(memory_space=pl.ANY),
            pl.BlockSpec((tm, TOP_K), lambda i, pos: (i, 0)),
            pl.BlockSpec((1, d), lambda i, pos: (0, 0)),
        ],
        out_specs=pl.BlockSpec((tm, d), lambda i, pos: (i, 0)),
        scratch_shapes=[
            pltpu.VMEM((2, TOP_K * tm * rpt, LANES), U32),
            pltpu.SemaphoreType.DMA((2,)),
        ],
    )
    return pl.pallas_call(
        functools.partial(_combine_kernel, final=final, n_steps=n_steps, tf=MOE_TF),
        grid_spec=grid_spec,
        out_shape=jax.ShapeDtypeStruct((t, d), F32),
        compiler_params=_cparams(("arbitrary",)),
        name="combine",
    )(pos, x2, y_tiles, top_w, g)


def _rope_tables(seq):
    half = ROT_DIM // 2
    inv = ROPE_THETA ** (-(jnp.arange(half, dtype=F32) * 2.0 / ROT_DIM))
    ang = jnp.arange(seq, dtype=F32)[:, None] * inv[None, :]
    cos, sin = jnp.cos(ang), jnp.sin(ang)
    ones = jnp.ones((seq, HEAD_DIM - ROT_DIM), F32)
    zeros = jnp.zeros((seq, HEAD_DIM - ROT_DIM), F32)
    zh = jnp.zeros((seq, half), F32)
    c = jnp.concatenate([cos, cos, ones], axis=1)
    sa = jnp.concatenate([-sin, zh, zeros], axis=1)
    sb = jnp.concatenate([zh, sin, zeros], axis=1)
    rep = LANES // HEAD_DIM
    return jnp.tile(c, (1, rep)), jnp.tile(sa, (1, rep)), jnp.tile(sb, (1, rep))


def _route(top_idx, tm):
    t = top_idx.shape[0]
    i32 = jnp.int32
    n_assign = t * TOP_K
    flat_e = top_idx.reshape(n_assign).astype(i32)
    order = jnp.argsort(flat_e).astype(i32)
    rank = jnp.argsort(order).astype(i32)
    onehot = flat_e[:, None] == jnp.arange(N_EXPERTS, dtype=i32)[None, :]
    counts = jnp.sum(onehot, axis=0, dtype=i32)
    start = jnp.cumsum(counts) - counts
    tiles_e = (counts + tm - 1) // tm
    tile_end = jnp.cumsum(tiles_e)
    tile_first = tile_end - tiles_e
    n_tiles = n_assign // tm + N_EXPERTS
    j = jnp.arange(n_tiles, dtype=i32)
    tile_e = jnp.minimum(jnp.searchsorted(tile_end, j, side="right"), N_EXPERTS - 1).astype(i32)
    local = j - tile_first[tile_e]
    tile_valid = jnp.clip(counts[tile_e] - local * tm, 0, tm).astype(i32)
    tile_c0 = jnp.where(tile_valid > 0, start[tile_e] + local * tm, 0).astype(i32)
    last_e = jnp.max(jnp.where(tile_valid > 0, tile_e, 0))
    tile_e = jnp.where(tile_valid > 0, tile_e, last_e).astype(i32)
    return order // TOP_K, rank, tile_e, tile_valid, tile_c0


def kernel(x, mem, g_mix, w_in, rpb_na, sinks, w_na_o, w_win_o, w_out, g_cross, g_mem, w_cq, w_ckv, w_co,
           g_moe, w_router, b_router, w_gate, b_gate, w_up, b_up, w_down, b_down, g_final):
    b, s, d = x.shape
    mem_len = mem.shape[1]
    t = b * s
    depth = w_in.shape[0]
    xc = x.reshape(t, d)
    mem2d = mem.reshape(b * mem_len, d)
    rope_c, rope_sa, rope_sb = _rope_tables(s)

    cols = _cols(d)
    n_attn = cols.end - cols.qa
    scale = HEAD_DIM ** -0.5
    o_qb, o_kb = 3 * NA_WIDTH, 3 * NA_WIDTH + WIN_Q_WIDTH

    for l in range(depth):
        w = w_in[l]
        w_in_l = jnp.concatenate([w[:, n_attn:], w[:, :NA_WIDTH] * scale, w[:, NA_WIDTH:o_qb],
                                  _swa_head_order(w[:, o_qb:o_kb] * scale, axis=1), w[:, o_kb:n_attn]],
                                 axis=1).astype(BF16)
        proj = _in_proj(xc, g_mix[l].reshape(1, d), w_in_l, rope_c, rope_sa, rope_sb, s)
        ya = _natten(proj, _natten_bias_table(rpb_na[l]), cols, b, s)
        yb = _swa(proj, sinks[l].astype(F32), cols, b, s)
        x1 = _merge(ya, yb, proj, xc, w_na_o[l].astype(BF16), _swa_head_order(w_win_o[l], axis=0).astype(BF16),
                    w_out[l].astype(BF16))
        memkv = _memkv(mem2d, g_mem[l].reshape(1, d), w_ckv[l].astype(BF16), mem_len)
        wr_hi = w_router[l].astype(BF16)
        wr_lo = (w_router[l] - wr_hi.astype(F32)).astype(BF16)
        x2, h3, top_w, top_idx = _cross(x1, g_cross[l].reshape(1, d), w_cq[l].astype(BF16), memkv,
                                        w_co[l].astype(BF16), g_moe[l].reshape(1, d),
                                        jnp.concatenate([wr_hi, wr_lo], axis=1), b_router[l].reshape(1, -1),
                                        s, mem_len)
        toks, pos, tile_e, tile_valid, tile_c0 = _route(top_idx, MOE_TM)
        ys = _moe_ffn(tile_e, tile_valid, tile_c0, toks, h3, w_gate[l], b_gate[l], w_up[l], b_up[l],
                      w_down[l], b_down[l])
        xc = _combine(pos, x2, ys, top_w, g_final.reshape(1, d), final=(l + 1 == depth))
    return xc.reshape(b, s, d)
```

```python
import functools
from typing import NamedTuple

import jax
import jax.numpy as jnp
from jax import lax
from jax.experimental import pallas as pl
from jax.experimental.pallas import tpu as pltpu

F32 = jnp.float32
BF16 = jnp.bfloat16
U32 = jnp.uint32

GRID_W = 64
HEAD_DIM = 64
NA_HEADS = 16
NA_ROWS = 8
NA_COLS = 16
WIN_Q_HEADS = 16
WIN_KV_HEADS = 4
WINDOW = 128
ROT_DIM = HEAD_DIM // 4
ROPE_THETA = 500000.0
CROSS_HEADS = 4
CROSS_HEAD_DIM = 128
N_EXPERTS = 32
TOP_K = 4
SWIGLU_LIMIT = 7.0
SWIGLU_ALPHA = 1.702
NORM_EPS = 1e-5
NEG_INF = -1e30

LANES = 128
SUBLANES = 8
VMEM_LIMIT = 56 * 1024 * 1024

NA_WIDTH = NA_HEADS * HEAD_DIM
WIN_Q_WIDTH = WIN_Q_HEADS * HEAD_DIM
WIN_KV_WIDTH = WIN_KV_HEADS * HEAD_DIM
CROSS_WIDTH = CROSS_HEADS * CROSS_HEAD_DIM


class _Cols(NamedTuple):
    ga: int
    gb: int
    qa: int
    ka: int
    va: int
    qb: int
    kb: int
    vb: int
    end: int


def _cols(d):
    widths = [d, d, NA_WIDTH, NA_WIDTH, NA_WIDTH, WIN_Q_WIDTH, WIN_KV_WIDTH, WIN_KV_WIDTH]
    offs = [0]
    for w in widths:
        offs.append(offs[-1] + w)
    return _Cols(*offs)


PROJ_TM = 1024
PROJ_TN = 512
NA_UNROLL = 4
MERGE_TM = 256
CROSS_TM = 512
MOE_TM = 2304
MOE_TSB = 256
MOE_TF = 256
GATHER_UNROLL = 8
MOE_BIG_BLOCK = 8
MOE_BLOCK_COPIES = 192
MOE_GATE_SPLIT = 4
COMBINE_TM = 256


def _cparams(sem):
    return pltpu.CompilerParams(dimension_semantics=sem, vmem_limit_bytes=VMEM_LIMIT)


def _rms(x, g):
    ms = jnp.mean(x * x, axis=-1, keepdims=True)
    return x * lax.rsqrt(ms + NORM_EPS) * g


def _dot(a, b):
    return jnp.dot(a, b, preferred_element_type=F32)


def _dot_nt(a, b):
    return lax.dot_general(a, b, (((1,), (1,)), ((), ())), preferred_element_type=F32)


def _pack_halves(v):
    half = v.shape[1] // 2
    lo = lax.shift_right_logical(pltpu.bitcast(v[:, :half], U32), jnp.uint32(16))
    hi = pltpu.bitcast(v[:, half:], U32) & jnp.uint32(0xFFFF0000)
    return hi | lo


def _unpack_lo(p):
    return pltpu.bitcast(lax.shift_left(p, jnp.uint32(16)), F32)


def _unpack_hi(p):
    return pltpu.bitcast(p & jnp.uint32(0xFFFF0000), F32)


def _store_token_tiles(ref, packed):
    m, w = packed.shape
    n_sl = w // LANES
    for s in range(n_sl):
        ref[pl.ds(s, m, stride=n_sl), :] = packed[:, s * LANES:(s + 1) * LANES]


def _rope_chunk(x, c, sa, sb):
    return x * c + pltpu.roll(x, LANES - ROT_DIM // 2, 1) * sa + pltpu.roll(x, ROT_DIM // 2, 1) * sb


def _inproj_kernel(x_ref, g_ref, w_ref, c_ref, sa_ref, sb_ref, o_ref, h_ref, *, cols):
    j = pl.program_id(1)
    nchunk = o_ref.shape[0]
    tn = nchunk * LANES

    @pl.when(j == 0)
    def _():
        h_ref[...] = _rms(x_ref[...], g_ref[...]).astype(BF16)

    acc = _dot(h_ref[...], w_ref[...])
    for k in range(nchunk):
        o_ref[k] = acc[:, k * LANES:(k + 1) * LANES].astype(BF16)

    j_qb, j_kb = cols.qb // tn, cols.kb // tn
    kb_chunks = WIN_KV_WIDTH // LANES

    @pl.when(j >= j_qb)
    def _():
        c, sa, sb = c_ref[...], sa_ref[...], sb_ref[...]

        def rotate(k):
            sl = slice(k * LANES, (k + 1) * LANES)
            o_ref[k] = _rope_chunk(acc[:, sl], c, sa, sb).astype(BF16)

        for k in range(nchunk):
            if k < kb_chunks:
                rotate(k)
            else:
                pl.when(j < j_kb)(functools.partial(rotate, k))


def _in_proj(x2d, g, w_bf, rope_c, rope_sa, rope_sb, seq):
    t, d = x2d.shape
    n = w_bf.shape[1]
    cols = _cols(d)
    tm, tn = PROJ_TM, PROJ_TN
    assert t % tm == 0 and n % tn == 0 and seq % tm == 0 and n == cols.end
    assert all(c % tn == 0 for c in (cols.qa, cols.ka, cols.qb, cols.kb)) and cols.end == cols.kb + tn
    sblocks = seq // tm
    rope_spec = pl.BlockSpec((tm, LANES), lambda i, j: (i % sblocks, 0))
    return pl.pallas_call(
        functools.partial(_inproj_kernel, cols=cols),
        grid=(t // tm, n // tn),
        in_specs=[
            pl.BlockSpec((tm, d), lambda i, j: (i, 0)),
            pl.BlockSpec((1, d), lambda i, j: (0, 0)),
            pl.BlockSpec((d, tn), lambda i, j: (0, j)),
            rope_spec, rope_spec, rope_spec,
        ],
        out_specs=pl.BlockSpec((tn // LANES, tm, LANES), lambda i, j: (j, i, 0)),
        out_shape=jax.ShapeDtypeStruct((n // LANES, t, LANES), BF16),
        scratch_shapes=[pltpu.VMEM((tm, d), BF16)],
        compiler_params=_cparams(("parallel", "arbitrary")),
        name="in_proj",
    )(x2d, g, w_bf, rope_c, rope_sa, rope_sb)


def _natten_kernel(q_ref, k_ref, v_ref, b_ref, o_ref, s_ref, p_ref, l_ref, *, rows):
    kr = NA_ROWS
    nu = NA_UNROLL
    lane = lax.broadcasted_iota(jnp.int32, (GRID_W, LANES), 1)
    first = lane < HEAD_DIM

    def window(r):
        rs = jnp.clip(r - kr // 2, 0, rows - kr)
        return pl.multiple_of(rs * GRID_W, GRID_W), rs - r + (NA_ROWS - 1)

    def stage_a(r, slot):
        k0, dr0 = window(r)
        q2 = q_ref[pl.ds(pl.multiple_of(r * GRID_W, GRID_W), GRID_W), :]
        k2 = k_ref[pl.ds(k0, kr * GRID_W), :]
        zero = jnp.zeros_like(q2)
        qs = jnp.concatenate([jnp.where(first, q2, zero), jnp.where(first, zero, q2)], axis=0)
        s_ref[slot] = _dot_nt(qs, k2) + b_ref[dr0]

    def stage_b(slot):
        s = s_ref[slot]
        m = jnp.max(s, axis=-1, keepdims=True)
        p = jnp.exp(s - m)
        l_ref[slot] = jnp.broadcast_to(1.0 / jnp.sum(p, axis=-1, keepdims=True), l_ref.shape[1:])
        p_ref[slot] = p.astype(BF16)

    def stage_c(r, slot):
        k0, _ = window(r)
        v2 = v_ref[pl.ds(k0, kr * GRID_W), :]
        o = _dot(p_ref[slot], v2) * l_ref[slot]
        o_ref[pl.ds(pl.multiple_of(r * GRID_W, GRID_W), GRID_W), :] = (
            jnp.where(first, o[:GRID_W], o[GRID_W:]).astype(BF16))

    def iteration(t, do_a, do_b, do_c):
        for j in range(nu):
            if do_c:
                stage_c((t - 2) * nu + j, j)
        for j in range(nu):
            if do_b:
                stage_b(j)
        for j in range(nu):
            if do_a:
                stage_a(t * nu + j, j)

    nt = rows // nu
    iteration(0, True, False, False)
    iteration(1, True, True, False)

    def body(t, carry):
        iteration(t, True, True, True)
        return carry

    lax.fori_loop(2, nt, body, 0)
    iteration(nt, False, True, True)
    iteration(nt + 1, False, False, True)


def _natten(proj, bias_tab, cols, batch, seq):
    npairs = NA_HEADS // 2
    rows = seq // GRID_W
    nu = NA_UNROLL
    assert rows >= NA_ROWS and rows % nu == 0 and rows // nu >= 2
    kq, kk, kv = cols.qa // LANES, cols.ka // LANES, cols.va // LANES
    blk = (None, seq, LANES)
    return pl.pallas_call(
        functools.partial(_natten_kernel, rows=rows),
        grid=(batch, npairs),
        in_specs=[
            pl.BlockSpec(blk, lambda b, p: (kq + p, b, 0)),
            pl.BlockSpec(blk, lambda b, p: (kk + p, b, 0)),
            pl.BlockSpec(blk, lambda b, p: (kv + p, b, 0)),
            pl.BlockSpec((None, NA_ROWS, 2 * GRID_W, NA_ROWS * GRID_W), lambda b, p: (p, 0, 0, 0)),
        ],
        out_specs=pl.BlockSpec(blk, lambda b, p: (p, b, 0)),
        out_shape=jax.ShapeDtypeStruct((npairs, batch * seq, LANES), BF16),
        scratch_shapes=[
            pltpu.VMEM((nu, 2 * GRID_W, NA_ROWS * GRID_W), F32),
            pltpu.VMEM((nu, 2 * GRID_W, NA_ROWS * GRID_W), BF16),
            pltpu.VMEM((nu, 2 * GRID_W, LANES), F32),
        ],
        compiler_params=_cparams(("parallel", "parallel")),
        name="natten",
    )(proj, proj, proj, bias_tab)


def _natten_bias_kernel(rpb_ref, o_ref):
    p = pl.program_id(0)
    n_dr, n_dc = 2 * NA_ROWS - 1, 2 * NA_COLS - 1
    c = lax.broadcasted_iota(jnp.int32, (GRID_W, LANES), 0)
    lane = lax.broadcasted_iota(jnp.int32, (GRID_W, LANES), 1)
    kc = jnp.where(lane >= GRID_W, lane - GRID_W, lane)
    dc = kc - c + (NA_COLS - 1)
    c0 = jnp.clip(c - NA_COLS // 2, 0, GRID_W - NA_COLS)
    inwin = (kc >= c0) & (kc < c0 + NA_COLS)
    for hh in range(2):
        base = (2 * p + hh) * (n_dr * n_dc)
        rows = []
        for dr in range(n_dr):
            t = jnp.zeros((GRID_W, LANES), F32)
            for j in range(n_dc):
                t = jnp.where(dc == j, rpb_ref[base + dr * n_dc + j], t)
            rows.append(jnp.where(inwin, t, NEG_INF))
        for v in range(NA_ROWS):
            for i in range(0, NA_ROWS, 2):
                o_ref[v, hh * GRID_W:(hh + 1) * GRID_W, i * GRID_W:(i + 2) * GRID_W] = (
                    jnp.where(lane < GRID_W, rows[v + i], rows[v + i + 1]))


def _natten_bias_table(rpb):
    assert 2 * GRID_W == LANES and NA_ROWS % 2 == 0
    return pl.pallas_call(
        _natten_bias_kernel,
        grid=(NA_HEADS // 2,),
        in_specs=[pl.BlockSpec(memory_space=pltpu.SMEM)],
        out_specs=pl.BlockSpec((None, NA_ROWS, 2 * GRID_W, NA_ROWS * GRID_W), lambda p: (p, 0, 0, 0)),
        out_shape=jax.ShapeDtypeStruct((NA_HEADS // 2, NA_ROWS, 2 * GRID_W, NA_ROWS * GRID_W), F32),
        compiler_params=_cparams(("parallel",)),
        name="natten_bias",
    )(rpb.astype(F32).reshape(-1))


def _swa_kernel(sink_ref, q_ref, k_ref, v_ref, mask_ref, o_ref, s_ref, p_ref, l_ref, *, nb):
    kp = pl.program_id(1)
    w = WINDOW
    grp = WIN_Q_HEADS // WIN_KV_HEADS
    lane = lax.broadcasted_iota(jnp.int32, (w, LANES), 1)
    first = lane < HEAD_DIM
    rowhalf = lax.broadcasted_iota(jnp.int32, (2 * w, 1), 0) < w

    def window(n):
        start = jnp.clip(n - 1, 0, nb - 3)
        return pl.multiple_of(start * w, w), n - start

    def stage_a(n, c):
        k0, variant = window(n)
        q2 = q_ref[c, pl.ds(pl.multiple_of(n * w, w), w), :]
        zero = jnp.zeros_like(q2)
        qs = jnp.concatenate([jnp.where(first, q2, zero), jnp.where(first, zero, q2)], axis=0)
        s_ref[c] = _dot_nt(qs, k_ref[pl.ds(k0, 3 * w), :]) + mask_ref[variant]

    def stage_b(c):
        s = s_ref[c]
        sink = jnp.where(rowhalf, sink_ref[(2 * kp) * grp + c], sink_ref[(2 * kp + 1) * grp + c])
        m = jnp.maximum(jnp.max(s, axis=-1, keepdims=True), sink)
        p = jnp.exp(s - m)
        denom = jnp.sum(p, axis=-1, keepdims=True) + jnp.exp(sink - m)
        l_ref[c] = jnp.broadcast_to(1.0 / denom, l_ref.shape[1:])
        p_ref[c] = p.astype(BF16)

    def stage_c(n, c):
        k0, _ = window(n)
        o = _dot(p_ref[c], v_ref[pl.ds(k0, 3 * w), :]) * l_ref[c]
        o_ref[c, pl.ds(pl.multiple_of(n * w, w), w), :] = (
            jnp.where(first, o[:w], o[w:]).astype(BF16))

    def iteration(n, do_a, do_b, do_c):
        for c in range(grp):
            if do_c:
                stage_c(n - 2, c)
        for c in range(grp):
            if do_b:
                stage_b(c)
        for c in range(grp):
            if do_a:
                stage_a(n, c)

    iteration(0, True, False, False)
    iteration(1, True, True, False)

    def body(n, carry):
        iteration(n, True, True, True)
        return carry

    lax.fori_loop(2, nb, body, 0)
    iteration(nb, False, True, True)
    iteration(nb + 1, False, False, True)


def _swa_mask_table():
    w = WINDOW
    qi = jnp.arange(2 * w) % w
    kj = jnp.arange(3 * w)
    off = kj[None, None, :] - w * jnp.arange(3)[:, None, None] - qi[None, :, None]
    return jnp.where(jnp.abs(off) <= w, 0.0, NEG_INF).astype(F32)


def _swa(proj, sinks, cols, batch, seq):
    w = WINDOW
    nb = seq // w
    grp = WIN_Q_HEADS // WIN_KV_HEADS
    qw = grp * LANES
    assert cols.qb % qw == 0 and nb >= 3
    cq, ck, cv = cols.qb // qw, cols.kb // LANES, cols.vb // LANES
    return pl.pallas_call(
        functools.partial(_swa_kernel, nb=nb),
        grid=(batch, WIN_KV_HEADS // 2),
        in_specs=[
            pl.BlockSpec(memory_space=pltpu.SMEM),
            pl.BlockSpec((grp, seq, LANES), lambda b, kp: (cq + kp, b, 0)),
            pl.BlockSpec((None, seq, LANES), lambda b, kp: (ck + kp, b, 0)),
            pl.BlockSpec((None, seq, LANES), lambda b, kp: (cv + kp, b, 0)),
            pl.BlockSpec((3, 2 * w, 3 * w), lambda b, kp: (0, 0, 0)),
        ],
        out_specs=pl.BlockSpec((grp, seq, LANES), lambda b, kp: (kp, b, 0)),
        out_shape=jax.ShapeDtypeStruct((WIN_Q_WIDTH // LANES, batch * seq, LANES), BF16),
        scratch_shapes=[
            pltpu.VMEM((grp, 2 * w, 3 * w), F32),
            pltpu.VMEM((grp, 2 * w, 3 * w), BF16),
            pltpu.VMEM((grp, 2 * w, LANES), F32),
        ],
        compiler_params=_cparams(("parallel", "parallel")),
        name="swa",
    )(sinks, proj, proj, proj, _swa_mask_table())


def _swa_head_order(w, axis):
    grp = WIN_Q_HEADS // WIN_KV_HEADS
    shape = w.shape
    split = shape[:axis] + (WIN_KV_HEADS // 2, 2, grp, HEAD_DIM) + shape[axis + 1:]
    return jnp.swapaxes(w.reshape(split), axis + 1, axis + 2).reshape(shape)


def _merge_kernel(ya_ref, yb_ref, ga_ref, gb_ref, x_ref, wna_ref, wwin_ref, wout_ref, o_ref):
    def rows(ref):
        return jnp.concatenate([ref[k] for k in range(ref.shape[0])], axis=1)

    a = _dot(rows(ya_ref), wna_ref[...])
    b = _dot(rows(yb_ref), wwin_ref[...])
    merged = jax.nn.sigmoid(rows(ga_ref).astype(F32)) * a + jax.nn.sigmoid(rows(gb_ref).astype(F32)) * b
    o_ref[...] = x_ref[...] + _dot(merged.astype(BF16), wout_ref[...])


def _resident(shape):
    return pl.BlockSpec(shape, lambda i: (0,) * len(shape), pipeline_mode=pl.Buffered(1))


def _merge(ya, yb, proj, x2d, wna, wwin, wout):
    t, d = x2d.shape
    tm = MERGE_TM
    return pl.pallas_call(
        _merge_kernel,
        grid=(t // tm,),
        in_specs=[
            pl.BlockSpec((NA_WIDTH // LANES, tm, LANES), lambda i: (0, i, 0)),
            pl.BlockSpec((WIN_Q_WIDTH // LANES, tm, LANES), lambda i: (0, i, 0)),
            pl.BlockSpec((d // LANES, tm, LANES), lambda i: (0, i, 0)),
            pl.BlockSpec((d // LANES, tm, LANES), lambda i: (1, i, 0)),
            pl.BlockSpec((tm, d), lambda i: (i, 0)),
            _resident(wna.shape), _resident(wwin.shape), _resident(wout.shape),
        ],
        out_specs=pl.BlockSpec((tm, d), lambda i: (i, 0)),
        out_shape=jax.ShapeDtypeStruct((t, d), F32),
        compiler_params=_cparams(("parallel",)),
        name="merge",
    )(ya, yb, proj, proj, x2d, wna, wwin, wout)


def _memkv_kernel(m_ref, g_ref, w_ref, o_ref):
    o_ref[...] = _dot(_rms(m_ref[...], g_ref[...]).astype(BF16), w_ref[...]).astype(BF16)


def _memkv(mem2d, g, w_bf, mem_len):
    t, d = mem2d.shape
    n = w_bf.shape[1]
    return pl.pallas_call(
        _memkv_kernel,
        grid=(t // mem_len,),
        in_specs=[
            pl.BlockSpec((mem_len, d), lambda i: (i, 0)),
            pl.BlockSpec((1, d), lambda i: (0, 0)),
            pl.BlockSpec((d, n), lambda i: (0, 0)),
        ],
        out_specs=pl.BlockSpec((mem_len, n), lambda i: (i, 0)),
        out_shape=jax.ShapeDtypeStruct((t, n), BF16),
        compiler_params=_cparams(("parallel",)),
        name="memkv",
    )(mem2d, g, w_bf)


def _cross_kernel(x_ref, gc_ref, wcq_ref, k_ref, v_ref, wco_ref, gm_ref, wrc_ref, br_ref,
                  x2_ref, h_ref, tw_ref, ti_ref):
    x = x_ref[...]
    q = _dot(_rms(x, gc_ref[...]).astype(BF16), wcq_ref[...]).astype(BF16)
    scale = CROSS_HEAD_DIM ** -0.5
    outs = []
    for h in range(CROSS_HEADS):
        sl = slice(h * CROSS_HEAD_DIM, (h + 1) * CROSS_HEAD_DIM)
        s = _dot_nt(q[:, sl], k_ref[:, sl]) * scale
        m = jnp.max(s, axis=-1, keepdims=True)
        p = jnp.exp(s - m)
        l = jnp.sum(p, axis=-1, keepdims=True)
        outs.append((_dot(p.astype(BF16), v_ref[:, sl]) * (1.0 / l)).astype(BF16))
    o = jnp.concatenate(outs, axis=-1)
    x2 = x + _dot(o, wco_ref[...])
    x2_ref[...] = x2
    h = _rms(x2, gm_ref[...])
    h_hi = h.astype(BF16)
    h_lo = (h - h_hi.astype(F32)).astype(BF16)
    _store_token_tiles(h_ref, _pack_halves(h_hi.astype(F32)))
    ne = br_ref.shape[1]
    hh = _dot(h_hi, wrc_ref[...])
    lg = hh[:, :ne] + hh[:, ne:] + _dot(h_lo, wrc_ref[:, :ne]) + br_ref[...]
    col = lax.broadcasted_iota(jnp.int32, lg.shape, 1)
    vals, idxs = [], []
    for _ in range(TOP_K):
        m = jnp.max(lg, axis=-1, keepdims=True)
        idx = jnp.min(jnp.where(lg == m, col, ne), axis=-1, keepdims=True)
        vals.append(m)
        idxs.append(idx)
        lg = jnp.where(col == idx, -jnp.inf, lg)
    top = jnp.concatenate(vals, axis=-1)
    e = jnp.exp(top - vals[0])
    tw_ref[...] = e / jnp.sum(e, axis=-1, keepdims=True)
    ti_ref[...] = jnp.concatenate(idxs, axis=-1)


def _cross(x1, gc, wcq, memkv, wco, gm, wr_cat, br, seq, mem_len):
    t, d = x1.shape
    tm = CROSS_TM
    ne = br.shape[1]
    per_b = seq // tm
    row = lambda i: (i, 0)
    return pl.pallas_call(
        _cross_kernel,
        grid=(t // tm,),
        in_specs=[
            pl.BlockSpec((tm, d), row),
            _resident((1, d)),
            _resident(wcq.shape),
            pl.BlockSpec((mem_len, CROSS_WIDTH), lambda i: (i // per_b, 0)),
            pl.BlockSpec((mem_len, CROSS_WIDTH), lambda i: (i // per_b, 1)),
            _resident(wco.shape),
            _resident((1, d)),
            _resident(wr_cat.shape),
            _resident((1, ne)),
        ],
        out_specs=[
            pl.BlockSpec((tm, d), row),
            pl.BlockSpec((tm * SUBLANES, LANES), row),
            pl.BlockSpec((tm, TOP_K), row),
            pl.BlockSpec((tm, TOP_K), row),
        ],
        out_shape=[
            jax.ShapeDtypeStruct((t, d), F32),
            jax.ShapeDtypeStruct((t * SUBLANES, LANES), U32),
            jax.ShapeDtypeStruct((t, TOP_K), F32),
            jax.ShapeDtypeStruct((t, TOP_K), jnp.int32),
        ],
        compiler_params=_cparams(("parallel",)),
        name="cross",
    )(x1, gc, wcq, memkv, memkv, wco, gm, wr_cat, br)


def _moe_kernel(te_ref, tv_ref, tc_ref, tok_ref, h_hbm, wg_ref, wu_ref, wd_ref, bg_ref, bu_ref, bd_ref, y_hbm,
                raw_ref, xb_ref, a_ref, ys_ref, cnt_ref, gsem, osem, *, nf, n_tiles, n_assign):
    i = pl.program_id(0)
    s = pl.program_id(1)
    tsb = MOE_TSB
    rpt = SUBLANES
    tm = xb_ref.shape[0]
    tf = wg_ref.shape[1]
    d = xb_ref.shape[1]
    half = d // 2
    n_blk = tm // tsb
    valid = tv_ref[i]
    n_sb = (valid + tsb - 1) // tsb
    nxt = jnp.minimum(i + 1, n_tiles - 1)
    next_active = (i + 1 < n_tiles) & (tv_ref[nxt] > 0)
    next_c0 = tc_ref[nxt]

    def row_copy(c0, k):
        tok = tok_ref[c0 + k]
        src = h_hbm.at[pl.ds(pl.multiple_of(tok * rpt, rpt), rpt)]
        dst = raw_ref.at[pl.ds(pl.multiple_of(k * rpt, rpt), rpt)]
        return pltpu.make_async_copy(src, dst, gsem)

    def issue_rows(c0, k0, count):
        for u in range(count):
            row_copy(c0, k0 + u).start(priority=u % 2)

    def issue_loop(c0, k0):
        def group(gi, c):
            issue_rows(c0, k0 + gi * GATHER_UNROLL, GATHER_UNROLL)
            return c

        lax.fori_loop(0, lax.shift_right_logical(tm - k0, GATHER_UNROLL.bit_length() - 1), group, 0)

    def block_rows(ref, sb):
        return ref.at[pl.ds(pl.multiple_of(sb * (tsb * rpt), tsb * rpt), tsb * rpt)]

    def out_copy_at(first_row):
        dst = y_hbm.at[pl.ds(pl.multiple_of(first_row * rpt, rpt), tm * rpt)]
        return pltpu.make_async_copy(ys_ref, dst, osem)

    def out_copy(t):
        return out_copy_at(tc_ref[t])

    @pl.when((i == 0) & (s == 0))
    def _():
        ys_ref[...] = jnp.zeros(ys_ref.shape, U32)
        out_copy_at(n_assign).start()
        out_copy_at(n_assign).wait()

        @pl.when(valid > 0)
        def _():
            issue_loop(tc_ref[0], 0)

    @pl.when(s == 0)
    def _():
        cnt_ref[0] = 0

        @pl.when(valid > 0)
        def _():
            for sb in range(n_blk):
                pltpu.make_async_copy(block_rows(h_hbm, 0), block_rows(raw_ref, sb), gsem).wait()

        def unpack(sb, c):
            r0 = pl.multiple_of(sb * tsb, tsb)
            for sl in range(rpt):
                p = raw_ref[pl.ds(r0 * rpt + sl, tsb, stride=rpt), :]
                xb_ref[pl.ds(r0, tsb), sl * LANES:(sl + 1) * LANES] = _unpack_lo(p).astype(BF16)
                xb_ref[pl.ds(r0, tsb), half + sl * LANES:half + (sl + 1) * LANES] = _unpack_hi(p).astype(BF16)
            return c

        lax.fori_loop(0, n_sb, unpack, 0)

    def for_row_blocks(fn):
        big = MOE_BIG_BLOCK * tsb

        def big_block(b, c):
            r0 = pl.multiple_of(b * big, big)
            k0 = cnt_ref[0]
            issue = next_active & (k0 + MOE_BLOCK_COPIES <= tm)

            @pl.when(issue)
            def _():
                fn(r0, big, k0)
                cnt_ref[0] = k0 + MOE_BLOCK_COPIES

            @pl.when(jnp.logical_not(issue))
            def _():
                fn(r0, big, None)

            return c

        n_big = n_sb // MOE_BIG_BLOCK
        lax.fori_loop(0, n_big, big_block, 0)
        rem = n_sb - n_big * MOE_BIG_BLOCK

        @pl.when(rem >= 4)
        def _():
            fn(pl.multiple_of(n_big * big, 4 * tsb), 4 * tsb, None)

        def single(b, c):
            fn(pl.multiple_of(b * tsb, tsb), tsb, None)
            return c

        lax.fori_loop(n_sb - (rem & 3), n_sb, single, 0)

    @pl.when((s < nf) & (valid > 0))
    def _():
        def gate_up(r0, rows, k0):
            wg = wg_ref[...].astype(BF16)
            wu = wu_ref[...].astype(BF16)
            n_split = MOE_GATE_SPLIT
            piece = d // n_split
            per = MOE_BLOCK_COPIES // (2 * n_split)
            g = bg_ref[...]
            u = bu_ref[...]
            for j in range(n_split):
                xs = xb_ref[pl.ds(r0, rows), j * piece:(j + 1) * piece]
                if k0 is not None:
                    issue_rows(next_c0, k0 + (2 * j) * per, per)
                g = g + _dot(xs, wg[j * piece:(j + 1) * piece, :])
                if k0 is not None:
                    issue_rows(next_c0, k0 + (2 * j + 1) * per, per)
                u = u + _dot(xs, wu[j * piece:(j + 1) * piece, :])
            g = jnp.minimum(g, SWIGLU_LIMIT)
            u = jnp.clip(u, -SWIGLU_LIMIT, SWIGLU_LIMIT)
            a_ref[s, pl.ds(r0, rows), :] = ((u + 1.0) * (g * jax.nn.sigmoid(SWIGLU_ALPHA * g))).astype(BF16)

        for_row_blocks(gate_up)

    prev_active = (i > 0) & (tv_ref[jnp.maximum(i - 1, 0)] > 0)

    @pl.when((s == nf) & (valid > 0) & prev_active)
    def _():
        out_copy(i - 1).wait()

    @pl.when((valid == 0) & (s == 0) & prev_active)
    def _():
        out_copy(i - 1).wait()

    @pl.when((s >= nf) & (valid > 0))
    def _():
        n = s - nf

        def down(r0, rows, k0):
            wd = wd_ref[...].astype(BF16)
            per = MOE_BLOCK_COPIES // nf
            acc = bd_ref[...]
            for f in range(nf):
                if k0 is not None:
                    issue_rows(next_c0, k0 + f * per, per)
                acc = acc + _dot(a_ref[f, pl.ds(r0, rows), :], wd[f * tf:(f + 1) * tf, :])
            packed = _pack_halves(acc.astype(BF16).astype(F32))
            for j in range(tf // 2 // LANES):
                ys_ref[pl.ds(r0 * rpt + n * (tf // 2 // LANES) + j, rows, stride=rpt), :] = (
                    packed[:, j * LANES:(j + 1) * LANES])

        for_row_blocks(down)

    @pl.when((s == 2 * nf - 1) & (valid > 0))
    def _():
        @pl.when(next_active)
        def _():
            issue_loop(next_c0, cnt_ref[0])

        out_copy(i).start()

        @pl.when(i == n_tiles - 1)
        def _():
            out_copy(i).wait()


def _moe_ffn(tile_e, tile_valid, tile_c0, toks, h_tiles, w_gate, b_gate, w_up, b_up, w_down, b_down):
    e, d, dff = w_gate.shape
    tm, tf = MOE_TM, MOE_TF
    rpt = SUBLANES
    assert d == dff
    assert d == 2 * rpt * LANES, "a packed token row must be exactly one (8, 128) tile"
    assert tm % MOE_TSB == 0 and dff % tf == 0 and (tf // 2) % LANES == 0
    nf = dff // tf
    assert nf >= 2
    assert MOE_BLOCK_COPIES % (2 * MOE_GATE_SPLIT) == 0 and MOE_BLOCK_COPIES % nf == 0
    assert tm % MOE_BLOCK_COPIES == 0 and MOE_BLOCK_COPIES % GATHER_UNROLL == 0 and tm % GATHER_UNROLL == 0
    assert d % MOE_GATE_SPLIT == 0 and 1 <= MOE_BIG_BLOCK <= 8
    n_tiles = tile_e.shape[0]
    up = lambda i, s, te, tv, tc, tk: (te[i], 0, jnp.where(tv[i] > 0, jnp.minimum(s, nf - 1), nf - 1))
    dn = lambda i, s, te, tv, tc, tk: (te[i], 0, jnp.where(tv[i] > 0, jnp.maximum(s - nf, 0), nf - 1))
    grid_spec = pltpu.PrefetchScalarGridSpec(
        num_scalar_prefetch=4,
        grid=(n_tiles, 2 * nf),
        in_specs=[
            pl.BlockSpec(memory_space=pl.ANY),
            pl.BlockSpec((None, d, tf), up),
            pl.BlockSpec((None, d, tf), up),
            pl.BlockSpec((None, dff, tf), dn),
            pl.BlockSpec((None, 1, tf), up),
            pl.BlockSpec((None, 1, tf), up),
            pl.BlockSpec((None, 1, tf), dn),
        ],
        out_specs=pl.BlockSpec(memory_space=pl.ANY),
        scratch_shapes=[
            pltpu.VMEM((tm * rpt, LANES), U32),
            pltpu.VMEM((tm, d), BF16),
            pltpu.VMEM((nf, tm, tf), BF16),
            pltpu.VMEM((tm * rpt, LANES), U32),
            pltpu.SMEM((1,), jnp.int32),
            pltpu.SemaphoreType.DMA(()),
            pltpu.SemaphoreType.DMA(()),
        ],
    )
    return pl.pallas_call(
        functools.partial(_moe_kernel, nf=nf, n_tiles=n_tiles, n_assign=toks.shape[0]),
        grid_spec=grid_spec,
        out_shape=jax.ShapeDtypeStruct(((toks.shape[0] + tm) * rpt, LANES), U32),
        compiler_params=_cparams(("arbitrary", "arbitrary")),
        name="moe_ffn",
    )(tile_e, tile_valid, tile_c0, jnp.concatenate([toks, jnp.zeros((tm,), toks.dtype)]), h_tiles, w_gate, w_up, w_down,
      b_gate.reshape(e, 1, dff), b_up.reshape(e, 1, dff), b_down.reshape(e, 1, d))


def _combine_kernel(pos_ref, x_ref, y_hbm, w_ref, g_ref, o_ref, buf_ref, sem, *, final, n_steps, tf):
    i = pl.program_id(0)
    tm, d = x_ref.shape
    rpt = SUBLANES
    slot = i % 2

    def issue(step, sl_):
        def body(pair, c):
            for u in range(2):
                tt = pair * 2 + u
                for k in range(TOP_K):
                    p = pos_ref[(step * tm + tt) * TOP_K + k]
                    src = y_hbm.at[pl.ds(pl.multiple_of(p * rpt, rpt), rpt)]
                    dst = buf_ref.at[sl_, pl.ds(pl.multiple_of((k * tm + tt) * rpt, rpt), rpt)]
                    pltpu.make_async_copy(src, dst, sem.at[sl_]).start(priority=k % 2)
            return c

        lax.fori_loop(0, tm // 2, body, 0)

    @pl.when(i == 0)
    def _():
        issue(0, 0)

    @pl.when(i + 1 < n_steps)
    def _():
        issue(i + 1, 1 - slot)

    pltpu.make_async_copy(y_hbm.at[pl.ds(0, TOP_K * tm * rpt)], buf_ref.at[slot], sem.at[slot]).wait()

    w = w_ref[...]
    wk = [jnp.broadcast_to(w[:, k:k + 1], (tm, LANES)) for k in range(TOP_K)]
    per_chunk = tf // 2 // LANES
    slabs = [None] * (d // LANES)
    for sl in range(rpt):
        n, j = sl // per_chunk, sl % per_chunk
        c_lo = (n * tf) // LANES + j
        c_hi = c_lo + per_chunk
        lo = x_ref[:, c_lo * LANES:(c_lo + 1) * LANES]
        hi = x_ref[:, c_hi * LANES:(c_hi + 1) * LANES]
        for k in range(TOP_K):
            p = buf_ref[slot, pl.ds(k * tm * rpt + sl, tm, stride=rpt), :]
            lo = lo + wk[k] * _unpack_lo(p)
            hi = hi + wk[k] * _unpack_hi(p)
        slabs[c_lo], slabs[c_hi] = lo, hi
    acc = jnp.concatenate(slabs, axis=1)
    o_ref[...] = _rms(acc, g_ref[...]) if final else acc


def _combine(pos, x2, y_tiles, top_w, g, final):
    t, d = x2.shape
    tm = COMBINE_TM
    rpt = SUBLANES
    n_steps = t // tm
    grid_spec = pltpu.PrefetchScalarGridSpec(
        num_scalar_prefetch=1,
        grid=(n_steps,),
        in_specs=[
            pl.BlockSpec((tm, d), lambda i, pos: (i, 0)),
            pl.BlockSpec(memory_space=pl.ANY),
            pl.BlockSpec((tm, TOP_K), lambda i, pos: (i, 0)),
            pl.BlockSpec((1, d), lambda i, pos: (0, 0)),
        ],
        out_specs=pl.BlockSpec((tm, d), lambda i, pos: (i, 0)),
        scratch_shapes=[
            pltpu.VMEM((2, TOP_K * tm * rpt, LANES), U32),
            pltpu.SemaphoreType.DMA((2,)),
        ],
    )
    return pl.pallas_call(
        functools.partial(_combine_kernel, final=final, n_steps=n_steps, tf=MOE_TF),
        grid_spec=grid_spec,
        out_shape=jax.ShapeDtypeStruct((t, d), F32),
        compiler_params=_cparams(("arbitrary",)),
        name="combine",
    )(pos, x2, y_tiles, top_w, g)


def _rope_tables(seq):
    half = ROT_DIM // 2
    inv = ROPE_THETA ** (-(jnp.arange(half, dtype=F32) * 2.0 / ROT_DIM))
    ang = jnp.arange(seq, dtype=F32)[:, None] * inv[None, :]
    cos, sin = jnp.cos(ang), jnp.sin(ang)
    ones = jnp.ones((seq, HEAD_DIM - ROT_DIM), F32)
    zeros = jnp.zeros((seq, HEAD_DIM - ROT_DIM), F32)
    zh = jnp.zeros((seq, half), F32)
    c = jnp.concatenate([cos, cos, ones], axis=1)
    sa = jnp.concatenate([-sin, zh, zeros], axis=1)
    sb = jnp.concatenate([zh, sin, zeros], axis=1)
    rep = LANES // HEAD_DIM
    return jnp.tile(c, (1, rep)), jnp.tile(sa, (1, rep)), jnp.tile(sb, (1, rep))


def _route(top_idx, tm):
    t = top_idx.shape[0]
    i32 = jnp.int32
    n_assign = t * TOP_K
    flat_e = top_idx.reshape(n_assign).astype(i32)
    order = jnp.argsort(flat_e).astype(i32)
    rank = jnp.argsort(order).astype(i32)
    onehot = flat_e[:, None] == jnp.arange(N_EXPERTS, dtype=i32)[None, :]
    counts = jnp.sum(onehot, axis=0, dtype=i32)
    start = jnp.cumsum(counts) - counts
    tiles_e = (counts + tm - 1) // tm
    tile_end = jnp.cumsum(tiles_e)
    tile_first = tile_end - tiles_e
    n_tiles = n_assign // tm + N_EXPERTS
    j = jnp.arange(n_tiles, dtype=i32)
    tile_e = jnp.minimum(jnp.searchsorted(tile_end, j, side="right"), N_EXPERTS - 1).astype(i32)
    local = j - tile_first[tile_e]
    tile_valid = jnp.clip(counts[tile_e] - local * tm, 0, tm).astype(i32)
    tile_c0 = jnp.where(tile_valid > 0, start[tile_e] + local * tm, 0).astype(i32)
    last_e = jnp.max(jnp.where(tile_valid > 0, tile_e, 0))
    tile_e = jnp.where(tile_valid > 0, tile_e, last_e).astype(i32)
    return order // TOP_K, rank, tile_e, tile_valid, tile_c0


def kernel(x, mem, g_mix, w_in, rpb_na, sinks, w_na_o, w_win_o, w_out, g_cross, g_mem, w_cq, w_ckv, w_co,
           g_moe, w_router, b_router, w_gate, b_gate, w_up, b_up, w_down, b_down, g_final):
    b, s, d = x.shape
    mem_len = mem.shape[1]
    t = b * s
    depth = w_in.shape[0]
    xc = x.reshape(t, d)
    mem2d = mem.reshape(b * mem_len, d)
    rope_c, rope_sa, rope_sb = _rope_tables(s)

    cols = _cols(d)
    n_attn = cols.end - cols.qa
    scale = HEAD_DIM ** -0.5
    o_qb, o_kb = 3 * NA_WIDTH, 3 * NA_WIDTH + WIN_Q_WIDTH

    for l in range(depth):
        w = w_in[l]
        w_in_l = jnp.concatenate([w[:, n_attn:], w[:, :NA_WIDTH] * scale, w[:, NA_WIDTH:o_qb],
                                  _swa_head_order(w[:, o_qb:o_kb] * scale, axis=1), w[:, o_kb:n_attn]],
                                 axis=1).astype(BF16)
        proj = _in_proj(xc, g_mix[l].reshape(1, d), w_in_l, rope_c, rope_sa, rope_sb, s)
        ya = _natten(proj, _natten_bias_table(rpb_na[l]), cols, b, s)
        yb = _swa(proj, sinks[l].astype(F32), cols, b, s)
        x1 = _merge(ya, yb, proj, xc, w_na_o[l].astype(BF16), _swa_head_order(w_win_o[l], axis=0).astype(BF16),
                    w_out[l].astype(BF16))
        memkv = _memkv(mem2d, g_mem[l].reshape(1, d), w_ckv[l].astype(BF16), mem_len)
        wr_hi = w_router[l].astype(BF16)
        wr_lo = (w_router[l] - wr_hi.astype(F32)).astype(BF16)
        x2, h3, top_w, top_idx = _cross(x1, g_cross[l].reshape(1, d), w_cq[l].astype(BF16), memkv,
                                        w_co[l].astype(BF16), g_moe[l].reshape(1, d),
                                        jnp.concatenate([wr_hi, wr_lo], axis=1), b_router[l].reshape(1, -1),
                                        s, mem_len)
        toks, pos, tile_e, tile_valid, tile_c0 = _route(top_idx, MOE_TM)
        ys = _moe_ffn(tile_e, tile_valid, tile_c0, toks, h3, w_gate[l], b_gate[l], w_up[l], b_up[l],
                      w_down[l], b_down[l])
        xc = _combine(pos, x2, ys, top_w, g_final.reshape(1, d), final=(l + 1 == depth))
    return xc.reshape(b, s, d)
```
